```python
import math
import jax, jax.numpy as jnp
from jax import lax
import numpy as np

D_MODEL = 1024
BATCH = 4
SEQ = 4096
DEPTH = 1

N_DIFF_HEADS = 8
DIFF_HEAD_DIM = 64
DIFF_V_DIM = 2 * DIFF_HEAD_DIM
DIFF_QK_WIDTH = N_DIFF_HEADS * 2 * DIFF_HEAD_DIM
DIFF_WIDTH = N_DIFF_HEADS * DIFF_V_DIM
Q_BLOCK = 128
CHUNK = 128
N_SG_GROUPS = 8
SG_WIDTH = D_MODEL
SG_GROUP_DIM = SG_WIDTH // N_SG_GROUPS
D_FF = 2816
CONV_WIDTH = 3
EPS = 1e-6
P_WIDTH = 2 * DIFF_QK_WIDTH + DIFF_WIDTH + 2 * SG_WIDTH + 2 * D_MODEL

kernel_name = "hybrid_diffattn_sgu_convffn_block"


def rmsnorm(x, g):
    xf = x.astype(jnp.float32)
    y = xf * lax.rsqrt(jnp.mean(xf * xf, axis=-1, keepdims=True) + EPS)
    return (y * g.astype(jnp.float32)).astype(x.dtype)


def lambda_init_for(layer):
    return 0.8 - 0.6 * math.exp(-0.3 * layer)


def diff_attention(q, k, v, lam):
    B, H, _, S, Dh = q.shape
    n_blocks = S // Q_BLOCK
    slopes = 2.0 ** (-8.0 * jnp.arange(1, H + 1, dtype=jnp.float32) / H)
    key_pos = jnp.arange(S, dtype=jnp.float32)

    def one_block(i):
        start = i * Q_BLOCK
        qb = lax.dynamic_slice_in_dim(q, start, Q_BLOCK, axis=3)
        s = jnp.einsum('bhmqd,bhmkd->bhmqk', qb, k).astype(jnp.float32)
        q_pos = (start + jnp.arange(Q_BLOCK)).astype(jnp.float32)
        dist = jnp.abs(q_pos[:, None] - key_pos[None, :])
        s = s - slopes[None, :, None, None, None] * dist[None, None, None]
        p = jax.nn.softmax(s, axis=-1)
        attn = p[:, :, 0] - lam * p[:, :, 1]
        return jnp.einsum('bhqk,bhkd->bhqd', attn.astype(v.dtype), v)

    o = lax.map(one_block, jnp.arange(n_blocks))
    return o.transpose(1, 0, 3, 2, 4).reshape(B, S, H, DIFF_V_DIM)


def depthwise_conv_seq(u, w, b):
    C = u.shape[-1]
    pad = CONV_WIDTH // 2
    y = lax.conv_general_dilated(u, w[:, None, :].astype(u.dtype), window_strides=(1,),
                                 padding=((pad, pad),),
                                 dimension_numbers=('NWC', 'WIO', 'NWC'),
                                 feature_group_count=C)
    return y + b


def setup_inputs(seed: int = 0) -> dict:
    key = jax.random.key(seed)
    ks = jax.random.split(key, 24)
    f32 = jnp.float32
    L = DEPTH

    def nrm(k, shape, scale):
        return jax.random.normal(k, shape, f32) * scale

    return {
        "x": jax.random.normal(ks[0], (BATCH, SEQ, D_MODEL), f32),
        "ln_mix_g": 1.0 + nrm(ks[1], (L, D_MODEL), 0.02),
        "w_in": nrm(ks[2], (L, D_MODEL, P_WIDTH), D_MODEL ** -0.5),
        "q_norm_g": 1.0 + nrm(ks[3], (L, DIFF_HEAD_DIM), 0.02),
        "k_norm_g": 1.0 + nrm(ks[4], (L, DIFF_HEAD_DIM), 0.02),
        "lambda_q1": nrm(ks[5], (L, DIFF_HEAD_DIM), 0.1),
        "lambda_k1": nrm(ks[6], (L, DIFF_HEAD_DIM), 0.1),
        "lambda_q2": nrm(ks[7], (L, DIFF_HEAD_DIM), 0.1),
        "lambda_k2": nrm(ks[8], (L, DIFF_HEAD_DIM), 0.1),
        "subln_g": 1.0 + nrm(ks[9], (L, DIFF_V_DIM), 0.02),
        "sg_norm_g": 1.0 + nrm(ks[10], (L, SG_WIDTH), 0.02),
        "sg_w": nrm(ks[11], (L, N_SG_GROUPS, CHUNK, CHUNK), CHUNK ** -0.5),
        "sg_b": 1.0 + nrm(ks[12], (L, N_SG_GROUPS, CHUNK), 0.1),
        "w_branch_a": nrm(ks[13], (L, DIFF_WIDTH, D_MODEL), DIFF_WIDTH ** -0.5),
        "w_branch_b": nrm(ks[14], (L, SG_WIDTH, D_MODEL), SG_WIDTH ** -0.5),
        "w_out": nrm(ks[15], (L, D_MODEL, D_MODEL), D_MODEL ** -0.5),
        "ln_ffn_g": 1.0 + nrm(ks[16], (L, D_MODEL), 0.02),
        "w_up": nrm(ks[17], (L, D_MODEL, 2 * D_FF), D_MODEL ** -0.5),
        "conv_w": nrm(ks[18], (L, CONV_WIDTH, 2 * D_FF), CONV_WIDTH ** -0.5),
        "conv_b": nrm(ks[19], (L, 2 * D_FF), 0.02),
        "w_down": nrm(ks[20], (L, D_FF, D_MODEL), D_FF ** -0.5),
    }


def reference(x, ln_mix_g, w_in, q_norm_g, k_norm_g, lambda_q1, lambda_k1, lambda_q2,
              lambda_k2, subln_g, sg_norm_g, sg_w, sg_b, w_branch_a, w_branch_b, w_out,
              ln_ffn_g, w_up, conv_w, conv_b, w_down):
    B, S, D = x.shape
    H, Dh = N_DIFF_HEADS, DIFF_HEAD_DIM
    G, Cg = N_SG_GROUPS, SG_GROUP_DIM
    n_chunks = S // CHUNK
    o_q, o_k = 0, DIFF_QK_WIDTH
    o_v = 2 * DIFF_QK_WIDTH
    o_u = o_v + DIFF_WIDTH
    o_sv = o_u + SG_WIDTH
    o_ga = o_sv + SG_WIDTH
    o_gb = o_ga + D_MODEL

    for l in range(DEPTH):
        lam_init = lambda_init_for(l)
        h = rmsnorm(x, ln_mix_g[l])
        p = h @ w_in[l]

        q = rmsnorm(p[..., o_q:o_k].reshape(B, S, H, 2, Dh), q_norm_g[l]) * (Dh ** -0.5)
        k = rmsnorm(p[..., o_k:o_v].reshape(B, S, H, 2, Dh), k_norm_g[l])
        v = p[..., o_v:o_u].reshape(B, S, H, DIFF_V_DIM)
        q = q.transpose(0, 2, 3, 1, 4)
        k = k.transpose(0, 2, 3, 1, 4)
        v = v.transpose(0, 2, 1, 3)
        lam = (jnp.exp(jnp.sum(lambda_q1[l].astype(jnp.float32) * lambda_k1[l].astype(jnp.float32)))
               - jnp.exp(jnp.sum(lambda_q2[l].astype(jnp.float32) * lambda_k2[l].astype(jnp.float32)))
               + lam_init)
        o_a = diff_attention(q, k, v, lam)
        o_a = (rmsnorm(o_a, subln_g[l]) * (1.0 - lam_init)).reshape(B, S, DIFF_WIDTH)

        u = jax.nn.gelu(p[..., o_u:o_sv])
        sv = rmsnorm(jax.nn.gelu(p[..., o_sv:o_ga]).reshape(B, S, G, Cg),
                     sg_norm_g[l].reshape(G, Cg))
        sv = sv.reshape(B, n_chunks, CHUNK, G, Cg)
        mixed = jnp.einsum('gts,bcsgd->bctgd', sg_w[l], sv) + sg_b[l].T[:, :, None]
        o_b = u * mixed.reshape(B, S, SG_WIDTH)

        merged = (jax.nn.sigmoid(p[..., o_ga:o_gb]) * (o_a @ w_branch_a[l])
                  + jax.nn.sigmoid(p[..., o_gb:]) * (o_b @ w_branch_b[l]))
        x = x + merged @ w_out[l]

        h2 = rmsnorm(x, ln_ffn_g[l])
        up = depthwise_conv_seq(h2 @ w_up[l], conv_w[l], conv_b[l])
        gate, val = up[..., :D_FF], up[..., D_FF:]
        x = x + (jax.nn.silu(gate) * val) @ w_down[l]
    return x
```

```python
import functools
import math

import jax
import jax.numpy as jnp
from jax import lax
from jax.experimental import pallas as pl
from jax.experimental.pallas import tpu as pltpu

F32 = jnp.float32
BF16 = jnp.bfloat16

D_MODEL = 1024
N_HEADS = 8
HEAD_DIM = 64
V_DIM = 2 * HEAD_DIM
CHUNK = 128
N_GROUPS = 8
GROUP_DIM = D_MODEL // N_GROUPS
D_FF = 2816
EPS = 1e-6
N_SECTIONS = 7

V7X_MXU_DIM = 256
V7X_VMEM_LIMIT = 56 * 1024 * 1024

PROJ_ROWS = 512
ATTN_Q_ROWS = 256
ATTN_K_ROWS = 512
MERGE_ROWS = 512
FFN_ROWS = 512
FFN_HALO = 16
FFN_COLS = 256


def _rms_rows(x, g):
    return x * lax.rsqrt(jnp.mean(x * x, axis=-1, keepdims=True) + EPS) * g


def _gelu_tanh(x):
    c = math.sqrt(2.0 / math.pi)
    return 0.5 * x * (1.0 + jnp.tanh(c * (x + 0.044715 * (x * x * x))))


def _sigmoid(x):
    return 1.0 / (1.0 + jnp.exp(-x))


def _group_mean_matrix(group):
    r = lax.broadcasted_iota(jnp.int32, (V7X_MXU_DIM, V7X_MXU_DIM), 0) // group
    c = lax.broadcasted_iota(jnp.int32, (V7X_MXU_DIM, V7X_MXU_DIM), 1) // group
    return jnp.where(r == c, 1.0 / group, 0.0).astype(BF16)


def _group_rms_scale(a, mean_mat):
    sq = (a * a).astype(BF16)
    ms = jnp.dot(sq, mean_mat, preferred_element_type=F32)
    return lax.rsqrt(ms + EPS)


def _proj_kernel(x_ref, g_ref, w_ref, qg_ref, kg_ref, sgg_ref,
                 q_ref, k_ref, v_ref, u_ref, sv_ref, ga_ref, gb_ref):
    h = _rms_rows(x_ref[...], g_ref[...]).astype(BF16)
    mean64 = _group_mean_matrix(HEAD_DIM)
    mean128 = _group_mean_matrix(GROUP_DIM)
    w = V7X_MXU_DIM

    def section(s, c):
        lo = s * D_MODEL + c * w
        return jnp.dot(h, w_ref[:, lo:lo + w], preferred_element_type=F32)

    for c in range(D_MODEL // w):
        cols = slice(c * w, (c + 1) * w)
        a = section(0, c)
        q_ref[:, cols] = (a * _group_rms_scale(a, mean64) * (qg_ref[:, cols] * HEAD_DIM ** -0.5)).astype(BF16)
        a = section(1, c)
        k_ref[:, cols] = (a * _group_rms_scale(a, mean64) * kg_ref[:, cols]).astype(BF16)
        v_ref[:, cols] = section(2, c).astype(BF16)
        u_ref[:, cols] = _gelu_tanh(section(3, c)).astype(BF16)
        a = _gelu_tanh(section(4, c))
        sv_ref[:, cols] = (a * _group_rms_scale(a, mean128) * sgg_ref[:, cols]).astype(BF16)
        ga_ref[:, cols] = _sigmoid(section(5, c)).astype(BF16)
        gb_ref[:, cols] = _sigmoid(section(6, c)).astype(BF16)


def _projection(x2, ln_g, w_in, q_g, k_g, sg_g):
    t = x2.shape[0]
    row = lambda i: (i, 0)
    fixed = lambda i: (0, 0)
    vec = pl.BlockSpec((1, D_MODEL), fixed)
    out = jax.ShapeDtypeStruct((t, D_MODEL), BF16)
    return pl.pallas_call(
        _proj_kernel,
        grid=(t // PROJ_ROWS,),
        in_specs=[
            pl.BlockSpec((PROJ_ROWS, D_MODEL), row),
            vec,
            pl.BlockSpec((D_MODEL, N_SECTIONS * D_MODEL), fixed, pipeline_mode=pl.Buffered(1)),
            vec, vec, vec,
        ],
        out_specs=[pl.BlockSpec((PROJ_ROWS, D_MODEL), row)] * N_SECTIONS,
        out_shape=[out] * N_SECTIONS,
        compiler_params=pltpu.CompilerParams(
            dimension_semantics=("arbitrary",), vmem_limit_bytes=V7X_VMEM_LIMIT),
        name="proj",
    )(x2, ln_g, w_in, q_g, k_g, sg_g)


def _attn_kernel(slopes_ref, lam_init_ref, q_ref, k_ref, v_ref, lamv_ref, subg_ref, o_ref):
    head = pl.program_id(1)
    qi = pl.program_id(2)
    tq, tk = ATTN_Q_ROWS, ATTN_K_ROWS
    seq = k_ref.shape[0]
    slope = slopes_ref[head]

    q = q_ref[...]
    lane = lax.broadcasted_iota(jnp.int32, q.shape, 1)
    zero = jnp.zeros_like(q)
    q_maps = (jnp.where(lane < HEAD_DIM, q, zero), jnp.where(lane >= HEAD_DIM, q, zero))
    rel = (lax.broadcasted_iota(jnp.int32, (tq, tk), 0)
           - lax.broadcasted_iota(jnp.int32, (tq, tk), 1)).astype(F32)

    def body(j, carry):
        start = pl.multiple_of(j * tk, tk)
        kb = k_ref[pl.ds(start, tk), :]
        vb = v_ref[pl.ds(start, tk), :]
        bias = slope * jnp.abs(rel + (qi * tq - j * tk).astype(F32))
        new = []
        for mp in range(2):
            m, l, acc = carry[mp]
            s = lax.dot_general(q_maps[mp], kb, (((1,), (1,)), ((), ())),
                                preferred_element_type=F32) - bias
            m_new = jnp.maximum(m, jnp.max(s, axis=-1, keepdims=True))
            alpha = jnp.exp(m - m_new)
            p = jnp.exp(s - m_new)
            l = alpha * l + jnp.sum(p, axis=-1, keepdims=True)
            acc = alpha * acc + jnp.dot(p.astype(BF16), vb, preferred_element_type=F32)
            new.append((m_new, l, acc))
        return tuple(new)

    init = tuple((jnp.full((tq, 1), -jnp.inf, F32), jnp.zeros((tq, 1), F32),
                  jnp.zeros((tq, V_DIM), F32)) for _ in range(2))
    (_, l1, a1), (_, l2, a2) = lax.fori_loop(0, seq // tk, body, init)

    lamv = lamv_ref[...]
    lam_init = lam_init_ref[0]
    lam = (jnp.exp(jnp.sum(lamv[0:1] * lamv[1:2], axis=-1, keepdims=True))
           - jnp.exp(jnp.sum(lamv[2:3] * lamv[3:4], axis=-1, keepdims=True)) + lam_init)
    o = a1 / l1 - lam * (a2 / l2)
    o_ref[...] = (_rms_rows(o, subg_ref[...]) * (1.0 - lam_init)).astype(BF16)


def _attention(q, k, v, lamv, sub_g, slopes, lam_init, batch, seq):
    smem = pl.BlockSpec(memory_space=pltpu.SMEM)
    nq = seq // ATTN_Q_ROWS
    k3 = k.reshape(batch, seq, D_MODEL)
    v3 = v.reshape(batch, seq, D_MODEL)
    return pl.pallas_call(
        _attn_kernel,
        grid=(batch, N_HEADS, nq),
        in_specs=[
            smem, smem,
            pl.BlockSpec((ATTN_Q_ROWS, V_DIM), lambda b, h, i: (b * nq + i, h)),
            pl.BlockSpec((None, seq, V_DIM), lambda b, h, i: (b, 0, h)),
            pl.BlockSpec((None, seq, V_DIM), lambda b, h, i: (b, 0, h)),
            pl.BlockSpec((4, HEAD_DIM), lambda b, h, i: (0, 0)),
            pl.BlockSpec((1, V_DIM), lambda b, h, i: (0, 0)),
        ],
        out_specs=pl.BlockSpec((ATTN_Q_ROWS, V_DIM), lambda b, h, i: (b * nq + i, h)),
        out_shape=jax.ShapeDtypeStruct((batch * seq, D_MODEL), BF16),
        compiler_params=pltpu.CompilerParams(
            dimension_semantics=("arbitrary", "arbitrary", "arbitrary"),
            vmem_limit_bytes=V7X_VMEM_LIMIT),
        name="diff_attn",
    )(slopes, lam_init, q, k3, v3, lamv, sub_g)


def _merge_kernel(x_ref, oa_ref, u_ref, sv_ref, ga_ref, gb_ref, sgw_ref, sgb_ref,
                  wa_ref, wb_ref, wo_ref, o_ref, ob_ref):
    for c in range(MERGE_ROWS // CHUNK):
        rows = slice(c * CHUNK, (c + 1) * CHUNK)
        for g in range(N_GROUPS):
            cols = slice(g * GROUP_DIM, (g + 1) * GROUP_DIM)
            mixed = jnp.dot(sgw_ref[g], sv_ref[rows, cols], preferred_element_type=F32)
            ob_ref[rows, cols] = (u_ref[rows, cols].astype(F32) * (mixed + sgb_ref[:, cols])).astype(BF16)
    ya = jnp.dot(oa_ref[...], wa_ref[...], preferred_element_type=F32)
    yb = jnp.dot(ob_ref[...], wb_ref[...], preferred_element_type=F32)
    merged = (ga_ref[...].astype(F32) * ya + gb_ref[...].astype(F32) * yb).astype(BF16)
    o_ref[...] = x_ref[...] + jnp.dot(merged, wo_ref[...], preferred_element_type=F32)


def _merge(x2, oa, u, sv, ga, gb, sgw, sgb_full, wa, wb, wo):
    t = x2.shape[0]
    row = lambda i: (i, 0)
    fixed2 = lambda i: (0, 0)
    tile = pl.BlockSpec((MERGE_ROWS, D_MODEL), row)
    wspec = pl.BlockSpec((D_MODEL, D_MODEL), fixed2, pipeline_mode=pl.Buffered(1))
    return pl.pallas_call(
        _merge_kernel,
        grid=(t // MERGE_ROWS,),
        in_specs=[
            tile, tile, tile, tile, tile, tile,
            pl.BlockSpec((N_GROUPS, CHUNK, CHUNK), lambda i: (0, 0, 0)),
            pl.BlockSpec((CHUNK, D_MODEL), fixed2),
            wspec, wspec, wspec,
        ],
        out_specs=tile,
        out_shape=jax.ShapeDtypeStruct((t, D_MODEL), F32),
        scratch_shapes=[pltpu.VMEM((MERGE_ROWS, D_MODEL), BF16)],
        compiler_params=pltpu.CompilerParams(
            dimension_semantics=("arbitrary",), vmem_limit_bytes=V7X_VMEM_LIMIT),
        name="sgu_merge",
    )(x2, oa, u, sv, ga, gb, sgw, sgb_full, wa, wb, wo)


def _ffn_kernel(x_ref, prev_ref, next_ref, g_ref, wup_ref, cw_ref, cb_ref, wdn_ref,
                o_ref, h_ref, act_ref, *, blocks_per_seq):
    i = pl.program_id(0)
    tm, halo = FFN_ROWS, FFN_HALO
    g = g_ref[...]
    x = x_ref[...]
    keep_prev = (i % blocks_per_seq != 0).astype(F32)
    keep_next = ((i + 1) % blocks_per_seq != 0).astype(F32)
    h_ref[0:halo, :] = (_rms_rows(prev_ref[...], g) * keep_prev).astype(BF16)
    h_ref[halo:halo + tm, :] = _rms_rows(x, g).astype(BF16)
    h_ref[halo + tm:, :] = (_rms_rows(next_ref[...], g) * keep_next).astype(BF16)
    h = h_ref[...]
    ext = tm + 2 * halo

    def conv(cols):
        up = jnp.dot(h, wup_ref[:, cols], preferred_element_type=F32)
        y = (pltpu.roll(up, 1, 0) * cw_ref[0:1, cols] + up * cw_ref[1:2, cols]
             + pltpu.roll(up, ext - 1, 0) * cw_ref[2:3, cols])
        return y[halo:halo + tm] + cb_ref[:, cols]

    for c in range(D_FF // FFN_COLS):
        gate = conv(slice(c * FFN_COLS, (c + 1) * FFN_COLS))
        val = conv(slice(D_FF + c * FFN_COLS, D_FF + (c + 1) * FFN_COLS))
        act_ref[:, c * FFN_COLS:(c + 1) * FFN_COLS] = (gate * _sigmoid(gate) * val).astype(BF16)
    o_ref[...] = x + jnp.dot(act_ref[...], wdn_ref[...], preferred_element_type=F32)


def _ffn(x2, ln_g, wup, conv_w, conv_b, wdn, seq):
    t = x2.shape[0]
    tm, halo = FFN_ROWS, FFN_HALO
    r = tm // halo
    last = t // halo - 1
    fixed = lambda i: (0, 0)
    return pl.pallas_call(
        functools.partial(_ffn_kernel, blocks_per_seq=seq // tm),
        grid=(t // tm,),
        in_specs=[
            pl.BlockSpec((tm, D_MODEL), lambda i: (i, 0)),
            pl.BlockSpec((halo, D_MODEL), lambda i: (jnp.maximum(i * r - 1, 0), 0)),
            pl.BlockSpec((halo, D_MODEL), lambda i: (jnp.minimum((i + 1) * r, last), 0)),
            pl.BlockSpec((1, D_MODEL), fixed),
            pl.BlockSpec((D_MODEL, 2 * D_FF), fixed, pipeline_mode=pl.Buffered(1)),
            pl.BlockSpec((3, 2 * D_FF), fixed),
            pl.BlockSpec((1, 2 * D_FF), fixed),
            pl.BlockSpec((D_FF, D_MODEL), fixed, pipeline_mode=pl.Buffered(1)),
        ],
        out_specs=pl.BlockSpec((tm, D_MODEL), lambda i: (i, 0)),
        out_shape=jax.ShapeDtypeStruct((t, D_MODEL), F32),
        scratch_shapes=[pltpu.VMEM((tm + 2 * halo, D_MODEL), BF16),
                        pltpu.VMEM((tm, D_FF), BF16)],
        compiler_params=pltpu.CompilerParams(
            dimension_semantics=("arbitrary",), vmem_limit_bytes=V7X_VMEM_LIMIT),
        name="conv_ffn",
    )(x2, x2, x2, ln_g, wup, conv_w, conv_b, wdn)


def kernel(x, ln_mix_g, w_in, q_norm_g, k_norm_g, lambda_q1, lambda_k1, lambda_q2, lambda_k2,
           subln_g, sg_norm_g, sg_w, sg_b, w_branch_a, w_branch_b, w_out, ln_ffn_g, w_up,
           conv_w, conv_b, w_down):
    batch, seq, d = x.shape
    depth = w_in.shape[0]
    slopes = 2.0 ** (-8.0 * jnp.arange(1, N_HEADS + 1, dtype=F32) / N_HEADS)
    x2 = x.reshape(batch * seq, d)
    for l in range(depth):
        lam_init = jnp.full((1,), 0.8 - 0.6 * math.exp(-0.3 * l), F32)
        q, k, v, u, sv, ga, gb = _projection(
            x2, ln_mix_g[l][None], w_in[l].astype(BF16),
            jnp.tile(q_norm_g[l], 2 * N_HEADS)[None], jnp.tile(k_norm_g[l], 2 * N_HEADS)[None],
            sg_norm_g[l][None])
        lamv = jnp.stack([lambda_q1[l], lambda_k1[l], lambda_q2[l], lambda_k2[l]])
        oa = _attention(q, k, v, lamv, subln_g[l][None], slopes, lam_init, batch, seq)
        sgb_full = jnp.repeat(sg_b[l].T, GROUP_DIM, axis=1)
        x2 = _merge(x2, oa, u, sv, ga, gb, sg_w[l].astype(BF16), sgb_full,
                    w_branch_a[l].astype(BF16), w_branch_b[l].astype(BF16), w_out[l].astype(BF16))
        x2 = _ffn(x2, ln_ffn_g[l][None], w_up[l].astype(BF16), conv_w[l], conv_b[l][None],
                  w_down[l].astype(BF16), seq)
    return x2.reshape(batch, seq, d)
```

```python
import functools
import math

import jax
import jax.numpy as jnp
from jax import lax
from jax.experimental import pallas as pl
from jax.experimental.pallas import tpu as pltpu

F32 = jnp.float32
BF16 = jnp.bfloat16

D_MODEL = 1024
N_HEADS = 8
HEAD_DIM = 64
V_DIM = 2 * HEAD_DIM
CHUNK = 128
N_GROUPS = 8
GROUP_DIM = D_MODEL // N_GROUPS
D_FF = 2816
EPS = 1e-6
N_SECTIONS = 7

V7X_MXU_DIM = 256
V7X_VMEM_LIMIT = 56 * 1024 * 1024

PROJ_ROWS = 512
ATTN_BLOCK = 256
ATTN_VT_ROWS = V_DIM + 16
ATTN_SLOTS = 4
LOG2E = math.log2(math.e)
MERGE_ROWS = 512
FFN_ROWS = 512
FFN_HALO = 16
FFN_COLS = 256


def _rms_rows(x, g):
    return x * lax.rsqrt(jnp.mean(x * x, axis=-1, keepdims=True) + EPS) * g


def _gelu_tanh(x):
    c = math.sqrt(2.0 / math.pi)
    return 0.5 * x * (1.0 + jnp.tanh(c * (x + 0.044715 * (x * x * x))))


def _sigmoid(x):
    return 1.0 / (1.0 + jnp.exp(-x))


def _group_mean_matrix(group):
    r = lax.broadcasted_iota(jnp.int32, (V7X_MXU_DIM, V7X_MXU_DIM), 0) // group
    c = lax.broadcasted_iota(jnp.int32, (V7X_MXU_DIM, V7X_MXU_DIM), 1) // group
    return jnp.where(r == c, 1.0 / group, 0.0).astype(BF16)


def _group_rms_scale(a, mean_mat):
    sq = (a * a).astype(BF16)
    ms = jnp.dot(sq, mean_mat, preferred_element_type=F32)
    return lax.rsqrt(ms + EPS)


def _proj_kernel(x_ref, g_ref, w_ref, qg_ref, kg_ref, sgg_ref,
                 q_ref, k_ref, v_ref, u_ref, sv_ref, ga_ref, gb_ref):
    h = _rms_rows(x_ref[...], g_ref[...]).astype(BF16)
    mean64 = _group_mean_matrix(HEAD_DIM)
    mean128 = _group_mean_matrix(GROUP_DIM)
    w = V7X_MXU_DIM

    def section(s, c):
        lo = s * D_MODEL + c * w
        return jnp.dot(h, w_ref[:, lo:lo + w], preferred_element_type=F32)

    for c in range(D_MODEL // w):
        cols = slice(c * w, (c + 1) * w)
        a = section(0, c)
        q_ref[:, cols] = (a * _group_rms_scale(a, mean64) * (qg_ref[:, cols] * (HEAD_DIM ** -0.5 * LOG2E))).astype(BF16)
        a = section(1, c)
        k_ref[:, cols] = (a * _group_rms_scale(a, mean64) * kg_ref[:, cols]).astype(BF16)
        v_ref[:, cols] = section(2, c).astype(BF16)
        u_ref[:, cols] = _gelu_tanh(section(3, c)).astype(BF16)
        a = _gelu_tanh(section(4, c))
        sv_ref[:, cols] = (a * _group_rms_scale(a, mean128) * sgg_ref[:, cols]).astype(BF16)
        ga_ref[:, cols] = _sigmoid(section(5, c)).astype(BF16)
        gb_ref[:, cols] = _sigmoid(section(6, c)).astype(BF16)


def _projection(x2, ln_g, w_in, q_g, k_g, sg_g):
    t = x2.shape[0]
    row = lambda i: (i, 0)
    fixed = lambda i: (0, 0)
    vec = pl.BlockSpec((1, D_MODEL), fixed)
    out = jax.ShapeDtypeStruct((t, D_MODEL), BF16)
    return pl.pallas_call(
        _proj_kernel,
        grid=(t // PROJ_ROWS,),
        in_specs=[
            pl.BlockSpec((PROJ_ROWS, D_MODEL), row),
            vec,
            pl.BlockSpec((D_MODEL, N_SECTIONS * D_MODEL), fixed, pipeline_mode=pl.Buffered(1)),
            vec, vec, vec,
        ],
        out_specs=[pl.BlockSpec((PROJ_ROWS, D_MODEL), row)] * N_SECTIONS,
        out_shape=[out] * N_SECTIONS,
        compiler_params=pltpu.CompilerParams(
            dimension_semantics=("arbitrary",), vmem_limit_bytes=V7X_VMEM_LIMIT),
        name="proj",
    )(x2, ln_g, w_in, q_g, k_g, sg_g)


def _split3(c):
    hi = c.astype(BF16)
    mid = (c - hi.astype(F32)).astype(BF16)
    lo = (c - hi.astype(F32) - mid.astype(F32)).astype(BF16)
    return hi, mid, lo


def _attn_kernel(slopes_ref, lam_init_ref, q_ref, k_ref, v_ref, lamv_ref, subg_ref, o_ref,
                 k1_ref, k2_ref, vt_ref, qv_ref, s_ref, p_ref, alpha_ref, m_ref, acc_ref):
    head = pl.program_id(1)
    qi = pl.program_id(2)
    blk = ATTN_BLOCK
    seq = k_ref.shape[0]
    nblk = seq // blk
    c = jnp.full((1, 1), slopes_ref[head] * LOG2E, F32)
    c_parts = _split3(c)

    @pl.when(qi == 0)
    def _():
        lane = lax.broadcasted_iota(jnp.int32, (seq, V_DIM), 1)
        pos = (lax.broadcasted_iota(jnp.int32, (seq, V_DIM), 0) % blk).astype(F32)
        for mp, ref in enumerate((k1_ref, k2_ref)):
            a = lane - (HEAD_DIM if mp == 0 else 0)
            aug = jnp.where(a == 0, -c_parts[0].astype(F32),
                  jnp.where(a == 1, -c_parts[1].astype(F32),
                  jnp.where(a == 2, -c_parts[2].astype(F32),
                  jnp.where((a >= 3) & (a < 6), pos, 0.0))))
            own = (lane < HEAD_DIM) if mp == 0 else (lane >= HEAD_DIM)
            ref[...] = jnp.where(own, k_ref[...].astype(F32), aug).astype(BF16)
        step = 512
        for s0 in range(0, seq, step):
            vt_ref[0:V_DIM, s0:s0 + step] = v_ref[s0:s0 + step, :].astype(F32).T.astype(BF16)
        r = lax.broadcasted_iota(jnp.int32, (ATTN_VT_ROWS - V_DIM, seq), 0)
        vt_ref[V_DIM:, :] = jnp.where(r == 0, 1.0, 0.0).astype(BF16)

    qt = q_ref[...].astype(F32).T
    row = lax.broadcasted_iota(jnp.int32, (HEAD_DIM, blk), 0)
    il = lax.broadcasted_iota(jnp.int32, (HEAD_DIM, blk), 1).astype(F32)
    aug_lo = jnp.where(row < 3, il,
             jnp.where(row == 3, c_parts[0].astype(F32),
             jnp.where(row == 4, c_parts[1].astype(F32),
             jnp.where(row == 5, c_parts[2].astype(F32), 0.0))))
    zeros = jnp.zeros((HEAD_DIM, blk), F32)

    def q_variants(aug):
        return (jnp.concatenate([qt[0:HEAD_DIM], aug], axis=0).astype(BF16),
                jnp.concatenate([aug, qt[HEAD_DIM:]], axis=0).astype(BF16))

    def key_block(n):
        n = jnp.clip(n, 0, nblk - 1)
        return jnp.where(n == 0, qi, n - (n <= qi).astype(jnp.int32))

    def stage_scores(j, qts, slot, bias=None):
        start = pl.multiple_of(j * blk, blk)
        for mp, kref in enumerate((k1_ref, k2_ref)):
            s = jnp.dot(kref[pl.ds(start, blk), :], qts[mp], preferred_element_type=F32)
            s_ref[slot, mp] = s if bias is None else s + bias

    def stage_softmax(j, slot, pslot):
        const = -(c_blk * jnp.abs(qi - j).astype(F32))
        for mp in range(2):
            s = s_ref[slot, mp]
            m_old = m_ref[mp]
            m_new = jnp.maximum(m_old, jnp.max(s, axis=0, keepdims=True) + const)
            p_ref[pslot, mp] = jnp.exp2(s - (m_new - const)).astype(BF16)
            alpha_ref[pslot, mp] = jnp.exp2(m_old - m_new)
            m_ref[mp] = m_new

    def stage_values(j, pslot):
        start = pl.multiple_of(j * blk, blk)
        vt = vt_ref[:, pl.ds(start, blk)]
        for mp in range(2):
            acc_ref[mp] = (acc_ref[mp] * alpha_ref[pslot, mp]
                           + jnp.dot(vt, p_ref[pslot, mp], preferred_element_type=F32))

    c_blk = c * float(blk)
    lo = q_variants(aug_lo)
    hi = q_variants(-aug_lo)
    for mp in range(2):
        qv_ref[0, mp] = lo[mp]
        qv_ref[1, mp] = hi[mp]
    m_ref[...] = jnp.full(m_ref.shape, -jnp.inf, F32)
    acc_ref[...] = jnp.zeros(acc_ref.shape, F32)
    p_ref[1] = jnp.zeros(p_ref.shape[1:], BF16)
    alpha_ref[1] = jnp.zeros(alpha_ref.shape[1:], F32)

    d = lax.broadcasted_iota(jnp.int32, (blk, blk), 0) - lax.broadcasted_iota(jnp.int32, (blk, blk), 1)
    stage_scores(qi, q_variants(zeros), 0, -(c * jnp.abs(d).astype(F32)))
    j1 = key_block(1)
    stage_scores(j1, (qv_ref[(j1 > qi).astype(jnp.int32), 0], qv_ref[(j1 > qi).astype(jnp.int32), 1]), 1)

    def steps(it, carry):
        for u in range(ATTN_SLOTS):
            n = it * ATTN_SLOTS + u
            stage_values(key_block(n - 1), (u + 1) % 2)
            stage_softmax(key_block(n), u, u % 2)
            ja = key_block(n + 2)
            side = (ja > qi).astype(jnp.int32)
            stage_scores(ja, (qv_ref[side, 0], qv_ref[side, 1]), (u + 2) % ATTN_SLOTS)
        return carry

    lax.fori_loop(0, nblk // ATTN_SLOTS, steps, 0)
    stage_values(key_block(nblk - 1), (nblk - 1) % 2)

    lamv = lamv_ref[...]
    lam_init = lam_init_ref[0]
    lam = (jnp.exp(jnp.sum(lamv[0:1] * lamv[1:2], axis=-1, keepdims=True))
           - jnp.exp(jnp.sum(lamv[2:3] * lamv[3:4], axis=-1, keepdims=True)) + lam_init)
    a1 = acc_ref[0]
    a2 = acc_ref[1]
    ot = a1[0:V_DIM] / a1[V_DIM:V_DIM + 1] - lam * (a2[0:V_DIM] / a2[V_DIM:V_DIM + 1])
    ot = ot * lax.rsqrt(jnp.mean(ot * ot, axis=0, keepdims=True) + EPS)
    o_ref[...] = (ot.T * (subg_ref[...] * (1.0 - lam_init))).astype(BF16)


def _attention(q, k, v, lamv, sub_g, slopes, lam_init, batch, seq):
    smem = pl.BlockSpec(memory_space=pltpu.SMEM)
    blk = ATTN_BLOCK
    nq = seq // blk
    k3 = k.reshape(batch, seq, D_MODEL)
    v3 = v.reshape(batch, seq, D_MODEL)
    return pl.pallas_call(
        _attn_kernel,
        grid=(batch, N_HEADS, nq),
        in_specs=[
            smem, smem,
            pl.BlockSpec((blk, V_DIM), lambda b, h, i: (b * nq + i, h)),
            pl.BlockSpec((None, seq, V_DIM), lambda b, h, i: (b, 0, h)),
            pl.BlockSpec((None, seq, V_DIM), lambda b, h, i: (b, 0, h)),
            pl.BlockSpec((4, HEAD_DIM), lambda b, h, i: (0, 0)),
            pl.BlockSpec((1, V_DIM), lambda b, h, i: (0, 0)),
        ],
        out_specs=pl.BlockSpec((blk, V_DIM), lambda b, h, i: (b * nq + i, h)),
        out_shape=jax.ShapeDtypeStruct((batch * seq, D_MODEL), BF16),
        scratch_shapes=[
            pltpu.VMEM((seq, V_DIM), BF16),
            pltpu.VMEM((seq, V_DIM), BF16),
            pltpu.VMEM((ATTN_VT_ROWS, seq), BF16),
            pltpu.VMEM((2, 2, V_DIM, blk), BF16),
            pltpu.VMEM((ATTN_SLOTS, 2, blk, blk), F32),
            pltpu.VMEM((2, 2, blk, blk), BF16),
            pltpu.VMEM((2, 2, 1, blk), F32),
            pltpu.VMEM((2, 1, blk), F32),
            pltpu.VMEM((2, ATTN_VT_ROWS, blk), F32),
        ],
        compiler_params=pltpu.CompilerParams(
            dimension_semantics=("arbitrary", "arbitrary", "arbitrary"),
            vmem_limit_bytes=V7X_VMEM_LIMIT),
        name="diff_attn",
    )(slopes, lam_init, q, k3, v3, lamv, sub_g)


def _merge_kernel(x_ref, oa_ref, u_ref, sv_ref, ga_ref, gb_ref, sgw_ref, sgb_ref,
                  wa_ref, wb_ref, wo_ref, o_ref, ob_ref):
    for c in range(MERGE_ROWS // CHUNK):
        rows = slice(c * CHUNK, (c + 1) * CHUNK)
        for g in range(N_GROUPS):
            cols = slice(g * GROUP_DIM, (g + 1) * GROUP_DIM)
            mixed = jnp.dot(sgw_ref[g], sv_ref[rows, cols], preferred_element_type=F32)
            ob_ref[rows, cols] = (u_ref[rows, cols].astype(F32) * (mixed + sgb_ref[:, cols])).astype(BF16)
    ya = jnp.dot(oa_ref[...], wa_ref[...], preferred_element_type=F32)
    yb = jnp.dot(ob_ref[...], wb_ref[...], preferred_element_type=F32)
    merged = (ga_ref[...].astype(F32) * ya + gb_ref[...].astype(F32) * yb).astype(BF16)
    o_ref[...] = x_ref[...] + jnp.dot(merged, wo_ref[...], preferred_element_type=F32)


def _merge(x2, oa, u, sv, ga, gb, sgw, sgb_full, wa, wb, wo):
    t = x2.shape[0]
    row = lambda i: (i, 0)
    fixed2 = lambda i: (0, 0)
    tile = pl.BlockSpec((MERGE_ROWS, D_MODEL), row)
    wspec = pl.BlockSpec((D_MODEL, D_MODEL), fixed2, pipeline_mode=pl.Buffered(1))
    return pl.pallas_call(
        _merge_kernel,
        grid=(t // MERGE_ROWS,),
        in_specs=[
            tile, tile, tile, tile, tile, tile,
            pl.BlockSpec((N_GROUPS, CHUNK, CHUNK), lambda i: (0, 0, 0)),
            pl.BlockSpec((CHUNK, D_MODEL), fixed2),
            wspec, wspec, wspec,
        ],
        out_specs=tile,
        out_shape=jax.ShapeDtypeStruct((t, D_MODEL), F32),
        scratch_shapes=[pltpu.VMEM((MERGE_ROWS, D_MODEL), BF16)],
        compiler_params=pltpu.CompilerParams(
            dimension_semantics=("arbitrary",), vmem_limit_bytes=V7X_VMEM_LIMIT),
        name="sgu_merge",
    )(x2, oa, u, sv, ga, gb, sgw, sgb_full, wa, wb, wo)


def _ffn_kernel(x_ref, prev_ref, next_ref, g_ref, wup_ref, cw_ref, cb_ref, wdn_ref,
                o_ref, h_ref, act_ref, *, blocks_per_seq):
    i = pl.program_id(0)
    tm, halo = FFN_ROWS, FFN_HALO
    g = g_ref[...]
    x = x_ref[...]
    keep_prev = (i % blocks_per_seq != 0).astype(F32)
    keep_next = ((i + 1) % blocks_per_seq != 0).astype(F32)
    h_ref[0:halo, :] = (_rms_rows(prev_ref[...], g) * keep_prev).astype(BF16)
    h_ref[halo:halo + tm, :] = _rms_rows(x, g).astype(BF16)
    h_ref[halo + tm:, :] = (_rms_rows(next_ref[...], g) * keep_next).astype(BF16)
    h = h_ref[...]
    ext = tm + 2 * halo

    def conv(cols):
        up = jnp.dot(h, wup_ref[:, cols], preferred_element_type=F32)
        y = (pltpu.roll(up, 1, 0) * cw_ref[0:1, cols] + up * cw_ref[1:2, cols]
             + pltpu.roll(up, ext - 1, 0) * cw_ref[2:3, cols])
        return y[halo:halo + tm] + cb_ref[:, cols]

    for c in range(D_FF // FFN_COLS):
        gate = conv(slice(c * FFN_COLS, (c + 1) * FFN_COLS))
        val = conv(slice(D_FF + c * FFN_COLS, D_FF + (c + 1) * FFN_COLS))
        act_ref[:, c * FFN_COLS:(c + 1) * FFN_COLS] = (gate * _sigmoid(gate) * val).astype(BF16)
    o_ref[...] = x + jnp.dot(act_ref[...], wdn_ref[...], preferred_element_type=F32)


def _ffn(x2, ln_g, wup, conv_w, conv_b, wdn, seq):
    t = x2.shape[0]
    tm, halo = FFN_ROWS, FFN_HALO
    r = tm // halo
    last = t // halo - 1
    fixed = lambda i: (0, 0)
    return pl.pallas_call(
        functools.partial(_ffn_kernel, blocks_per_seq=seq // tm),
        grid=(t // tm,),
        in_specs=[
            pl.BlockSpec((tm, D_MODEL), lambda i: (i, 0)),
            pl.BlockSpec((halo, D_MODEL), lambda i: (jnp.maximum(i * r - 1, 0), 0)),
            pl.BlockSpec((halo, D_MODEL), lambda i: (jnp.minimum((i + 1) * r, last), 0)),
            pl.BlockSpec((1, D_MODEL), fixed),
            pl.BlockSpec((D_MODEL, 2 * D_FF), fixed, pipeline_mode=pl.Buffered(1)),
            pl.BlockSpec((3, 2 * D_FF), fixed),
            pl.BlockSpec((1, 2 * D_FF), fixed),
            pl.BlockSpec((D_FF, D_MODEL), fixed, pipeline_mode=pl.Buffered(1)),
        ],
        out_specs=pl.BlockSpec((tm, D_MODEL), lambda i: (i, 0)),
        out_shape=jax.ShapeDtypeStruct((t, D_MODEL), F32),
        scratch_shapes=[pltpu.VMEM((tm + 2 * halo, D_MODEL), BF16),
                        pltpu.VMEM((tm, D_FF), BF16)],
        compiler_params=pltpu.CompilerParams(
            dimension_semantics=("arbitrary",), vmem_limit_bytes=V7X_VMEM_LIMIT),
        name="conv_ffn",
    )(x2, x2, x2, ln_g, wup, conv_w, conv_b, wdn)


def kernel(x, ln_mix_g, w_in, q_norm_g, k_norm_g, lambda_q1, lambda_k1, lambda_q2, lambda_k2,
           subln_g, sg_norm_g, sg_w, sg_b, w_branch_a, w_branch_b, w_out, ln_ffn_g, w_up,
           conv_w, conv_b, w_down):
    batch, seq, d = x.shape
    depth = w_in.shape[0]
    slopes = 2.0 ** (-8.0 * jnp.arange(1, N_HEADS + 1, dtype=F32) / N_HEADS)
    x2 = x.reshape(batch * seq, d)
    for l in range(depth):
        lam_init = jnp.full((1,), 0.8 - 0.6 * math.exp(-0.3 * l), F32)
        q, k, v, u, sv, ga, gb = _projection(
            x2, ln_mix_g[l][None], w_in[l].astype(BF16),
            jnp.tile(q_norm_g[l], 2 * N_HEADS)[None], jnp.tile(k_norm_g[l], 2 * N_HEADS)[None],
            sg_norm_g[l][None])
        lamv = jnp.stack([lambda_q1[l], lambda_k1[l], lambda_q2[l], lambda_k2[l]])
        oa = _attention(q, k, v, lamv, subln_g[l][None], slopes, lam_init, batch, seq)
        sgb_full = jnp.repeat(sg_b[l].T, GROUP_DIM, axis=1)
        x2 = _merge(x2, oa, u, sv, ga, gb, sg_w[l].astype(BF16), sgb_full,
                    w_branch_a[l].astype(BF16), w_branch_b[l].astype(BF16), w_out[l].astype(BF16))
        x2 = _ffn(x2, ln_ffn_g[l][None], w_up[l].astype(BF16), conv_w[l], conv_b[l][None],
                  w_down[l].astype(BF16), seq)
    return x2.reshape(batch, seq, d)
```

```python
import functools
import math

import jax
import jax.numpy as jnp
from jax import lax
from jax.experimental import pallas as pl
from jax.experimental.pallas import tpu as pltpu

F32 = jnp.float32
BF16 = jnp.bfloat16

D_MODEL = 1024
N_HEADS = 8
HEAD_DIM = 64
V_DIM = 2 * HEAD_DIM
CHUNK = 128
N_GROUPS = 8
GROUP_DIM = D_MODEL // N_GROUPS
D_FF = 2816
EPS = 1e-6
N_SECTIONS = 7

V7X_MXU_DIM = 256
V7X_VMEM_LIMIT = 56 * 1024 * 1024

PROJ_ROWS = 512
ATTN_BLOCK = 256
ATTN_VT_ROWS = V_DIM + 16
ATTN_SLOTS = 4
LOG2E = math.log2(math.e)
MERGE_ROWS = 512
FFN_ROWS = 512
FFN_HALO = 16
FFN_COLS = 256


def _rms_rows(x, g):
    return x * lax.rsqrt(jnp.mean(x * x, axis=-1, keepdims=True) + EPS) * g


def _gelu_tanh(x):
    c = math.sqrt(2.0 / math.pi)
    return 0.5 * x * (1.0 + jnp.tanh(c * (x + 0.044715 * (x * x * x))))


def _sigmoid(x):
    return 1.0 / (1.0 + jnp.exp(-x))


def _group_mean_matrix(group):
    r = lax.broadcasted_iota(jnp.int32, (V7X_MXU_DIM, V7X_MXU_DIM), 0) // group
    c = lax.broadcasted_iota(jnp.int32, (V7X_MXU_DIM, V7X_MXU_DIM), 1) // group
    return jnp.where(r == c, 1.0 / group, 0.0).astype(BF16)


def _group_rms_scale(a, mean_mat):
    sq = (a * a).astype(BF16)
    ms = jnp.dot(sq, mean_mat, preferred_element_type=F32)
    return lax.rsqrt(ms + EPS)


def _proj_kernel(x_ref, g_ref, w_ref, qg_ref, kg_ref, sgg_ref,
                 q_ref, k_ref, v_ref, u_ref, sv_ref, ga_ref, gb_ref):
    h = _rms_rows(x_ref[...], g_ref[...]).astype(BF16)
    mean64 = _group_mean_matrix(HEAD_DIM)
    mean128 = _group_mean_matrix(GROUP_DIM)
    w = V7X_MXU_DIM

    def section(s, c):
        lo = s * D_MODEL + c * w
        return jnp.dot(h, w_ref[:, lo:lo + w], preferred_element_type=F32)

    for c in range(D_MODEL // w):
        cols = slice(c * w, (c + 1) * w)
        a = section(0, c)
        q_ref[:, cols] = (a * _group_rms_scale(a, mean64) * (qg_ref[:, cols] * (HEAD_DIM ** -0.5 * LOG2E))).astype(BF16)
        a = section(1, c)
        k_ref[:, cols] = (a * _group_rms_scale(a, mean64) * kg_ref[:, cols]).astype(BF16)
        v_ref[:, cols] = section(2, c).astype(BF16)
        u_ref[:, cols] = _gelu_tanh(section(3, c)).astype(BF16)
        a = _gelu_tanh(section(4, c))
        sv_ref[:, cols] = (a * _group_rms_scale(a, mean128) * sgg_ref[:, cols]).astype(BF16)
        ga_ref[:, cols] = _sigmoid(section(5, c)).astype(BF16)
        gb_ref[:, cols] = _sigmoid(section(6, c)).astype(BF16)


def _projection(x2, ln_g, w_in, q_g, k_g, sg_g):
    t = x2.shape[0]
    row = lambda i: (i, 0)
    fixed = lambda i: (0, 0)
    vec = pl.BlockSpec((1, D_MODEL), fixed)
    out = jax.ShapeDtypeStruct((t, D_MODEL), BF16)
    return pl.pallas_call(
        _proj_kernel,
        grid=(t // PROJ_ROWS,),
        in_specs=[
            pl.BlockSpec((PROJ_ROWS, D_MODEL), row),
            vec,
            pl.BlockSpec((D_MODEL, N_SECTIONS * D_MODEL), fixed, pipeline_mode=pl.Buffered(1)),
            vec, vec, vec,
        ],
        out_specs=[pl.BlockSpec((PROJ_ROWS, D_MODEL), row)] * N_SECTIONS,
        out_shape=[out] * N_SECTIONS,
        compiler_params=pltpu.CompilerParams(
            dimension_semantics=("arbitrary",), vmem_limit_bytes=V7X_VMEM_LIMIT),
        name="proj",
    )(x2, ln_g, w_in, q_g, k_g, sg_g)


def _split3(c):
    hi = c.astype(BF16)
    mid = (c - hi.astype(F32)).astype(BF16)
    lo = (c - hi.astype(F32) - mid.astype(F32)).astype(BF16)
    return hi, mid, lo


def _attn_kernel(slopes_ref, lam_init_ref, q_ref, k_ref, v_ref, lamv_ref, subg_ref, o_ref,
                 k1_ref, k2_ref, vt_ref, qv_ref, s_ref, p_ref, alpha_ref, m_ref, acc_ref):
    head = pl.program_id(1)
    qi = pl.program_id(2)
    blk = ATTN_BLOCK
    seq = k_ref.shape[0]
    nblk = seq // blk
    c = jnp.full((1, 1), slopes_ref[head] * LOG2E, F32)
    c_parts = _split3(c)

    @pl.when(qi == 0)
    def _():
        lane = lax.broadcasted_iota(jnp.int32, (seq, V_DIM), 1)
        pos = (lax.broadcasted_iota(jnp.int32, (seq, V_DIM), 0) % blk).astype(F32)
        for mp, ref in enumerate((k1_ref, k2_ref)):
            a = lane - (HEAD_DIM if mp == 0 else 0)
            aug = jnp.where(a == 0, -c_parts[0].astype(F32),
                  jnp.where(a == 1, -c_parts[1].astype(F32),
                  jnp.where(a == 2, -c_parts[2].astype(F32),
                  jnp.where((a >= 3) & (a < 6), pos, 0.0))))
            own = (lane < HEAD_DIM) if mp == 0 else (lane >= HEAD_DIM)
            ref[...] = jnp.where(own, k_ref[...].astype(F32), aug).astype(BF16)
        step = 512
        for s0 in range(0, seq, step):
            vt_ref[0:V_DIM, s0:s0 + step] = v_ref[s0:s0 + step, :].astype(F32).T.astype(BF16)
        r = lax.broadcasted_iota(jnp.int32, (ATTN_VT_ROWS - V_DIM, seq), 0)
        vt_ref[V_DIM:, :] = jnp.where(r == 0, 1.0, 0.0).astype(BF16)

    qt = q_ref[...].astype(F32).T
    row = lax.broadcasted_iota(jnp.int32, (HEAD_DIM, blk), 0)
    il = lax.broadcasted_iota(jnp.int32, (HEAD_DIM, blk), 1).astype(F32)
    aug_lo = jnp.where(row < 3, il,
             jnp.where(row == 3, c_parts[0].astype(F32),
             jnp.where(row == 4, c_parts[1].astype(F32),
             jnp.where(row == 5, c_parts[2].astype(F32), 0.0))))
    zeros = jnp.zeros((HEAD_DIM, blk), F32)

    def q_variants(aug):
        return (jnp.concatenate([qt[0:HEAD_DIM], aug], axis=0).astype(BF16),
                jnp.concatenate([aug, qt[HEAD_DIM:]], axis=0).astype(BF16))

    def key_block(n):
        n = jnp.clip(n, 0, nblk - 1)
        return jnp.where(n == 0, qi, n - (n <= qi).astype(jnp.int32))

    def stage_scores(j, qts, slot, bias=None):
        start = pl.multiple_of(j * blk, blk)
        for mp, kref in enumerate((k1_ref, k2_ref)):
            s = jnp.dot(kref[pl.ds(start, blk), :], qts[mp], preferred_element_type=F32)
            s_ref[slot, mp] = s if bias is None else s + bias

    def stage_softmax(j, slot, pslot):
        const = -(c_blk * jnp.abs(qi - j).astype(F32))
        for mp in range(2):
            s = s_ref[slot, mp]
            m_old = m_ref[mp]
            m_new = jnp.maximum(m_old, jnp.max(s, axis=0, keepdims=True) + const)
            p_ref[pslot, mp] = jnp.exp2(s - (m_new - const)).astype(BF16)
            alpha_ref[pslot, mp] = jnp.exp2(m_old - m_new)
            m_ref[mp] = m_new

    def stage_values(j, pslot):
        start = pl.multiple_of(j * blk, blk)
        vt = vt_ref[:, pl.ds(start, blk)]
        for mp in range(2):
            acc_ref[mp] = (acc_ref[mp] * alpha_ref[pslot, mp]
                           + jnp.dot(vt, p_ref[pslot, mp], preferred_element_type=F32))

    c_blk = c * float(blk)
    lo = q_variants(aug_lo)
    hi = q_variants(-aug_lo)
    for mp in range(2):
        qv_ref[0, mp] = lo[mp]
        qv_ref[1, mp] = hi[mp]
    m_ref[...] = jnp.full(m_ref.shape, -jnp.inf, F32)
    acc_ref[...] = jnp.zeros(acc_ref.shape, F32)

    d = lax.broadcasted_iota(jnp.int32, (blk, blk), 0) - lax.broadcasted_iota(jnp.int32, (blk, blk), 1)
    stage_scores(qi, q_variants(zeros), 0, -(c * jnp.abs(d).astype(F32)))
    j1 = key_block(1)
    stage_scores(j1, (qv_ref[(j1 > qi).astype(jnp.int32), 0], qv_ref[(j1 > qi).astype(jnp.int32), 1]), 1)

    for n in range(nblk):
        if n > 0:
            stage_values(key_block(n - 1), (n - 1) % 2)
        stage_softmax(key_block(n), n % ATTN_SLOTS, n % 2)
        if n + 2 < nblk:
            ja = key_block(n + 2)
            side = (ja > qi).astype(jnp.int32)
            stage_scores(ja, (qv_ref[side, 0], qv_ref[side, 1]), (n + 2) % ATTN_SLOTS)
    stage_values(key_block(nblk - 1), (nblk - 1) % 2)

    lamv = lamv_ref[...]
    lam_init = lam_init_ref[0]
    lam = (jnp.exp(jnp.sum(lamv[0:1] * lamv[1:2], axis=-1, keepdims=True))
           - jnp.exp(jnp.sum(lamv[2:3] * lamv[3:4], axis=-1, keepdims=True)) + lam_init)
    a1 = acc_ref[0]
    a2 = acc_ref[1]
    ot = a1[0:V_DIM] / a1[V_DIM:V_DIM + 1] - lam * (a2[0:V_DIM] / a2[V_DIM:V_DIM + 1])
    ot = ot * lax.rsqrt(jnp.mean(ot * ot, axis=0, keepdims=True) + EPS)
    o_ref[...] = (ot.T * (subg_ref[...] * (1.0 - lam_init))).astype(BF16)


def _attention(q, k, v, lamv, sub_g, slopes, lam_init, batch, seq):
    smem = pl.BlockSpec(memory_space=pltpu.SMEM)
    blk = ATTN_BLOCK
    nq = seq // blk
    k3 = k.reshape(batch, seq, D_MODEL)
    v3 = v.reshape(batch, seq, D_MODEL)
    return pl.pallas_call(
        _attn_kernel,
        grid=(batch, N_HEADS, nq),
        in_specs=[
            smem, smem,
            pl.BlockSpec((blk, V_DIM), lambda b, h, i: (b * nq + i, h)),
            pl.BlockSpec((None, seq, V_DIM), lambda b, h, i: (b, 0, h)),
            pl.BlockSpec((None, seq, V_DIM), lambda b, h, i: (b, 0, h)),
            pl.BlockSpec((4, HEAD_DIM), lambda b, h, i: (0, 0)),
            pl.BlockSpec((1, V_DIM), lambda b, h, i: (0, 0)),
        ],
        out_specs=pl.BlockSpec((blk, V_DIM), lambda b, h, i: (b * nq + i, h)),
        out_shape=jax.ShapeDtypeStruct((batch * seq, D_MODEL), BF16),
        scratch_shapes=[
            pltpu.VMEM((seq, V_DIM), BF16),
            pltpu.VMEM((seq, V_DIM), BF16),
            pltpu.VMEM((ATTN_VT_ROWS, seq), BF16),
            pltpu.VMEM((2, 2, V_DIM, blk), BF16),
            pltpu.VMEM((ATTN_SLOTS, 2, blk, blk), F32),
            pltpu.VMEM((2, 2, blk, blk), BF16),
            pltpu.VMEM((2, 2, 1, blk), F32),
            pltpu.VMEM((2, 1, blk), F32),
            pltpu.VMEM((2, ATTN_VT_ROWS, blk), F32),
        ],
        compiler_params=pltpu.CompilerParams(
            dimension_semantics=("arbitrary", "arbitrary", "arbitrary"),
            vmem_limit_bytes=V7X_VMEM_LIMIT),
        name="diff_attn",
    )(slopes, lam_init, q, k3, v3, lamv, sub_g)


def _merge_kernel(x_ref, oa_ref, u_ref, sv_ref, ga_ref, gb_ref, sgw_ref, sgb_ref,
                  wa_ref, wb_ref, wo_ref, o_ref, ob_ref):
    for c in range(MERGE_ROWS // CHUNK):
        rows = slice(c * CHUNK, (c + 1) * CHUNK)
        for g in range(N_GROUPS):
            cols = slice(g * GROUP_DIM, (g + 1) * GROUP_DIM)
            mixed = jnp.dot(sgw_ref[g], sv_ref[rows, cols], preferred_element_type=F32)
            ob_ref[rows, cols] = (u_ref[rows, cols].astype(F32) * (mixed + sgb_ref[:, cols])).astype(BF16)
    ya = jnp.dot(oa_ref[...], wa_ref[...], preferred_element_type=F32)
    yb = jnp.dot(ob_ref[...], wb_ref[...], preferred_element_type=F32)
    merged = (ga_ref[...].astype(F32) * ya + gb_ref[...].astype(F32) * yb).astype(BF16)
    o_ref[...] = x_ref[...] + jnp.dot(merged, wo_ref[...], preferred_element_type=F32)


def _merge(x2, oa, u, sv, ga, gb, sgw, sgb_full, wa, wb, wo):
    t = x2.shape[0]
    row = lambda i: (i, 0)
    fixed2 = lambda i: (0, 0)
    tile = pl.BlockSpec((MERGE_ROWS, D_MODEL), row)
    wspec = pl.BlockSpec((D_MODEL, D_MODEL), fixed2, pipeline_mode=pl.Buffered(1))
    return pl.pallas_call(
        _merge_kernel,
        grid=(t // MERGE_ROWS,),
        in_specs=[
            tile, tile, tile, tile, tile, tile,
            pl.BlockSpec((N_GROUPS, CHUNK, CHUNK), lambda i: (0, 0, 0)),
            pl.BlockSpec((CHUNK, D_MODEL), fixed2),
            wspec, wspec, wspec,
        ],
        out_specs=tile,
        out_shape=jax.ShapeDtypeStruct((t, D_MODEL), F32),
        scratch_shapes=[pltpu.VMEM((MERGE_ROWS, D_MODEL), BF16)],
        compiler_params=pltpu.CompilerParams(
            dimension_semantics=("arbitrary",), vmem_limit_bytes=V7X_VMEM_LIMIT),
        name="sgu_merge",
    )(x2, oa, u, sv, ga, gb, sgw, sgb_full, wa, wb, wo)


def _ffn_kernel(x_ref, prev_ref, next_ref, g_ref, wup_ref, cw_ref, cb_ref, wdn_ref,
                o_ref, h_ref, act_ref, *, blocks_per_seq):
    i = pl.program_id(0)
    tm, halo = FFN_ROWS, FFN_HALO
    g = g_ref[...]
    x = x_ref[...]
    keep_prev = (i % blocks_per_seq != 0).astype(F32)
    keep_next = ((i + 1) % blocks_per_seq != 0).astype(F32)
    h_ref[0:halo, :] = (_rms_rows(prev_ref[...], g) * keep_prev).astype(BF16)
    h_ref[halo:halo + tm, :] = _rms_rows(x, g).astype(BF16)
    h_ref[halo + tm:, :] = (_rms_rows(next_ref[...], g) * keep_next).astype(BF16)
    h = h_ref[...]
    ext = tm + 2 * halo

    def conv(cols):
        up = jnp.dot(h, wup_ref[:, cols], preferred_element_type=F32)
        y = (pltpu.roll(up, 1, 0) * cw_ref[0:1, cols] + up * cw_ref[1:2, cols]
             + pltpu.roll(up, ext - 1, 0) * cw_ref[2:3, cols])
        return y[halo:halo + tm] + cb_ref[:, cols]

    for c in range(D_FF // FFN_COLS):
        gate = conv(slice(c * FFN_COLS, (c + 1) * FFN_COLS))
        val = conv(slice(D_FF + c * FFN_COLS, D_FF + (c + 1) * FFN_COLS))
        act_ref[:, c * FFN_COLS:(c + 1) * FFN_COLS] = (gate * _sigmoid(gate) * val).astype(BF16)
    o_ref[...] = x + jnp.dot(act_ref[...], wdn_ref[...], preferred_element_type=F32)


def _ffn(x2, ln_g, wup, conv_w, conv_b, wdn, seq):
    t = x2.shape[0]
    tm, halo = FFN_ROWS, FFN_HALO
    r = tm // halo
    last = t // halo - 1
    fixed = lambda i: (0, 0)
    return pl.pallas_call(
        functools.partial(_ffn_kernel, blocks_per_seq=seq // tm),
        grid=(t // tm,),
        in_specs=[
            pl.BlockSpec((tm, D_MODEL), lambda i: (i, 0)),
            pl.BlockSpec((halo, D_MODEL), lambda i: (jnp.maximum(i * r - 1, 0), 0)),
            pl.BlockSpec((halo, D_MODEL), lambda i: (jnp.minimum((i + 1) * r, last), 0)),
            pl.BlockSpec((1, D_MODEL), fixed),
            pl.BlockSpec((D_MODEL, 2 * D_FF), fixed, pipeline_mode=pl.Buffered(1)),
            pl.BlockSpec((3, 2 * D_FF), fixed),
            pl.BlockSpec((1, 2 * D_FF), fixed),
            pl.BlockSpec((D_FF, D_MODEL), fixed, pipeline_mode=pl.Buffered(1)),
        ],
        out_specs=pl.BlockSpec((tm, D_MODEL), lambda i: (i, 0)),
        out_shape=jax.ShapeDtypeStruct((t, D_MODEL), F32),
        scratch_shapes=[pltpu.VMEM((tm + 2 * halo, D_MODEL), BF16),
                        pltpu.VMEM((tm, D_FF), BF16)],
        compiler_params=pltpu.CompilerParams(
            dimension_semantics=("arbitrary",), vmem_limit_bytes=V7X_VMEM_LIMIT),
        name="conv_ffn",
    )(x2, x2, x2, ln_g, wup, conv_w, conv_b, wdn)


def kernel(x, ln_mix_g, w_in, q_norm_g, k_norm_g, lambda_q1, lambda_k1, lambda_q2, lambda_k2,
           subln_g, sg_norm_g, sg_w, sg_b, w_branch_a, w_branch_b, w_out, ln_ffn_g, w_up,
           conv_w, conv_b, w_down):
    batch, seq, d = x.shape
    depth = w_in.shape[0]
    slopes = 2.0 ** (-8.0 * jnp.arange(1, N_HEADS + 1, dtype=F32) / N_HEADS)
    x2 = x.reshape(batch * seq, d)
    for l in range(depth):
        lam_init = jnp.full((1,), 0.8 - 0.6 * math.exp(-0.3 * l), F32)
        q, k, v, u, sv, ga, gb = _projection(
            x2, ln_mix_g[l][None], w_in[l].astype(BF16),
            jnp.tile(q_norm_g[l], 2 * N_HEADS)[None], jnp.tile(k_norm_g[l], 2 * N_HEADS)[None],
            sg_norm_g[l][None])
        lamv = jnp.stack([lambda_q1[l], lambda_k1[l], lambda_q2[l], lambda_k2[l]])
        oa = _attention(q, k, v, lamv, subln_g[l][None], slopes, lam_init, batch, seq)
        sgb_full = jnp.repeat(sg_b[l].T, GROUP_DIM, axis=1)
        x2 = _merge(x2, oa, u, sv, ga, gb, sg_w[l].astype(BF16), sgb_full,
                    w_branch_a[l].astype(BF16), w_branch_b[l].astype(BF16), w_out[l].astype(BF16))
        x2 = _ffn(x2, ln_ffn_g[l][None], w_up[l].astype(BF16), conv_w[l], conv_b[l][None],
                  w_down[l].astype(BF16), seq)
    return x2.reshape(batch, seq, d)
```

```python
import functools
import math

import jax
import jax.numpy as jnp
from jax import lax
from jax.experimental import pallas as pl
from jax.experimental.pallas import tpu as pltpu

F32 = jnp.float32
BF16 = jnp.bfloat16

D_MODEL = 1024
N_HEADS = 8
HEAD_DIM = 64
V_DIM = 2 * HEAD_DIM
CHUNK = 128
N_GROUPS = 8
GROUP_DIM = D_MODEL // N_GROUPS
D_FF = 2816
EPS = 1e-6
N_SECTIONS = 7

V7X_MXU_DIM = 256
V7X_VMEM_LIMIT = 56 * 1024 * 1024

PROJ_ROWS = 512
ATTN_BLOCK = 256
ATTN_VT_ROWS = V_DIM + 16
ATTN_SLOTS = 4
ATTN_SHIFT_VISITS = 9
LOG2E = math.log2(math.e)
ATTN_MAX_SHIFT = 30.0
ATTN_ZERO_LOG2 = 160.0
ATTN_NULL_SCORE = 30000.0
MERGE_ROWS = 512
FFN_ROWS = 512
FFN_HALO = 16
FFN_COLS = 256


def _rms_rows(x, g):
    return x * lax.rsqrt(jnp.mean(x * x, axis=-1, keepdims=True) + EPS) * g


def _gelu_tanh(x):
    c = math.sqrt(2.0 / math.pi)
    return 0.5 * x * (1.0 + jnp.tanh(c * (x + 0.044715 * (x * x * x))))


def _sigmoid(x):
    return 1.0 / (1.0 + jnp.exp(-x))


def _group_mean_matrix(group):
    r = lax.broadcasted_iota(jnp.int32, (V7X_MXU_DIM, V7X_MXU_DIM), 0) // group
    c = lax.broadcasted_iota(jnp.int32, (V7X_MXU_DIM, V7X_MXU_DIM), 1) // group
    return jnp.where(r == c, 1.0 / group, 0.0).astype(BF16)


def _group_rms_scale(a, mean_mat):
    sq = (a * a).astype(BF16)
    ms = jnp.dot(sq, mean_mat, preferred_element_type=F32)
    return lax.rsqrt(ms + EPS)


def _proj_kernel(x_ref, g_ref, w_ref, qg_ref, kg_ref, sgg_ref,
                 q_ref, k_ref, v_ref, u_ref, sv_ref, ga_ref, gb_ref):
    h = _rms_rows(x_ref[...], g_ref[...]).astype(BF16)
    mean64 = _group_mean_matrix(HEAD_DIM)
    mean128 = _group_mean_matrix(GROUP_DIM)
    w = V7X_MXU_DIM

    def section(s, c):
        lo = s * D_MODEL + c * w
        return jnp.dot(h, w_ref[:, lo:lo + w], preferred_element_type=F32)

    for c in range(D_MODEL // w):
        cols = slice(c * w, (c + 1) * w)
        a = section(0, c)
        q_ref[:, cols] = (a * _group_rms_scale(a, mean64) * (qg_ref[:, cols] * (HEAD_DIM ** -0.5 * LOG2E))).astype(BF16)
        a = section(1, c)
        k_ref[:, cols] = (a * _group_rms_scale(a, mean64) * kg_ref[:, cols]).astype(BF16)
        v_ref[:, cols] = section(2, c).astype(BF16)
        u_ref[:, cols] = _gelu_tanh(section(3, c)).astype(BF16)
        a = _gelu_tanh(section(4, c))
        sv_ref[:, cols] = (a * _group_rms_scale(a, mean128) * sgg_ref[:, cols]).astype(BF16)
        ga_ref[:, cols] = _sigmoid(section(5, c)).astype(BF16)
        gb_ref[:, cols] = _sigmoid(section(6, c)).astype(BF16)


def _projection(x2, ln_g, w_in, q_g, k_g, sg_g):
    t = x2.shape[0]
    row = lambda i: (i, 0)
    fixed = lambda i: (0, 0)
    vec = pl.BlockSpec((1, D_MODEL), fixed)
    out = jax.ShapeDtypeStruct((t, D_MODEL), BF16)
    return pl.pallas_call(
        _proj_kernel,
        grid=(t // PROJ_ROWS,),
        in_specs=[
            pl.BlockSpec((PROJ_ROWS, D_MODEL), row),
            vec,
            pl.BlockSpec((D_MODEL, N_SECTIONS * D_MODEL), fixed, pipeline_mode=pl.Buffered(1)),
            vec, vec, vec,
        ],
        out_specs=[pl.BlockSpec((PROJ_ROWS, D_MODEL), row)] * N_SECTIONS,
        out_shape=[out] * N_SECTIONS,
        compiler_params=pltpu.CompilerParams(
            dimension_semantics=("arbitrary",), vmem_limit_bytes=V7X_VMEM_LIMIT),
        name="proj",
    )(x2, ln_g, w_in, q_g, k_g, sg_g)


def _split3(c):
    hi = c.astype(BF16)
    mid = (c - hi.astype(F32)).astype(BF16)
    lo = (c - hi.astype(F32) - mid.astype(F32)).astype(BF16)
    return hi.astype(F32), mid.astype(F32), lo.astype(F32)


def _pick_rows(row, entries, default=0.0):
    out = default
    for lo, hi, val in reversed(entries):
        out = jnp.where((row >= lo) & (row < hi), val, out)
    return out


def _attn_kernel(slopes_ref, lam_init_ref, q_ref, k_ref, v_ref, lamv_ref, subg_ref, o_ref,
                 k1_ref, k2_ref, vt_ref, kn_ref, qv_ref, s_ref, p_ref, alpha_ref, m_ref, acc_ref,
                 *, shifted_heads):
    head = pl.program_id(1)
    qi = pl.program_id(2)
    blk = ATTN_BLOCK
    seq = k_ref.shape[0]
    nblk = seq // blk
    c = jnp.full((1, 1), slopes_ref[head] * LOG2E, F32)
    c3 = _split3(c)
    c_blk = c * float(blk)

    @pl.when(qi == 0)
    def _():
        lane = lax.broadcasted_iota(jnp.int32, (seq, V_DIM), 1)
        pos = lax.broadcasted_iota(jnp.int32, (seq, V_DIM), 0)
        off = (pos % blk).astype(F32)
        cb3 = _split3(c_blk * (pos // blk).astype(F32))
        k32 = k_ref[...].astype(F32)
        for mp, ref in enumerate((k1_ref, k2_ref)):
            a = lane - (HEAD_DIM if mp == 0 else 0)
            aug = _pick_rows(a, [(0, 1, -c3[0]), (1, 2, -c3[1]), (2, 3, -c3[2]), (3, 6, off),
                                 (6, 7, cb3[0]), (7, 8, cb3[1]), (8, 9, cb3[2]), (9, 12, 1.0)])
            own = (lane < HEAD_DIM) if mp == 0 else (lane >= HEAD_DIM)
            ref[...] = jnp.where(own, k32, aug).astype(BF16)
            norm2 = jnp.sum(jnp.where(own, k32 * k32, 0.0), axis=1, keepdims=True)
            kn_ref[mp] = jnp.broadcast_to(jnp.max(norm2, axis=0, keepdims=True), kn_ref.shape[1:])
        step = 512
        for s0 in range(0, seq, step):
            vt_ref[0:V_DIM, s0:s0 + step] = v_ref[s0:s0 + step, :].astype(F32).T.astype(BF16)
        r = lax.broadcasted_iota(jnp.int32, (ATTN_VT_ROWS - V_DIM, seq), 0)
        vt_ref[V_DIM:, :] = jnp.where(r == 0, 1.0, 0.0).astype(BF16)

    qt = q_ref[...].astype(F32).T
    q_rows = (qt[0:HEAD_DIM], qt[HEAD_DIM:])
    row = lax.broadcasted_iota(jnp.int32, (HEAD_DIM, blk), 0)
    il = lax.broadcasted_iota(jnp.int32, (HEAD_DIM, blk), 1).astype(F32)

    def with_lanes(mp, aug):
        parts = [q_rows[0], aug] if mp == 0 else [aug, q_rows[1]]
        return jnp.concatenate(parts, axis=0).astype(BF16)

    def offset_lanes(sigma):
        return [(0, 3, sigma * il), (3, 4, sigma * c3[0]), (4, 5, sigma * c3[1]), (5, 6, sigma * c3[2])]

    d = lax.broadcasted_iota(jnp.int32, (blk, blk), 0) - lax.broadcasted_iota(jnp.int32, (blk, blk), 1)
    diag_bias = -(c * jnp.abs(d).astype(F32))

    shift = []
    for mp in range(2):
        qn2 = jnp.max(jnp.sum(q_rows[mp] * q_rows[mp], axis=0, keepdims=True), axis=1, keepdims=True)
        shift.append(1.01 * jnp.sqrt(qn2 * kn_ref[mp][0:1, 0:1]) + 0.01)
    shifted_ok = jnp.maximum(shift[0], shift[1])[0, 0] <= ATTN_MAX_SHIFT

    def values_stage(j, pslot, alpha=None):
        start = pl.multiple_of(j * blk, blk)
        vt = vt_ref[:, pl.ds(start, blk)]
        for mp in range(2):
            pv = jnp.dot(vt, p_ref[pslot, mp], preferred_element_type=F32)
            acc_ref[mp] = (acc_ref[mp] if alpha is None else acc_ref[mp] * alpha_ref[alpha, mp]) + pv

    acc_ref[...] = jnp.zeros(acc_ref.shape, F32)

    def shifted_path(dk):
        cbi = c_blk * qi.astype(F32)
        for mp in range(2):
            lo3 = _split3(-cbi - shift[mp])
            hi3 = _split3(cbi - shift[mp])
            qv_ref[0, mp] = with_lanes(mp, _pick_rows(row, offset_lanes(1.0) + [
                (6, 9, 1.0), (9, 10, lo3[0]), (10, 11, lo3[1]), (11, 12, lo3[2])]))
            qv_ref[1, mp] = with_lanes(mp, _pick_rows(row, offset_lanes(-1.0) + [
                (6, 9, -1.0), (9, 10, hi3[0]), (10, 11, hi3[1]), (11, 12, hi3[2])]))
            qv_ref[2, mp] = with_lanes(mp, _pick_rows(row, [(9, 10, -ATTN_NULL_SCORE)]))

        diag_q = []
        for mp in range(2):
            s3 = _split3(-shift[mp])
            diag_q.append(with_lanes(mp, _pick_rows(row, [(9, 10, s3[0]), (10, 11, s3[1]), (11, 12, s3[2])])))

        def visit_block(u):
            j = qi - dk + (u - 1) if u <= dk else qi + (u - dk)
            return jnp.clip(j, 0, nblk - 1), (j >= 0) & (j < nblk)

        def scores_stage(u):
            if u == 0:
                j, qts, bias = qi, diag_q, diag_bias
            else:
                j, valid = visit_block(u)
                var = jnp.where(valid, 0 if u <= dk else 1, 2)
                qts, bias = (qv_ref[var, 0], qv_ref[var, 1]), None
            start = pl.multiple_of(j * blk, blk)
            for mp, kref in enumerate((k1_ref, k2_ref)):
                s = jnp.dot(kref[pl.ds(start, blk), :], qts[mp], preferred_element_type=F32)
                p_ref[u, mp] = jnp.exp2(s if bias is None else s + bias).astype(BF16)

        def weighted_values(u, acc):
            j = qi if u == 0 else visit_block(u)[0]
            vt = vt_ref[:, pl.ds(pl.multiple_of(j * blk, blk), blk)]
            pv = [jnp.dot(vt, p_ref[u, mp], preferred_element_type=F32) for mp in range(2)]
            return pv if acc is None else [acc[mp] + pv[mp] for mp in range(2)]

        visits = 2 * dk + 1
        acc = None
        scores_stage(0)
        for u in range(visits):
            if u + 1 < visits:
                scores_stage(u + 1)
            acc = weighted_values(u, acc)
        for mp in range(2):
            acc_ref[mp] = acc[mp]

    use_shift = shifted_ok & functools.reduce(jnp.logical_or, [head == h for h in shifted_heads])
    for h, dk in shifted_heads.items():
        pl.when(shifted_ok & (head == h))(functools.partial(shifted_path, dk))

    @pl.when(jnp.logical_not(use_shift))
    def _():
        def key_block(n):
            return jnp.where(n == 0, qi, n - (n <= qi).astype(jnp.int32))

        def scores_stage(j, qts, slot, bias=None):
            start = pl.multiple_of(j * blk, blk)
            for mp, kref in enumerate((k1_ref, k2_ref)):
                s = jnp.dot(kref[pl.ds(start, blk), :], qts[mp], preferred_element_type=F32)
                s_ref[slot, mp] = s if bias is None else s + bias

        def softmax_stage(j, slot, pslot):
            const = -(c_blk * jnp.abs(qi - j).astype(F32))
            for mp in range(2):
                s = s_ref[slot, mp]
                m_old = m_ref[mp]
                m_new = jnp.maximum(m_old, jnp.max(s, axis=0, keepdims=True) + const)
                p_ref[pslot, mp] = jnp.exp2(s - (m_new - const)).astype(BF16)
                alpha_ref[pslot, mp] = jnp.exp2(m_old - m_new)
                m_ref[mp] = m_new

        for mp in range(2):
            qv_ref[0, mp] = with_lanes(mp, _pick_rows(row, offset_lanes(1.0)))
            qv_ref[1, mp] = with_lanes(mp, _pick_rows(row, offset_lanes(-1.0)))
        m_ref[...] = jnp.full(m_ref.shape, -jnp.inf, F32)
        zeros = jnp.zeros((HEAD_DIM, blk), F32)
        scores_stage(qi, (with_lanes(0, zeros), with_lanes(1, zeros)), 0, diag_bias)
        j1 = key_block(1)
        side = (j1 > qi).astype(jnp.int32)
        scores_stage(j1, (qv_ref[side, 0], qv_ref[side, 1]), 1)
        for n in range(nblk):
            if n > 0:
                values_stage(key_block(n - 1), (n - 1) % 2, alpha=(n - 1) % 2)
            softmax_stage(key_block(n), n % ATTN_SLOTS, n % 2)
            if n + 2 < nblk:
                ja = key_block(n + 2)
                side = (ja > qi).astype(jnp.int32)
                scores_stage(ja, (qv_ref[side, 0], qv_ref[side, 1]), (n + 2) % ATTN_SLOTS)
        values_stage(key_block(nblk - 1), (nblk - 1) % 2, alpha=(nblk - 1) % 2)

    lamv = lamv_ref[...]
    lam_init = lam_init_ref[0]
    lam = (jnp.exp(jnp.sum(lamv[0:1] * lamv[1:2], axis=-1, keepdims=True))
           - jnp.exp(jnp.sum(lamv[2:3] * lamv[3:4], axis=-1, keepdims=True)) + lam_init)
    a1 = acc_ref[0]
    a2 = acc_ref[1]
    ot = a1[0:V_DIM] / a1[V_DIM:V_DIM + 1] - lam * (a2[0:V_DIM] / a2[V_DIM:V_DIM + 1])
    ot = ot * lax.rsqrt(jnp.mean(ot * ot, axis=0, keepdims=True) + EPS)
    o_ref[...] = (ot.T * (subg_ref[...] * (1.0 - lam_init))).astype(BF16)


def _attention(q, k, v, lamv, sub_g, slopes, lam_init, batch, seq):
    smem = pl.BlockSpec(memory_space=pltpu.SMEM)
    blk = ATTN_BLOCK
    nq = seq // blk
    k3 = k.reshape(batch, seq, D_MODEL)
    v3 = v.reshape(batch, seq, D_MODEL)
    shifted_heads = {}
    for h in range(N_HEADS):
        slope = 2.0 ** (-8.0 * (h + 1) / N_HEADS)
        dk = math.ceil((ATTN_ZERO_LOG2 / (slope * LOG2E) - 1.0) / blk)
        if 2 * dk + 1 <= ATTN_SHIFT_VISITS:
            shifted_heads[h] = dk
    return pl.pallas_call(
        functools.partial(_attn_kernel, shifted_heads=shifted_heads),
        grid=(batch, N_HEADS, nq),
        in_specs=[
            smem, smem,
            pl.BlockSpec((blk, V_DIM), lambda b, h, i: (b * nq + i, h)),
            pl.BlockSpec((None, seq, V_DIM), lambda b, h, i: (b, 0, h)),
            pl.BlockSpec((None, seq, V_DIM), lambda b, h, i: (b, 0, h)),
            pl.BlockSpec((4, HEAD_DIM), lambda b, h, i: (0, 0)),
            pl.BlockSpec((1, V_DIM), lambda b, h, i: (0, 0)),
        ],
        out_specs=pl.BlockSpec((blk, V_DIM), lambda b, h, i: (b * nq + i, h)),
        out_shape=jax.ShapeDtypeStruct((batch * seq, D_MODEL), BF16),
        scratch_shapes=[
            pltpu.VMEM((seq, V_DIM), BF16),
            pltpu.VMEM((seq, V_DIM), BF16),
            pltpu.VMEM((ATTN_VT_ROWS, seq), BF16),
            pltpu.VMEM((2, 8, 128), F32),
            pltpu.VMEM((3, 2, V_DIM, blk), BF16),
            pltpu.VMEM((ATTN_SLOTS, 2, blk, blk), F32),
            pltpu.VMEM((ATTN_SHIFT_VISITS, 2, blk, blk), BF16),
            pltpu.VMEM((2, 2, 1, blk), F32),
            pltpu.VMEM((2, 1, blk), F32),
            pltpu.VMEM((2, ATTN_VT_ROWS, blk), F32),
        ],
        compiler_params=pltpu.CompilerParams(
            dimension_semantics=("arbitrary", "arbitrary", "arbitrary"),
            vmem_limit_bytes=V7X_VMEM_LIMIT),
        name="diff_attn",
    )(slopes, lam_init, q, k3, v3, lamv, sub_g)


def _merge_kernel(x_ref, oa_ref, u_ref, sv_ref, ga_ref, gb_ref, sgw_ref, sgb_ref,
                  wa_ref, wb_ref, wo_ref, o_ref, ob_ref):
    for c in range(MERGE_ROWS // CHUNK):
        rows = slice(c * CHUNK, (c + 1) * CHUNK)
        for g in range(N_GROUPS):
            cols = slice(g * GROUP_DIM, (g + 1) * GROUP_DIM)
            mixed = jnp.dot(sgw_ref[g], sv_ref[rows, cols], preferred_element_type=F32)
            ob_ref[rows, cols] = (u_ref[rows, cols].astype(F32) * (mixed + sgb_ref[:, cols])).astype(BF16)
    ya = jnp.dot(oa_ref[...], wa_ref[...], preferred_element_type=F32)
    yb = jnp.dot(ob_ref[...], wb_ref[...], preferred_element_type=F32)
    merged = (ga_ref[...].astype(F32) * ya + gb_ref[...].astype(F32) * yb).astype(BF16)
    o_ref[...] = x_ref[...] + jnp.dot(merged, wo_ref[...], preferred_element_type=F32)


def _merge(x2, oa, u, sv, ga, gb, sgw, sgb_full, wa, wb, wo):
    t = x2.shape[0]
    row = lambda i: (i, 0)
    fixed2 = lambda i: (0, 0)
    tile = pl.BlockSpec((MERGE_ROWS, D_MODEL), row)
    wspec = pl.BlockSpec((D_MODEL, D_MODEL), fixed2, pipeline_mode=pl.Buffered(1))
    return pl.pallas_call(
        _merge_kernel,
        grid=(t // MERGE_ROWS,),
        in_specs=[
            tile, tile, tile, tile, tile, tile,
            pl.BlockSpec((N_GROUPS, CHUNK, CHUNK), lambda i: (0, 0, 0)),
            pl.BlockSpec((CHUNK, D_MODEL), fixed2),
            wspec, wspec, wspec,
        ],
        out_specs=tile,
        out_shape=jax.ShapeDtypeStruct((t, D_MODEL), F32),
        scratch_shapes=[pltpu.VMEM((MERGE_ROWS, D_MODEL), BF16)],
        compiler_params=pltpu.CompilerParams(
            dimension_semantics=("arbitrary",), vmem_limit_bytes=V7X_VMEM_LIMIT),
        name="sgu_merge",
    )(x2, oa, u, sv, ga, gb, sgw, sgb_full, wa, wb, wo)


def _ffn_kernel(x_ref, prev_ref, next_ref, g_ref, wup_ref, cw_ref, cb_ref, wdn_ref,
                o_ref, h_ref, act_ref, *, blocks_per_seq):
    i = pl.program_id(0)
    tm, halo = FFN_ROWS, FFN_HALO
    g = g_ref[...]
    x = x_ref[...]
    keep_prev = (i % blocks_per_seq != 0).astype(F32)
    keep_next = ((i + 1) % blocks_per_seq != 0).astype(F32)
    h_ref[0:halo, :] = (_rms_rows(prev_ref[...], g) * keep_prev).astype(BF16)
    h_ref[halo:halo + tm, :] = _rms_rows(x, g).astype(BF16)
    h_ref[halo + tm:, :] = (_rms_rows(next_ref[...], g) * keep_next).astype(BF16)
    h = h_ref[...]
    ext = tm + 2 * halo

    def conv(cols):
        up = jnp.dot(h, wup_ref[:, cols], preferred_element_type=F32)
        y = (pltpu.roll(up, 1, 0) * cw_ref[0:1, cols] + up * cw_ref[1:2, cols]
             + pltpu.roll(up, ext - 1, 0) * cw_ref[2:3, cols])
        return y[halo:halo + tm] + cb_ref[:, cols]

    for c in range(D_FF // FFN_COLS):
        gate = conv(slice(c * FFN_COLS, (c + 1) * FFN_COLS))
        val = conv(slice(D_FF + c * FFN_COLS, D_FF + (c + 1) * FFN_COLS))
        act_ref[:, c * FFN_COLS:(c + 1) * FFN_COLS] = (gate * _sigmoid(gate) * val).astype(BF16)
    o_ref[...] = x + jnp.dot(act_ref[...], wdn_ref[...], preferred_element_type=F32)


def _ffn(x2, ln_g, wup, conv_w, conv_b, wdn, seq):
    t = x2.shape[0]
    tm, halo = FFN_ROWS, FFN_HALO
    r = tm // halo
    last = t // halo - 1
    fixed = lambda i: (0, 0)
    return pl.pallas_call(
        functools.partial(_ffn_kernel, blocks_per_seq=seq // tm),
        grid=(t // tm,),
        in_specs=[
            pl.BlockSpec((tm, D_MODEL), lambda i: (i, 0)),
            pl.BlockSpec((halo, D_MODEL), lambda i: (jnp.maximum(i * r - 1, 0), 0)),
            pl.BlockSpec((halo, D_MODEL), lambda i: (jnp.minimum((i + 1) * r, last), 0)),
            pl.BlockSpec((1, D_MODEL), fixed),
            pl.BlockSpec((D_MODEL, 2 * D_FF), fixed, pipeline_mode=pl.Buffered(1)),
            pl.BlockSpec((3, 2 * D_FF), fixed),
            pl.BlockSpec((1, 2 * D_FF), fixed),
            pl.BlockSpec((D_FF, D_MODEL), fixed, pipeline_mode=pl.Buffered(1)),
        ],
        out_specs=pl.BlockSpec((tm, D_MODEL), lambda i: (i, 0)),
        out_shape=jax.ShapeDtypeStruct((t, D_MODEL), F32),
        scratch_shapes=[pltpu.VMEM((tm + 2 * halo, D_MODEL), BF16),
                        pltpu.VMEM((tm, D_FF), BF16)],
        compiler_params=pltpu.CompilerParams(
            dimension_semantics=("arbitrary",), vmem_limit_bytes=V7X_VMEM_LIMIT),
        name="conv_ffn",
    )(x2, x2, x2, ln_g, wup, conv_w, conv_b, wdn)


def kernel(x, ln_mix_g, w_in, q_norm_g, k_norm_g, lambda_q1, lambda_k1, lambda_q2, lambda_k2,
           subln_g, sg_norm_g, sg_w, sg_b, w_branch_a, w_branch_b, w_out, ln_ffn_g, w_up,
           conv_w, conv_b, w_down):
    batch, seq, d = x.shape
    depth = w_in.shape[0]
    slopes = 2.0 ** (-8.0 * jnp.arange(1, N_HEADS + 1, dtype=F32) / N_HEADS)
    x2 = x.reshape(batch * seq, d)
    for l in range(depth):
        lam_init = jnp.full((1,), 0.8 - 0.6 * math.exp(-0.3 * l), F32)
        q, k, v, u, sv, ga, gb = _projection(
            x2, ln_mix_g[l][None], w_in[l].astype(BF16),
            jnp.tile(q_norm_g[l], 2 * N_HEADS)[None], jnp.tile(k_norm_g[l], 2 * N_HEADS)[None],
            sg_norm_g[l][None])
        lamv = jnp.stack([lambda_q1[l], lambda_k1[l], lambda_q2[l], lambda_k2[l]])
        oa = _attention(q, k, v, lamv, subln_g[l][None], slopes, lam_init, batch, seq)
        sgb_full = jnp.repeat(sg_b[l].T, GROUP_DIM, axis=1)
        x2 = _merge(x2, oa, u, sv, ga, gb, sg_w[l].astype(BF16), sgb_full,
                    w_branch_a[l].astype(BF16), w_branch_b[l].astype(BF16), w_out[l].astype(BF16))
        x2 = _ffn(x2, ln_ffn_g[l][None], w_up[l].astype(BF16), conv_w[l], conv_b[l][None],
                  w_down[l].astype(BF16), seq)
    return x2.reshape(batch, seq, d)
```

```python
import functools
import math

import jax
import jax.numpy as jnp
from jax import lax
from jax.experimental import pallas as pl
from jax.experimental.pallas import tpu as pltpu

F32 = jnp.float32
BF16 = jnp.bfloat16

D_MODEL = 1024
N_HEADS = 8
HEAD_DIM = 64
V_DIM = 2 * HEAD_DIM
CHUNK = 128
N_GROUPS = 8
GROUP_DIM = D_MODEL // N_GROUPS
D_FF = 2816
EPS = 1e-6
N_SECTIONS = 7

V7X_MXU_DIM = 256
V7X_VMEM_LIMIT = 56 * 1024 * 1024

PROJ_ROWS = 512
ATTN_BLOCK = 256
ATTN_VT_ROWS = V_DIM + 16
ATTN_SLOTS = 4
ATTN_SHIFT_VISITS = 9
LOG2E = math.log2(math.e)
ATTN_MAX_SHIFT = 30.0
ATTN_ZERO_LOG2 = 160.0
ATTN_NULL_SCORE = 30000.0
MERGE_ROWS = 512
FFN_ROWS = 512
FFN_HALO = 16
FFN_COLS = 256


def _rms_rows(x, g):
    return x * lax.rsqrt(jnp.mean(x * x, axis=-1, keepdims=True) + EPS) * g


def _gelu_tanh(x):
    c = math.sqrt(2.0 / math.pi)
    return 0.5 * x * (1.0 + jnp.tanh(c * (x + 0.044715 * (x * x * x))))


def _sigmoid(x):
    return 1.0 / (1.0 + jnp.exp(-x))


def _group_mean_matrix(group):
    r = lax.broadcasted_iota(jnp.int32, (V7X_MXU_DIM, V7X_MXU_DIM), 0) // group
    c = lax.broadcasted_iota(jnp.int32, (V7X_MXU_DIM, V7X_MXU_DIM), 1) // group
    return jnp.where(r == c, 1.0 / group, 0.0).astype(BF16)


def _group_rms_scale(a, mean_mat):
    sq = (a * a).astype(BF16)
    ms = jnp.dot(sq, mean_mat, preferred_element_type=F32)
    return lax.rsqrt(ms + EPS)


def _proj_kernel(x_ref, g_ref, w_ref, qg_ref, kg_ref, sgg_ref,
                 q_ref, k_ref, v_ref, u_ref, sv_ref, ga_ref, gb_ref):
    h = _rms_rows(x_ref[...], g_ref[...]).astype(BF16)
    mean64 = _group_mean_matrix(HEAD_DIM)
    mean128 = _group_mean_matrix(GROUP_DIM)
    w = V7X_MXU_DIM

    def section(s, c):
        lo = s * D_MODEL + c * w
        return jnp.dot(h, w_ref[:, lo:lo + w], preferred_element_type=F32)

    for c in range(D_MODEL // w):
        cols = slice(c * w, (c + 1) * w)
        a = section(0, c)
        q_ref[:, cols] = (a * _group_rms_scale(a, mean64) * (qg_ref[:, cols] * (HEAD_DIM ** -0.5 * LOG2E))).astype(BF16)
        a = section(1, c)
        k_ref[:, cols] = (a * _group_rms_scale(a, mean64) * kg_ref[:, cols]).astype(BF16)
        v_ref[:, cols] = section(2, c).astype(BF16)
        u_ref[:, cols] = _gelu_tanh(section(3, c)).astype(BF16)
        a = _gelu_tanh(section(4, c))
        sv_ref[:, cols] = (a * _group_rms_scale(a, mean128) * sgg_ref[:, cols]).astype(BF16)
        ga_ref[:, cols] = _sigmoid(section(5, c)).astype(BF16)
        gb_ref[:, cols] = _sigmoid(section(6, c)).astype(BF16)


def _projection(x2, ln_g, w_in, q_g, k_g, sg_g):
    t = x2.shape[0]
    row = lambda i: (i, 0)
    fixed = lambda i: (0, 0)
    vec = pl.BlockSpec((1, D_MODEL), fixed)
    out = jax.ShapeDtypeStruct((t, D_MODEL), BF16)
    return pl.pallas_call(
        _proj_kernel,
        grid=(t // PROJ_ROWS,),
        in_specs=[
            pl.BlockSpec((PROJ_ROWS, D_MODEL), row),
            vec,
            pl.BlockSpec((D_MODEL, N_SECTIONS * D_MODEL), fixed, pipeline_mode=pl.Buffered(1)),
            vec, vec, vec,
        ],
        out_specs=[pl.BlockSpec((PROJ_ROWS, D_MODEL), row)] * N_SECTIONS,
        out_shape=[out] * N_SECTIONS,
        compiler_params=pltpu.CompilerParams(
            dimension_semantics=("arbitrary",), vmem_limit_bytes=V7X_VMEM_LIMIT),
        name="proj",
    )(x2, ln_g, w_in, q_g, k_g, sg_g)


def _split3(c):
    hi = c.astype(BF16)
    mid = (c - hi.astype(F32)).astype(BF16)
    lo = (c - hi.astype(F32) - mid.astype(F32)).astype(BF16)
    return hi.astype(F32), mid.astype(F32), lo.astype(F32)


def _pick_rows(row, entries, default=0.0):
    out = default
    for lo, hi, val in reversed(entries):
        out = jnp.where((row >= lo) & (row < hi), val, out)
    return out


def _attn_kernel(slopes_ref, lam_init_ref, q_ref, k_ref, v_ref, lamv_ref, subg_ref, qg_ref, kg_ref, o_ref,
                 k1_ref, k2_ref, vt_ref, dbias_ref, shift_ref, flag_ref, qv_ref, s_ref, p_ref,
                 alpha_ref, m_ref, acc_ref, *, shifted_heads):
    head = pl.program_id(1)
    qi = pl.program_id(2)
    blk = ATTN_BLOCK
    seq = k_ref.shape[0]
    nblk = seq // blk
    c = jnp.full((1, 1), slopes_ref[head] * LOG2E, F32)
    c3 = _split3(c)
    c_blk = c * float(blk)

    @pl.when(qi == 0)
    def _():
        lane_b = lax.broadcasted_iota(jnp.int32, (blk, V_DIM), 1)
        off = lax.broadcasted_iota(jnp.int32, (blk, V_DIM), 0).astype(F32)
        for mp, ref in enumerate((k1_ref, k2_ref)):
            a = lane_b - (HEAD_DIM if mp == 0 else 0)
            own_b = (lane_b < HEAD_DIM) if mp == 0 else (lane_b >= HEAD_DIM)
            fixed = _pick_rows(a, [(0, 1, -c3[0]), (1, 2, -c3[1]), (2, 3, -c3[2]), (3, 6, off), (9, 12, 1.0)])
            for jb in range(nblk):
                cb3 = _split3(c_blk * float(jb))
                aug = _pick_rows(a, [(6, 7, cb3[0]), (7, 8, cb3[1]), (8, 9, cb3[2])], fixed)
                rows = slice(jb * blk, (jb + 1) * blk)
                ref[rows, :] = jnp.where(own_b, k_ref[rows, :].astype(F32), aug).astype(BF16)
        gmax = (jnp.max(jnp.abs(qg_ref[...]), axis=1, keepdims=True)
                * jnp.max(jnp.abs(kg_ref[...]), axis=1, keepdims=True))
        bound = 1.01 * (HEAD_DIM * HEAD_DIM ** -0.5 * LOG2E) * gmax + 0.01
        shift_ref[...] = jnp.broadcast_to(bound, shift_ref.shape)
        flag_ref[0] = (bound[0, 0] <= ATTN_MAX_SHIFT).astype(jnp.int32)
        step = 512
        for s0 in range(0, seq, step):
            vt_ref[0:V_DIM, s0:s0 + step] = v_ref[s0:s0 + step, :].astype(F32).T.astype(BF16)
        r = lax.broadcasted_iota(jnp.int32, (ATTN_VT_ROWS - V_DIM, seq), 0)
        vt_ref[V_DIM:, :] = jnp.where(r == 0, 1.0, 0.0).astype(BF16)
        d = lax.broadcasted_iota(jnp.int32, (blk, blk), 0) - lax.broadcasted_iota(jnp.int32, (blk, blk), 1)
        dbias_ref[...] = -(c * jnp.abs(d).astype(F32))

    shifted_ok = flag_ref[0] == 1
    row = lax.broadcasted_iota(jnp.int32, (HEAD_DIM, blk), 0)
    il = lax.broadcasted_iota(jnp.int32, (HEAD_DIM, blk), 1).astype(F32)

    def query_rows():
        qt = q_ref[...].astype(F32).T
        return qt[0:HEAD_DIM], qt[HEAD_DIM:]

    def with_lanes(q_rows, mp, aug):
        parts = [q_rows[0], aug] if mp == 0 else [aug, q_rows[1]]
        return jnp.concatenate(parts, axis=0).astype(BF16)

    def offset_lanes(sigma):
        return [(0, 3, sigma * il), (3, 4, sigma * c3[0]), (4, 5, sigma * c3[1]), (5, 6, sigma * c3[2])]

    def values_stage(j, pslot, alpha=None):
        start = pl.multiple_of(j * blk, blk)
        vt = vt_ref[:, pl.ds(start, blk)]
        for mp in range(2):
            pv = jnp.dot(vt, p_ref[pslot, mp], preferred_element_type=F32)
            acc_ref[mp] = (acc_ref[mp] if alpha is None else acc_ref[mp] * alpha_ref[alpha, mp]) + pv

    def shifted_path(dk):
        cbi = c_blk * qi.astype(F32)
        q_rows = query_rows()
        diag_q = []
        shift = shift_ref[0:1, 0:1]
        lo3 = _split3(-cbi - shift)
        hi3 = _split3(cbi - shift)
        s3 = _split3(-shift)
        for mp in range(2):
            qv_ref[0, mp] = with_lanes(q_rows, mp, _pick_rows(row, offset_lanes(1.0) + [
                (6, 9, 1.0), (9, 10, lo3[0]), (10, 11, lo3[1]), (11, 12, lo3[2])]))
            qv_ref[1, mp] = with_lanes(q_rows, mp, _pick_rows(row, offset_lanes(-1.0) + [
                (6, 9, -1.0), (9, 10, hi3[0]), (10, 11, hi3[1]), (11, 12, hi3[2])]))
            qv_ref[2, mp] = with_lanes(q_rows, mp, _pick_rows(row, [(9, 10, -ATTN_NULL_SCORE)]))
            diag_q.append(with_lanes(q_rows, mp, _pick_rows(
                row, [(9, 10, s3[0]), (10, 11, s3[1]), (11, 12, s3[2])])))

        def visit_block(u):
            j = qi - dk + (u - 1) if u <= dk else qi + (u - dk)
            return jnp.clip(j, 0, nblk - 1), (j >= 0) & (j < nblk)

        def scores_stage(u):
            if u == 0:
                j, qts, bias = qi, diag_q, dbias_ref[...]
            else:
                j, valid = visit_block(u)
                var = jnp.where(valid, 0 if u <= dk else 1, 2)
                qts, bias = (qv_ref[var, 0], qv_ref[var, 1]), None
            start = pl.multiple_of(j * blk, blk)
            for mp, kref in enumerate((k1_ref, k2_ref)):
                s = jnp.dot(kref[pl.ds(start, blk), :], qts[mp], preferred_element_type=F32)
                p_ref[u, mp] = jnp.exp2(s if bias is None else s + bias).astype(BF16)

        def weighted_values(u, acc):
            j = qi if u == 0 else visit_block(u)[0]
            vt = vt_ref[:, pl.ds(pl.multiple_of(j * blk, blk), blk)]
            pv = [jnp.dot(vt, p_ref[u, mp], preferred_element_type=F32) for mp in range(2)]
            return pv if acc is None else [acc[mp] + pv[mp] for mp in range(2)]

        visits = 2 * dk + 1
        acc = None
        scores_stage(0)
        for u in range(visits):
            if u + 1 < visits:
                scores_stage(u + 1)
            acc = weighted_values(u, acc)
        for mp in range(2):
            acc_ref[mp] = acc[mp]

    use_shift = shifted_ok & functools.reduce(jnp.logical_or, [head == h for h in shifted_heads])
    for h, dk in shifted_heads.items():
        pl.when(shifted_ok & (head == h))(functools.partial(shifted_path, dk))

    @pl.when(jnp.logical_not(use_shift))
    def _():
        def key_block(n):
            return jnp.where(n == 0, qi, n - (n <= qi).astype(jnp.int32))

        def scores_stage(j, qts, slot, bias=None):
            start = pl.multiple_of(j * blk, blk)
            for mp, kref in enumerate((k1_ref, k2_ref)):
                s = jnp.dot(kref[pl.ds(start, blk), :], qts[mp], preferred_element_type=F32)
                s_ref[slot, mp] = s if bias is None else s + bias

        def softmax_stage(j, slot, pslot):
            const = -(c_blk * jnp.abs(qi - j).astype(F32))
            for mp in range(2):
                s = s_ref[slot, mp]
                m_old = m_ref[mp]
                m_new = jnp.maximum(m_old, jnp.max(s, axis=0, keepdims=True) + const)
                p_ref[pslot, mp] = jnp.exp2(s - (m_new - const)).astype(BF16)
                alpha_ref[pslot, mp] = jnp.exp2(m_old - m_new)
                m_ref[mp] = m_new

        q_rows = query_rows()
        for mp in range(2):
            qv_ref[0, mp] = with_lanes(q_rows, mp, _pick_rows(row, offset_lanes(1.0)))
            qv_ref[1, mp] = with_lanes(q_rows, mp, _pick_rows(row, offset_lanes(-1.0)))
        m_ref[...] = jnp.full(m_ref.shape, -jnp.inf, F32)
        acc_ref[...] = jnp.zeros(acc_ref.shape, F32)
        zeros = jnp.zeros((HEAD_DIM, blk), F32)
        scores_stage(qi, (with_lanes(q_rows, 0, zeros), with_lanes(q_rows, 1, zeros)), 0, dbias_ref[...])
        j1 = key_block(1)
        side = (j1 > qi).astype(jnp.int32)
        scores_stage(j1, (qv_ref[side, 0], qv_ref[side, 1]), 1)
        for n in range(nblk):
            if n > 0:
                values_stage(key_block(n - 1), (n - 1) % 2, alpha=(n - 1) % 2)
            softmax_stage(key_block(n), n % ATTN_SLOTS, n % 2)
            if n + 2 < nblk:
                ja = key_block(n + 2)
                side = (ja > qi).astype(jnp.int32)
                scores_stage(ja, (qv_ref[side, 0], qv_ref[side, 1]), (n + 2) % ATTN_SLOTS)
        values_stage(key_block(nblk - 1), (nblk - 1) % 2, alpha=(nblk - 1) % 2)

    lamv = lamv_ref[...]
    lam_init = lam_init_ref[0]
    lam = (jnp.exp(jnp.sum(lamv[0:1] * lamv[1:2], axis=-1, keepdims=True))
           - jnp.exp(jnp.sum(lamv[2:3] * lamv[3:4], axis=-1, keepdims=True)) + lam_init)
    a1 = acc_ref[0]
    a2 = acc_ref[1]
    ot = a1[0:V_DIM] / a1[V_DIM:V_DIM + 1] - lam * (a2[0:V_DIM] / a2[V_DIM:V_DIM + 1])
    ot = ot * lax.rsqrt(jnp.mean(ot * ot, axis=0, keepdims=True) + EPS)
    o_ref[...] = (ot.T * (subg_ref[...] * (1.0 - lam_init))).astype(BF16)


def _attention(q, k, v, lamv, sub_g, q_g, k_g, slopes, lam_init, batch, seq):
    smem = pl.BlockSpec(memory_space=pltpu.SMEM)
    blk = ATTN_BLOCK
    nq = seq // blk
    k3 = k.reshape(batch, seq, D_MODEL)
    v3 = v.reshape(batch, seq, D_MODEL)
    shifted_heads = {}
    for h in range(N_HEADS):
        slope = 2.0 ** (-8.0 * (h + 1) / N_HEADS)
        dk = math.ceil((ATTN_ZERO_LOG2 / (slope * LOG2E) - 1.0) / blk)
        if 2 * dk + 1 <= ATTN_SHIFT_VISITS:
            shifted_heads[h] = dk
    return pl.pallas_call(
        functools.partial(_attn_kernel, shifted_heads=shifted_heads),
        grid=(batch, N_HEADS, nq),
        in_specs=[
            smem, smem,
            pl.BlockSpec((blk, V_DIM), lambda b, h, i: (b * nq + i, h)),
            pl.BlockSpec((None, seq, V_DIM), lambda b, h, i: (b, 0, h)),
            pl.BlockSpec((None, seq, V_DIM), lambda b, h, i: (b, 0, h)),
            pl.BlockSpec((4, HEAD_DIM), lambda b, h, i: (0, 0)),
            pl.BlockSpec((1, V_DIM), lambda b, h, i: (0, 0)),
            pl.BlockSpec((1, HEAD_DIM), lambda b, h, i: (0, 0)),
            pl.BlockSpec((1, HEAD_DIM), lambda b, h, i: (0, 0)),
        ],
        out_specs=pl.BlockSpec((blk, V_DIM), lambda b, h, i: (b * nq + i, h)),
        out_shape=jax.ShapeDtypeStruct((batch * seq, D_MODEL), BF16),
        scratch_shapes=[
            pltpu.VMEM((seq, V_DIM), BF16),
            pltpu.VMEM((seq, V_DIM), BF16),
            pltpu.VMEM((ATTN_VT_ROWS, seq), BF16),
            pltpu.VMEM((blk, blk), F32),
            pltpu.VMEM((8, 128), F32),
            pltpu.SMEM((1,), jnp.int32),
            pltpu.VMEM((3, 2, V_DIM, blk), BF16),
            pltpu.VMEM((ATTN_SLOTS, 2, blk, blk), F32),
            pltpu.VMEM((ATTN_SHIFT_VISITS, 2, blk, blk), BF16),
            pltpu.VMEM((2, 2, 1, blk), F32),
            pltpu.VMEM((2, 1, blk), F32),
            pltpu.VMEM((2, ATTN_VT_ROWS, blk), F32),
        ],
        compiler_params=pltpu.CompilerParams(
            dimension_semantics=("arbitrary", "arbitrary", "arbitrary"),
            vmem_limit_bytes=V7X_VMEM_LIMIT),
        name="diff_attn",
    )(slopes, lam_init, q, k3, v3, lamv, sub_g, q_g, k_g)


def _merge_kernel(x_ref, oa_ref, u_ref, sv_ref, ga_ref, gb_ref, sgw_ref, sgb_ref,
                  wa_ref, wb_ref, wo_ref, o_ref, ob_ref):
    for c in range(MERGE_ROWS // CHUNK):
        rows = slice(c * CHUNK, (c + 1) * CHUNK)
        for g in range(N_GROUPS):
            cols = slice(g * GROUP_DIM, (g + 1) * GROUP_DIM)
            mixed = jnp.dot(sgw_ref[g], sv_ref[rows, cols], preferred_element_type=F32)
            ob_ref[rows, cols] = (u_ref[rows, cols].astype(F32) * (mixed + sgb_ref[:, cols])).astype(BF16)
    ya = jnp.dot(oa_ref[...], wa_ref[...], preferred_element_type=F32)
    yb = jnp.dot(ob_ref[...], wb_ref[...], preferred_element_type=F32)
    merged = (ga_ref[...].astype(F32) * ya + gb_ref[...].astype(F32) * yb).astype(BF16)
    o_ref[...] = x_ref[...] + jnp.dot(merged, wo_ref[...], preferred_element_type=F32)


def _merge(x2, oa, u, sv, ga, gb, sgw, sgb_full, wa, wb, wo):
    t = x2.shape[0]
    row = lambda i: (i, 0)
    fixed2 = lambda i: (0, 0)
    tile = pl.BlockSpec((MERGE_ROWS, D_MODEL), row)
    wspec = pl.BlockSpec((D_MODEL, D_MODEL), fixed2, pipeline_mode=pl.Buffered(1))
    return pl.pallas_call(
        _merge_kernel,
        grid=(t // MERGE_ROWS,),
        in_specs=[
            tile, tile, tile, tile, tile, tile,
            pl.BlockSpec((N_GROUPS, CHUNK, CHUNK), lambda i: (0, 0, 0)),
            pl.BlockSpec((CHUNK, D_MODEL), fixed2),
            wspec, wspec, wspec,
        ],
        out_specs=tile,
        out_shape=jax.ShapeDtypeStruct((t, D_MODEL), F32),
        scratch_shapes=[pltpu.VMEM((MERGE_ROWS, D_MODEL), BF16)],
        compiler_params=pltpu.CompilerParams(
            dimension_semantics=("arbitrary",), vmem_limit_bytes=V7X_VMEM_LIMIT),
        name="sgu_merge",
    )(x2, oa, u, sv, ga, gb, sgw, sgb_full, wa, wb, wo)


def _ffn_kernel(x_ref, prev_ref, next_ref, g_ref, wup_ref, cw_ref, cb_ref, wdn_ref,
                o_ref, h_ref, act_ref, *, blocks_per_seq):
    i = pl.program_id(0)
    tm, halo = FFN_ROWS, FFN_HALO
    g = g_ref[...]
    x = x_ref[...]
    keep_prev = (i % blocks_per_seq != 0).astype(F32)
    keep_next = ((i + 1) % blocks_per_seq != 0).astype(F32)
    h_ref[0:halo, :] = (_rms_rows(prev_ref[...], g) * keep_prev).astype(BF16)
    h_ref[halo:halo + tm, :] = _rms_rows(x, g).astype(BF16)
    h_ref[halo + tm:, :] = (_rms_rows(next_ref[...], g) * keep_next).astype(BF16)
    h = h_ref[...]
    ext = tm + 2 * halo

    def conv(cols):
        up = jnp.dot(h, wup_ref[:, cols], preferred_element_type=F32)
        y = (pltpu.roll(up, 1, 0) * cw_ref[0:1, cols] + up * cw_ref[1:2, cols]
             + pltpu.roll(up, ext - 1, 0) * cw_ref[2:3, cols])
        return y[halo:halo + tm] + cb_ref[:, cols]

    for c in range(D_FF // FFN_COLS):
        gate = conv(slice(c * FFN_COLS, (c + 1) * FFN_COLS))
        val = conv(slice(D_FF + c * FFN_COLS, D_FF + (c + 1) * FFN_COLS))
        act_ref[:, c * FFN_COLS:(c + 1) * FFN_COLS] = (gate * _sigmoid(gate) * val).astype(BF16)
    o_ref[...] = x + jnp.dot(act_ref[...], wdn_ref[...], preferred_element_type=F32)


def _ffn(x2, ln_g, wup, conv_w, conv_b, wdn, seq):
    t = x2.shape[0]
    tm, halo = FFN_ROWS, FFN_HALO
    r = tm // halo
    last = t // halo - 1
    fixed = lambda i: (0, 0)
    return pl.pallas_call(
        functools.partial(_ffn_kernel, blocks_per_seq=seq // tm),
        grid=(t // tm,),
        in_specs=[
            pl.BlockSpec((tm, D_MODEL), lambda i: (i, 0)),
            pl.BlockSpec((halo, D_MODEL), lambda i: (jnp.maximum(i * r - 1, 0), 0)),
            pl.BlockSpec((halo, D_MODEL), lambda i: (jnp.minimum((i + 1) * r, last), 0)),
            pl.BlockSpec((1, D_MODEL), fixed),
            pl.BlockSpec((D_MODEL, 2 * D_FF), fixed, pipeline_mode=pl.Buffered(1)),
            pl.BlockSpec((3, 2 * D_FF), fixed),
            pl.BlockSpec((1, 2 * D_FF), fixed),
            pl.BlockSpec((D_FF, D_MODEL), fixed, pipeline_mode=pl.Buffered(1)),
        ],
        out_specs=pl.BlockSpec((tm, D_MODEL), lambda i: (i, 0)),
        out_shape=jax.ShapeDtypeStruct((t, D_MODEL), F32),
        scratch_shapes=[pltpu.VMEM((tm + 2 * halo, D_MODEL), BF16),
                        pltpu.VMEM((tm, D_FF), BF16)],
        compiler_params=pltpu.CompilerParams(
            dimension_semantics=("arbitrary",), vmem_limit_bytes=V7X_VMEM_LIMIT),
        name="conv_ffn",
    )(x2, x2, x2, ln_g, wup, conv_w, conv_b, wdn)


def kernel(x, ln_mix_g, w_in, q_norm_g, k_norm_g, lambda_q1, lambda_k1, lambda_q2, lambda_k2,
           subln_g, sg_norm_g, sg_w, sg_b, w_branch_a, w_branch_b, w_out, ln_ffn_g, w_up,
           conv_w, conv_b, w_down):
    batch, seq, d = x.shape
    depth = w_in.shape[0]
    slopes = 2.0 ** (-8.0 * jnp.arange(1, N_HEADS + 1, dtype=F32) / N_HEADS)
    x2 = x.reshape(batch * seq, d)
    for l in range(depth):
        lam_init = jnp.full((1,), 0.8 - 0.6 * math.exp(-0.3 * l), F32)
        q, k, v, u, sv, ga, gb = _projection(
            x2, ln_mix_g[l][None], w_in[l].astype(BF16),
            jnp.tile(q_norm_g[l], 2 * N_HEADS)[None], jnp.tile(k_norm_g[l], 2 * N_HEADS)[None],
            sg_norm_g[l][None])
        lamv = jnp.stack([lambda_q1[l], lambda_k1[l], lambda_q2[l], lambda_k2[l]])
        oa = _attention(q, k, v, lamv, subln_g[l][None], q_norm_g[l][None], k_norm_g[l][None],
                        slopes, lam_init, batch, seq)
        sgb_full = jnp.repeat(sg_b[l].T, GROUP_DIM, axis=1)
        x2 = _merge(x2, oa, u, sv, ga, gb, sg_w[l].astype(BF16), sgb_full,
                    w_branch_a[l].astype(BF16), w_branch_b[l].astype(BF16), w_out[l].astype(BF16))
        x2 = _ffn(x2, ln_ffn_g[l][None], w_up[l].astype(BF16), conv_w[l], conv_b[l][None],
                  w_down[l].astype(BF16), seq)
    return x2.reshape(batch, seq, d)
```

```python
import functools
import math

import jax
import jax.numpy as jnp
from jax import lax
from jax.experimental import pallas as pl
from jax.experimental.pallas import tpu as pltpu

F32 = jnp.float32
BF16 = jnp.bfloat16

D_MODEL = 1024
N_HEADS = 8
HEAD_DIM = 64
V_DIM = 2 * HEAD_DIM
CHUNK = 128
N_GROUPS = 8
GROUP_DIM = D_MODEL // N_GROUPS
D_FF = 2816
EPS = 1e-6
N_SECTIONS = 7

V7X_MXU_DIM = 256
V7X_VMEM_LIMIT = 56 * 1024 * 1024

PROJ_ROWS = 512
PROJ_COLS = 512
ATTN_BLOCK = 256
ATTN_VT_ROWS = V_DIM + 16
ATTN_SLOTS = 4
LOG2E = math.log2(math.e)
ATTN_MAX_SHIFT = 30.0
ATTN_ZERO_LOG2 = 160.0
ATTN_NULL_SCORE = 30000.0
MERGE_ROWS = 512
FFN_ROWS = 512
FFN_HALO = 16
FFN_COLS = 512


def _rms_rows(x, g):
    return x * lax.rsqrt(jnp.mean(x * x, axis=-1, keepdims=True) + EPS) * g


def _gelu_tanh(x):
    c = math.sqrt(2.0 / math.pi)
    return 0.5 * x * (1.0 + jnp.tanh(c * (x + 0.044715 * (x * x * x))))


def _sigmoid(x):
    return 1.0 / (1.0 + jnp.exp(-x))


def _group_mean_matrix(group):
    r = lax.broadcasted_iota(jnp.int32, (V7X_MXU_DIM, V7X_MXU_DIM), 0) // group
    c = lax.broadcasted_iota(jnp.int32, (V7X_MXU_DIM, V7X_MXU_DIM), 1) // group
    return jnp.where(r == c, 1.0 / group, 0.0).astype(BF16)


def _group_rms_scale(a, mean_mat):
    sq = (a * a).astype(BF16)
    w = mean_mat.shape[0]
    ms = jnp.concatenate([jnp.dot(sq[:, lo:lo + w], mean_mat, preferred_element_type=F32)
                          for lo in range(0, a.shape[1], w)], axis=1)
    return lax.rsqrt(ms + EPS)


def _proj_kernel(x_ref, g_ref, w_ref, qg_ref, kg_ref, sgg_ref,
                 q_ref, k_ref, v_ref, u_ref, sv_ref, ga_ref, gb_ref):
    h = _rms_rows(x_ref[...], g_ref[...]).astype(BF16)
    mean64 = _group_mean_matrix(HEAD_DIM)
    mean128 = _group_mean_matrix(GROUP_DIM)
    w = PROJ_COLS

    def section(s, c):
        lo = s * D_MODEL + c * w
        return jnp.dot(h, w_ref[:, lo:lo + w], preferred_element_type=F32)

    for c in range(D_MODEL // w):
        cols = slice(c * w, (c + 1) * w)
        a = section(0, c)
        q_ref[:, cols] = (a * _group_rms_scale(a, mean64) * (qg_ref[:, cols] * (HEAD_DIM ** -0.5 * LOG2E))).astype(BF16)
        a = section(1, c)
        k_ref[:, cols] = (a * _group_rms_scale(a, mean64) * kg_ref[:, cols]).astype(BF16)
        v_ref[:, cols] = section(2, c).astype(BF16)
        u_ref[:, cols] = _gelu_tanh(section(3, c)).astype(BF16)
        a = _gelu_tanh(section(4, c))
        sv_ref[:, cols] = (a * _group_rms_scale(a, mean128) * sgg_ref[:, cols]).astype(BF16)
        ga_ref[:, cols] = _sigmoid(section(5, c)).astype(BF16)
        gb_ref[:, cols] = _sigmoid(section(6, c)).astype(BF16)


def _projection(x2, ln_g, w_in, q_g, k_g, sg_g):
    t = x2.shape[0]
    row = lambda i: (i, 0)
    fixed = lambda i: (0, 0)
    vec = pl.BlockSpec((1, D_MODEL), fixed)
    out = jax.ShapeDtypeStruct((t, D_MODEL), BF16)
    return pl.pallas_call(
        _proj_kernel,
        grid=(t // PROJ_ROWS,),
        in_specs=[
            pl.BlockSpec((PROJ_ROWS, D_MODEL), row),
            vec,
            pl.BlockSpec((D_MODEL, N_SECTIONS * D_MODEL), fixed, pipeline_mode=pl.Buffered(1)),
            vec, vec, vec,
        ],
        out_specs=[pl.BlockSpec((PROJ_ROWS, D_MODEL), row)] * N_SECTIONS,
        out_shape=[out] * N_SECTIONS,
        compiler_params=pltpu.CompilerParams(
            dimension_semantics=("arbitrary",), vmem_limit_bytes=V7X_VMEM_LIMIT),
        name="proj",
    )(x2, ln_g, w_in, q_g, k_g, sg_g)


def _split3(c):
    hi = c.astype(BF16)
    mid = (c - hi.astype(F32)).astype(BF16)
    lo = (c - hi.astype(F32) - mid.astype(F32)).astype(BF16)
    return hi.astype(F32), mid.astype(F32), lo.astype(F32)


def _pick_rows(row, entries, default=0.0):
    out = default
    for lo, hi, val in reversed(entries):
        out = jnp.where((row >= lo) & (row < hi), val, out)
    return out


def _attn_kernel(slopes_ref, lam_init_ref, q_ref, k_ref, v_ref, lamv_ref, subg_ref, qg_ref, kg_ref, o_ref,
                 k1_ref, k2_ref, vt_ref, dbias_ref, shift_ref, flag_ref, qv_ref, s_ref, p_ref,
                 alpha_ref, m_ref, acc_ref, *, shifted_heads):
    head = pl.program_id(1)
    qi = pl.program_id(2)
    blk = ATTN_BLOCK
    seq = k_ref.shape[0]
    nblk = seq // blk
    c = jnp.full((1, 1), slopes_ref[head] * LOG2E, F32)
    c3 = _split3(c)
    c_blk = c * float(blk)

    @pl.when(qi == 0)
    def _():
        lane_b = lax.broadcasted_iota(jnp.int32, (blk, V_DIM), 1)
        off = lax.broadcasted_iota(jnp.int32, (blk, V_DIM), 0).astype(F32)
        for mp, ref in enumerate((k1_ref, k2_ref)):
            a = lane_b - (HEAD_DIM if mp == 0 else 0)
            own_b = (lane_b < HEAD_DIM) if mp == 0 else (lane_b >= HEAD_DIM)
            fixed = _pick_rows(a, [(0, 1, -c3[0]), (1, 2, -c3[1]), (2, 3, -c3[2]), (3, 6, off), (9, 12, 1.0)])
            for jb in range(nblk):
                cb3 = _split3(c_blk * float(jb))
                aug = _pick_rows(a, [(6, 7, cb3[0]), (7, 8, cb3[1]), (8, 9, cb3[2])], fixed)
                rows = slice(jb * blk, (jb + 1) * blk)
                ref[rows, :] = jnp.where(own_b, k_ref[rows, :].astype(F32), aug).astype(BF16)
        gmax = (jnp.max(jnp.abs(qg_ref[...]), axis=1, keepdims=True)
                * jnp.max(jnp.abs(kg_ref[...]), axis=1, keepdims=True))
        bound = 1.01 * (HEAD_DIM * HEAD_DIM ** -0.5 * LOG2E) * gmax + 0.01
        shift_ref[...] = jnp.broadcast_to(bound, shift_ref.shape)
        flag_ref[0] = (bound[0, 0] <= ATTN_MAX_SHIFT).astype(jnp.int32)
        step = 512
        for s0 in range(0, seq, step):
            vt_ref[0:V_DIM, s0:s0 + step] = v_ref[s0:s0 + step, :].astype(F32).T.astype(BF16)
        r = lax.broadcasted_iota(jnp.int32, (ATTN_VT_ROWS - V_DIM, seq), 0)
        vt_ref[V_DIM:, :] = jnp.where(r == 0, 1.0, 0.0).astype(BF16)
        d = lax.broadcasted_iota(jnp.int32, (blk, blk), 0) - lax.broadcasted_iota(jnp.int32, (blk, blk), 1)
        dbias_ref[...] = -(c * jnp.abs(d).astype(F32))

    shifted_ok = flag_ref[0] == 1
    row = lax.broadcasted_iota(jnp.int32, (HEAD_DIM, blk), 0)
    il = lax.broadcasted_iota(jnp.int32, (HEAD_DIM, blk), 1).astype(F32)

    def query_rows():
        qt = q_ref[...].astype(F32).T
        return qt[0:HEAD_DIM], qt[HEAD_DIM:]

    def with_lanes(q_rows, mp, aug):
        parts = [q_rows[0], aug] if mp == 0 else [aug, q_rows[1]]
        return jnp.concatenate(parts, axis=0).astype(BF16)

    def offset_lanes(sigma):
        return [(0, 3, sigma * il), (3, 4, sigma * c3[0]), (4, 5, sigma * c3[1]), (5, 6, sigma * c3[2])]

    def values_stage(j, pslot, alpha=None):
        start = pl.multiple_of(j * blk, blk)
        vt = vt_ref[:, pl.ds(start, blk)]
        for mp in range(2):
            pv = jnp.dot(vt, p_ref[pslot, mp], preferred_element_type=F32)
            acc_ref[mp] = (acc_ref[mp] if alpha is None else acc_ref[mp] * alpha_ref[alpha, mp]) + pv

    def shifted_path(dk):
        cbi = c_blk * qi.astype(F32)
        q_rows = query_rows()
        diag_q = []
        shift = shift_ref[0:1, 0:1]
        lo3 = _split3(-cbi - shift)
        hi3 = _split3(cbi - shift)
        s3 = _split3(-shift)
        for mp in range(2):
            qv_ref[0, mp] = with_lanes(q_rows, mp, _pick_rows(row, offset_lanes(1.0) + [
                (6, 9, 1.0), (9, 10, lo3[0]), (10, 11, lo3[1]), (11, 12, lo3[2])]))
            qv_ref[1, mp] = with_lanes(q_rows, mp, _pick_rows(row, offset_lanes(-1.0) + [
                (6, 9, -1.0), (9, 10, hi3[0]), (10, 11, hi3[1]), (11, 12, hi3[2])]))
            qv_ref[2, mp] = with_lanes(q_rows, mp, _pick_rows(row, [(9, 10, -ATTN_NULL_SCORE)]))
            diag_q.append(with_lanes(q_rows, mp, _pick_rows(
                row, [(9, 10, s3[0]), (10, 11, s3[1]), (11, 12, s3[2])])))

        def visit_block(u):
            if dk is None:
                j = (u - 1) + (u - 1 >= qi).astype(jnp.int32)
                return j, (j > qi).astype(jnp.int32)
            j = qi - dk + (u - 1) if u <= dk else qi + (u - dk)
            valid = (j >= 0) & (j < nblk)
            return jnp.clip(j, 0, nblk - 1), jnp.where(valid, 0 if u <= dk else 1, 2)

        def scores_stage(u):
            if u == 0:
                j, qts, bias = qi, diag_q, dbias_ref[...]
            else:
                j, var = visit_block(u)
                qts, bias = (qv_ref[var, 0], qv_ref[var, 1]), None
            start = pl.multiple_of(j * blk, blk)
            for mp, kref in enumerate((k1_ref, k2_ref)):
                s = jnp.dot(kref[pl.ds(start, blk), :], qts[mp], preferred_element_type=F32)
                p_ref[u, mp] = jnp.exp2(s if bias is None else s + bias).astype(BF16)

        def weighted_values(u, acc):
            j = qi if u == 0 else visit_block(u)[0]
            vt = vt_ref[:, pl.ds(pl.multiple_of(j * blk, blk), blk)]
            pv = [jnp.dot(vt, p_ref[u, mp], preferred_element_type=F32) for mp in range(2)]
            return pv if acc is None else [acc[mp] + pv[mp] for mp in range(2)]

        visits = nblk if dk is None else 2 * dk + 1
        acc = None
        scores_stage(0)
        for u in range(visits):
            if u + 1 < visits:
                scores_stage(u + 1)
            acc = weighted_values(u, acc)
        for mp in range(2):
            acc_ref[mp] = acc[mp]

    heads_by_radius = {}
    for h, dk in shifted_heads.items():
        heads_by_radius.setdefault(dk, []).append(h)
    for dk, heads in heads_by_radius.items():
        in_group = functools.reduce(jnp.logical_or, [head == h for h in heads])
        pl.when(shifted_ok & in_group)(functools.partial(shifted_path, dk))

    @pl.when(jnp.logical_not(shifted_ok))
    def _():
        def key_block(n):
            return jnp.where(n == 0, qi, n - (n <= qi).astype(jnp.int32))

        def scores_stage(j, qts, slot, bias=None):
            start = pl.multiple_of(j * blk, blk)
            for mp, kref in enumerate((k1_ref, k2_ref)):
                s = jnp.dot(kref[pl.ds(start, blk), :], qts[mp], preferred_element_type=F32)
                s_ref[slot, mp] = s if bias is None else s + bias

        def softmax_stage(j, slot, pslot):
            const = -(c_blk * jnp.abs(qi - j).astype(F32))
            for mp in range(2):
                s = s_ref[slot, mp]
                m_old = m_ref[mp]
                m_new = jnp.maximum(m_old, jnp.max(s, axis=0, keepdims=True) + const)
                p_ref[pslot, mp] = jnp.exp2(s - (m_new - const)).astype(BF16)
                alpha_ref[pslot, mp] = jnp.exp2(m_old - m_new)
                m_ref[mp] = m_new

        q_rows = query_rows()
        for mp in range(2):
            qv_ref[0, mp] = with_lanes(q_rows, mp, _pick_rows(row, offset_lanes(1.0)))
            qv_ref[1, mp] = with_lanes(q_rows, mp, _pick_rows(row, offset_lanes(-1.0)))
        m_ref[...] = jnp.full(m_ref.shape, -jnp.inf, F32)
        acc_ref[...] = jnp.zeros(acc_ref.shape, F32)
        zeros = jnp.zeros((HEAD_DIM, blk), F32)
        scores_stage(qi, (with_lanes(q_rows, 0, zeros), with_lanes(q_rows, 1, zeros)), 0, dbias_ref[...])
        j1 = key_block(1)
        side = (j1 > qi).astype(jnp.int32)
        scores_stage(j1, (qv_ref[side, 0], qv_ref[side, 1]), 1)
        for n in range(nblk):
            if n > 0:
                values_stage(key_block(n - 1), (n - 1) % 2, alpha=(n - 1) % 2)
            softmax_stage(key_block(n), n % ATTN_SLOTS, n % 2)
            if n + 2 < nblk:
                ja = key_block(n + 2)
                side = (ja > qi).astype(jnp.int32)
                scores_stage(ja, (qv_ref[side, 0], qv_ref[side, 1]), (n + 2) % ATTN_SLOTS)
        values_stage(key_block(nblk - 1), (nblk - 1) % 2, alpha=(nblk - 1) % 2)

    lamv = lamv_ref[...]
    lam_init = lam_init_ref[0]
    lam = (jnp.exp(jnp.sum(lamv[0:1] * lamv[1:2], axis=-1, keepdims=True))
           - jnp.exp(jnp.sum(lamv[2:3] * lamv[3:4], axis=-1, keepdims=True)) + lam_init)
    a1 = acc_ref[0]
    a2 = acc_ref[1]
    ot = a1[0:V_DIM] / a1[V_DIM:V_DIM + 1] - lam * (a2[0:V_DIM] / a2[V_DIM:V_DIM + 1])
    ot = ot * lax.rsqrt(jnp.mean(ot * ot, axis=0, keepdims=True) + EPS)
    o_ref[...] = (ot.T * (subg_ref[...] * (1.0 - lam_init))).astype(BF16)


def _attention(q, k, v, lamv, sub_g, q_g, k_g, slopes, lam_init, batch, seq):
    smem = pl.BlockSpec(memory_space=pltpu.SMEM)
    blk = ATTN_BLOCK
    nq = seq // blk
    k3 = k.reshape(batch, seq, D_MODEL)
    v3 = v.reshape(batch, seq, D_MODEL)
    shifted_heads = {}
    for h in range(N_HEADS):
        slope = 2.0 ** (-8.0 * (h + 1) / N_HEADS)
        dk = math.ceil((ATTN_ZERO_LOG2 / (slope * LOG2E) - 1.0) / blk)
        shifted_heads[h] = dk if 2 * dk + 1 < nq else None
    return pl.pallas_call(
        functools.partial(_attn_kernel, shifted_heads=shifted_heads),
        grid=(batch, N_HEADS, nq),
        in_specs=[
            smem, smem,
            pl.BlockSpec((blk, V_DIM), lambda b, h, i: (b * nq + i, h)),
            pl.BlockSpec((None, seq, V_DIM), lambda b, h, i: (b, 0, h)),
            pl.BlockSpec((None, seq, V_DIM), lambda b, h, i: (b, 0, h)),
            pl.BlockSpec((4, HEAD_DIM), lambda b, h, i: (0, 0)),
            pl.BlockSpec((1, V_DIM), lambda b, h, i: (0, 0)),
            pl.BlockSpec((1, HEAD_DIM), lambda b, h, i: (0, 0)),
            pl.BlockSpec((1, HEAD_DIM), lambda b, h, i: (0, 0)),
        ],
        out_specs=pl.BlockSpec((blk, V_DIM), lambda b, h, i: (b * nq + i, h)),
        out_shape=jax.ShapeDtypeStruct((batch * seq, D_MODEL), BF16),
        scratch_shapes=[
            pltpu.VMEM((seq, V_DIM), BF16),
            pltpu.VMEM((seq, V_DIM), BF16),
            pltpu.VMEM((ATTN_VT_ROWS, seq), BF16),
            pltpu.VMEM((blk, blk), F32),
            pltpu.VMEM((8, 128), F32),
            pltpu.SMEM((1,), jnp.int32),
            pltpu.VMEM((3, 2, V_DIM, blk), BF16),
            pltpu.VMEM((ATTN_SLOTS, 2, blk, blk), F32),
            pltpu.VMEM((nq, 2, blk, blk), BF16),
            pltpu.VMEM((2, 2, 1, blk), F32),
            pltpu.VMEM((2, 1, blk), F32),
            pltpu.VMEM((2, ATTN_VT_ROWS, blk), F32),
        ],
        compiler_params=pltpu.CompilerParams(
            dimension_semantics=("arbitrary", "arbitrary", "arbitrary"),
            vmem_limit_bytes=V7X_VMEM_LIMIT),
        name="diff_attn",
    )(slopes, lam_init, q, k3, v3, lamv, sub_g, q_g, k_g)


def _merge_kernel(x_ref, oa_ref, u_ref, sv_ref, ga_ref, gb_ref, sgw_ref, sgb_ref,
                  wa_ref, wb_ref, wo_ref, o_ref, ob_ref):
    for c in range(MERGE_ROWS // CHUNK):
        rows = slice(c * CHUNK, (c + 1) * CHUNK)
        for g in range(N_GROUPS):
            cols = slice(g * GROUP_DIM, (g + 1) * GROUP_DIM)
            mixed = jnp.dot(sgw_ref[g], sv_ref[rows, cols], preferred_element_type=F32)
            ob_ref[rows, cols] = (u_ref[rows, cols].astype(F32) * (mixed + sgb_ref[:, cols])).astype(BF16)
    ya = jnp.dot(oa_ref[...], wa_ref[...], preferred_element_type=F32)
    yb = jnp.dot(ob_ref[...], wb_ref[...], preferred_element_type=F32)
    merged = (ga_ref[...].astype(F32) * ya + gb_ref[...].astype(F32) * yb).astype(BF16)
    o_ref[...] = x_ref[...] + jnp.dot(merged, wo_ref[...], preferred_element_type=F32)


def _merge(x2, oa, u, sv, ga, gb, sgw, sgb_full, wa, wb, wo):
    t = x2.shape[0]
    row = lambda i: (i, 0)
    fixed2 = lambda i: (0, 0)
    tile = pl.BlockSpec((MERGE_ROWS, D_MODEL), row)
    wspec = pl.BlockSpec((D_MODEL, D_MODEL), fixed2, pipeline_mode=pl.Buffered(1))
    return pl.pallas_call(
        _merge_kernel,
        grid=(t // MERGE_ROWS,),
        in_specs=[
            tile, tile, tile, tile, tile, tile,
            pl.BlockSpec((N_GROUPS, CHUNK, CHUNK), lambda i: (0, 0, 0)),
            pl.BlockSpec((CHUNK, D_MODEL), fixed2),
            wspec, wspec, wspec,
        ],
        out_specs=tile,
        out_shape=jax.ShapeDtypeStruct((t, D_MODEL), F32),
        scratch_shapes=[pltpu.VMEM((MERGE_ROWS, D_MODEL), BF16)],
        compiler_params=pltpu.CompilerParams(
            dimension_semantics=("arbitrary",), vmem_limit_bytes=V7X_VMEM_LIMIT),
        name="sgu_merge",
    )(x2, oa, u, sv, ga, gb, sgw, sgb_full, wa, wb, wo)


def _ffn_kernel(x_ref, prev_ref, next_ref, g_ref, wup_ref, cw_ref, cb_ref, wdn_ref,
                o_ref, h_ref, act_ref, *, blocks_per_seq):
    i = pl.program_id(0)
    tm, halo = FFN_ROWS, FFN_HALO
    g = g_ref[...]
    x = x_ref[...]
    keep_prev = (i % blocks_per_seq != 0).astype(F32)
    keep_next = ((i + 1) % blocks_per_seq != 0).astype(F32)
    h_ref[0:halo, :] = (_rms_rows(prev_ref[...], g) * keep_prev).astype(BF16)
    h_ref[halo:halo + tm, :] = _rms_rows(x, g).astype(BF16)
    h_ref[halo + tm:, :] = (_rms_rows(next_ref[...], g) * keep_next).astype(BF16)
    h = h_ref[...]
    ext = tm + 2 * halo

    def conv(cols):
        up = jnp.dot(h, wup_ref[:, cols], preferred_element_type=F32)
        y = (pltpu.roll(up, 1, 0) * cw_ref[0:1, cols] + up * cw_ref[1:2, cols]
             + pltpu.roll(up, ext - 1, 0) * cw_ref[2:3, cols])
        return y[halo:halo + tm] + cb_ref[:, cols]

    for lo in range(0, D_FF, FFN_COLS):
        hi = min(lo + FFN_COLS, D_FF)
        gate = conv(slice(lo, hi))
        val = conv(slice(D_FF + lo, D_FF + hi))
        act_ref[:, lo:hi] = (gate * _sigmoid(gate) * val).astype(BF16)
    o_ref[...] = x + jnp.dot(act_ref[...], wdn_ref[...], preferred_element_type=F32)


def _ffn(x2, ln_g, wup, conv_w, conv_b, wdn, seq):
    t = x2.shape[0]
    tm, halo = FFN_ROWS, FFN_HALO
    r = tm // halo
    last = t // halo - 1
    fixed = lambda i: (0, 0)
    return pl.pallas_call(
        functools.partial(_ffn_kernel, blocks_per_seq=seq // tm),
        grid=(t // tm,),
        in_specs=[
            pl.BlockSpec((tm, D_MODEL), lambda i: (i, 0)),
            pl.BlockSpec((halo, D_MODEL), lambda i: (jnp.maximum(i * r - 1, 0), 0)),
            pl.BlockSpec((halo, D_MODEL), lambda i: (jnp.minimum((i + 1) * r, last), 0)),
            pl.BlockSpec((1, D_MODEL), fixed),
            pl.BlockSpec((D_MODEL, 2 * D_FF), fixed, pipeline_mode=pl.Buffered(1)),
            pl.BlockSpec((3, 2 * D_FF), fixed),
            pl.BlockSpec((1, 2 * D_FF), fixed),
            pl.BlockSpec((D_FF, D_MODEL), fixed, pipeline_mode=pl.Buffered(1)),
        ],
        out_specs=pl.BlockSpec((tm, D_MODEL), lambda i: (i, 0)),
        out_shape=jax.ShapeDtypeStruct((t, D_MODEL), F32),
        scratch_shapes=[pltpu.VMEM((tm + 2 * halo, D_MODEL), BF16),
                        pltpu.VMEM((tm, D_FF), BF16)],
        compiler_params=pltpu.CompilerParams(
            dimension_semantics=("arbitrary",), vmem_limit_bytes=V7X_VMEM_LIMIT),
        name="conv_ffn",
    )(x2, x2, x2, ln_g, wup, conv_w, conv_b, wdn)


def kernel(x, ln_mix_g, w_in, q_norm_g, k_norm_g, lambda_q1, lambda_k1, lambda_q2, lambda_k2,
           subln_g, sg_norm_g, sg_w, sg_b, w_branch_a, w_branch_b, w_out, ln_ffn_g, w_up,
           conv_w, conv_b, w_down):
    batch, seq, d = x.shape
    depth = w_in.shape[0]
    slopes = 2.0 ** (-8.0 * jnp.arange(1, N_HEADS + 1, dtype=F32) / N_HEADS)
    x2 = x.reshape(batch * seq, d)
    for l in range(depth):
        lam_init = jnp.full((1,), 0.8 - 0.6 * math.exp(-0.3 * l), F32)
        q, k, v, u, sv, ga, gb = _projection(
            x2, ln_mix_g[l][None], w_in[l].astype(BF16),
            jnp.tile(q_norm_g[l], 2 * N_HEADS)[None], jnp.tile(k_norm_g[l], 2 * N_HEADS)[None],
            sg_norm_g[l][None])
        lamv = jnp.stack([lambda_q1[l], lambda_k1[l], lambda_q2[l], lambda_k2[l]])
        oa = _attention(q, k, v, lamv, subln_g[l][None], q_norm_g[l][None], k_norm_g[l][None],
                        slopes, lam_init, batch, seq)
        sgb_full = jnp.repeat(sg_b[l].T, GROUP_DIM, axis=1)
        x2 = _merge(x2, oa, u, sv, ga, gb, sg_w[l].astype(BF16), sgb_full,
                    w_branch_a[l].astype(BF16), w_branch_b[l].astype(BF16), w_out[l].astype(BF16))
        x2 = _ffn(x2, ln_ffn_g[l][None], w_up[l].astype(BF16), conv_w[l], conv_b[l][None],
                  w_down[l].astype(BF16), seq)
    return x2.reshape(batch, seq, d)
```

```python
import functools
import math

import jax
import jax.numpy as jnp
from jax import lax
from jax.experimental import pallas as pl
from jax.experimental.pallas import tpu as pltpu

F32 = jnp.float32
BF16 = jnp.bfloat16

D_MODEL = 1024
N_HEADS = 8
HEAD_DIM = 64
V_DIM = 2 * HEAD_DIM
CHUNK = 128
N_GROUPS = 8
GROUP_DIM = D_MODEL // N_GROUPS
D_FF = 2816
EPS = 1e-6
N_SECTIONS = 7

V7X_MXU_DIM = 256
V7X_VMEM_LIMIT = 56 * 1024 * 1024

PROJ_ROWS = 512
PROJ_COLS = 512
ATTN_BLOCK = 256
ATTN_VT_ROWS = V_DIM + 16
ATTN_SLOTS = 4
ATTN_TILES = 2
LOG2E = math.log2(math.e)
ATTN_MAX_SHIFT = 30.0
ATTN_ZERO_LOG2 = 160.0
ATTN_NULL_SCORE = 30000.0
MERGE_ROWS = 512
FFN_ROWS = 512
FFN_HALO = 16
FFN_COLS = 512


def _rms_rows(x, g):
    return x * lax.rsqrt(jnp.mean(x * x, axis=-1, keepdims=True) + EPS) * g


def _gelu_tanh(x):
    c = math.sqrt(2.0 / math.pi)
    return 0.5 * x * (1.0 + jnp.tanh(c * (x + 0.044715 * (x * x * x))))


def _sigmoid(x):
    return 1.0 / (1.0 + jnp.exp(-x))


def _group_mean_matrix(group):
    r = lax.broadcasted_iota(jnp.int32, (V7X_MXU_DIM, V7X_MXU_DIM), 0) // group
    c = lax.broadcasted_iota(jnp.int32, (V7X_MXU_DIM, V7X_MXU_DIM), 1) // group
    return jnp.where(r == c, 1.0 / group, 0.0).astype(BF16)


def _group_rms_scale(a, mean_mat):
    sq = (a * a).astype(BF16)
    w = mean_mat.shape[0]
    ms = jnp.concatenate([jnp.dot(sq[:, lo:lo + w], mean_mat, preferred_element_type=F32)
                          for lo in range(0, a.shape[1], w)], axis=1)
    return lax.rsqrt(ms + EPS)


def _proj_kernel(x_ref, g_ref, w_ref, qg_ref, kg_ref, sgg_ref,
                 q_ref, k_ref, v_ref, u_ref, sv_ref, ga_ref, gb_ref):
    h = _rms_rows(x_ref[...], g_ref[...]).astype(BF16)
    mean64 = _group_mean_matrix(HEAD_DIM)
    mean128 = _group_mean_matrix(GROUP_DIM)
    w = PROJ_COLS

    def section(s, c):
        lo = s * D_MODEL + c * w
        return jnp.dot(h, w_ref[:, lo:lo + w], preferred_element_type=F32)

    for c in range(D_MODEL // w):
        cols = slice(c * w, (c + 1) * w)
        a = section(0, c)
        q_ref[:, cols] = (a * _group_rms_scale(a, mean64) * (qg_ref[:, cols] * (HEAD_DIM ** -0.5 * LOG2E))).astype(BF16)
        a = section(1, c)
        k_ref[:, cols] = (a * _group_rms_scale(a, mean64) * kg_ref[:, cols]).astype(BF16)
        v_ref[:, cols] = section(2, c).astype(BF16)
        u_ref[:, cols] = _gelu_tanh(section(3, c)).astype(BF16)
        a = _gelu_tanh(section(4, c))
        sv_ref[:, cols] = (a * _group_rms_scale(a, mean128) * sgg_ref[:, cols]).astype(BF16)
        ga_ref[:, cols] = _sigmoid(section(5, c)).astype(BF16)
        gb_ref[:, cols] = _sigmoid(section(6, c)).astype(BF16)


def _projection(x2, ln_g, w_in, q_g, k_g, sg_g):
    t = x2.shape[0]
    row = lambda i: (i, 0)
    fixed = lambda i: (0, 0)
    vec = pl.BlockSpec((1, D_MODEL), fixed)
    out = jax.ShapeDtypeStruct((t, D_MODEL), BF16)
    return pl.pallas_call(
        _proj_kernel,
        grid=(t // PROJ_ROWS,),
        in_specs=[
            pl.BlockSpec((PROJ_ROWS, D_MODEL), row),
            vec,
            pl.BlockSpec((D_MODEL, N_SECTIONS * D_MODEL), fixed, pipeline_mode=pl.Buffered(1)),
            vec, vec, vec,
        ],
        out_specs=[pl.BlockSpec((PROJ_ROWS, D_MODEL), row)] * N_SECTIONS,
        out_shape=[out] * N_SECTIONS,
        compiler_params=pltpu.CompilerParams(
            dimension_semantics=("arbitrary",), vmem_limit_bytes=V7X_VMEM_LIMIT),
        name="proj",
    )(x2, ln_g, w_in, q_g, k_g, sg_g)


def _split3(c):
    hi = c.astype(BF16)
    mid = (c - hi.astype(F32)).astype(BF16)
    lo = (c - hi.astype(F32) - mid.astype(F32)).astype(BF16)
    return hi.astype(F32), mid.astype(F32), lo.astype(F32)


def _pick_rows(row, entries, default=0.0):
    out = default
    for lo, hi, val in reversed(entries):
        out = jnp.where((row >= lo) & (row < hi), val, out)
    return out


def _attn_kernel(slopes_ref, lam_init_ref, q_ref, k_ref, v_ref, lamv_ref, subg_ref, qg_ref, kg_ref, o_ref,
                 k1_ref, k2_ref, vt_ref, dbias_ref, shift_ref, flag_ref, qv_ref, s_ref, p_ref,
                 alpha_ref, m_ref, acc_ref, *, shifted_heads):
    head = pl.program_id(1)
    step = pl.program_id(2)
    blk = ATTN_BLOCK
    seq = k_ref.shape[0]
    nblk = seq // blk
    c = jnp.full((1, 1), slopes_ref[head] * LOG2E, F32)
    c3 = _split3(c)
    c_blk = c * float(blk)

    @pl.when(step == 0)
    def _():
        lane_b = lax.broadcasted_iota(jnp.int32, (blk, V_DIM), 1)
        off = lax.broadcasted_iota(jnp.int32, (blk, V_DIM), 0).astype(F32)
        for mp, ref in enumerate((k1_ref, k2_ref)):
            a = lane_b - (HEAD_DIM if mp == 0 else 0)
            own_b = (lane_b < HEAD_DIM) if mp == 0 else (lane_b >= HEAD_DIM)
            fixed = _pick_rows(a, [(0, 1, -c3[0]), (1, 2, -c3[1]), (2, 3, -c3[2]), (3, 6, off), (9, 12, 1.0)])
            for jb in range(nblk):
                cb3 = _split3(c_blk * float(jb))
                aug = _pick_rows(a, [(6, 7, cb3[0]), (7, 8, cb3[1]), (8, 9, cb3[2])], fixed)
                rows = slice(jb * blk, (jb + 1) * blk)
                ref[rows, :] = jnp.where(own_b, k_ref[rows, :].astype(F32), aug).astype(BF16)
        gmax = (jnp.max(jnp.abs(qg_ref[...]), axis=1, keepdims=True)
                * jnp.max(jnp.abs(kg_ref[...]), axis=1, keepdims=True))
        bound = 1.01 * (HEAD_DIM * HEAD_DIM ** -0.5 * LOG2E) * gmax + 0.01
        shift_ref[...] = jnp.broadcast_to(bound, shift_ref.shape)
        flag_ref[0] = (bound[0, 0] <= ATTN_MAX_SHIFT).astype(jnp.int32)
        cols = 512
        for s0 in range(0, seq, cols):
            vt_ref[0:V_DIM, s0:s0 + cols] = v_ref[s0:s0 + cols, :].astype(F32).T.astype(BF16)
        r = lax.broadcasted_iota(jnp.int32, (ATTN_VT_ROWS - V_DIM, seq), 0)
        vt_ref[V_DIM:, :] = jnp.where(r == 0, 1.0, 0.0).astype(BF16)
        d = lax.broadcasted_iota(jnp.int32, (blk, blk), 0) - lax.broadcasted_iota(jnp.int32, (blk, blk), 1)
        dbias_ref[...] = -(c * jnp.abs(d).astype(F32))

    shifted_ok = flag_ref[0] == 1
    tiles = ATTN_TILES
    row = lax.broadcasted_iota(jnp.int32, (HEAD_DIM, blk), 0)
    il = lax.broadcasted_iota(jnp.int32, (HEAD_DIM, blk), 1).astype(F32)
    lamv = lamv_ref[...]
    lam_init = lam_init_ref[0]
    lam = (jnp.exp(jnp.sum(lamv[0:1] * lamv[1:2], axis=-1, keepdims=True))
           - jnp.exp(jnp.sum(lamv[2:3] * lamv[3:4], axis=-1, keepdims=True)) + lam_init)

    def query_rows(t):
        qt = q_ref[t * blk:(t + 1) * blk, :].astype(F32).T
        return qt[0:HEAD_DIM], qt[HEAD_DIM:]

    def with_lanes(q_rows, mp, aug):
        parts = [q_rows[0], aug] if mp == 0 else [aug, q_rows[1]]
        return jnp.concatenate(parts, axis=0).astype(BF16)

    def offset_lanes(sigma):
        return [(0, 3, sigma * il), (3, 4, sigma * c3[0]), (4, 5, sigma * c3[1]), (5, 6, sigma * c3[2])]

    def finish(t, acc):
        a1, a2 = acc
        ot = a1[0:V_DIM] / a1[V_DIM:V_DIM + 1] - lam * (a2[0:V_DIM] / a2[V_DIM:V_DIM + 1])
        ot = ot * lax.rsqrt(jnp.mean(ot * ot, axis=0, keepdims=True) + EPS)
        o_ref[t * blk:(t + 1) * blk, :] = (ot.T * (subg_ref[...] * (1.0 - lam_init))).astype(BF16)

    def shifted_path(dk):
        shift = shift_ref[0:1, 0:1]
        s3 = _split3(-shift)
        qis, diag_qs = [], []
        for t in range(tiles):
            qi = tiles * step + t
            cbi = c_blk * qi.astype(F32)
            q_rows = query_rows(t)
            lo3 = _split3(-cbi - shift)
            hi3 = _split3(cbi - shift)
            diag_q = []
            for mp in range(2):
                qv_ref[t, 0, mp] = with_lanes(q_rows, mp, _pick_rows(row, offset_lanes(1.0) + [
                    (6, 9, 1.0), (9, 10, lo3[0]), (10, 11, lo3[1]), (11, 12, lo3[2])]))
                qv_ref[t, 1, mp] = with_lanes(q_rows, mp, _pick_rows(row, offset_lanes(-1.0) + [
                    (6, 9, -1.0), (9, 10, hi3[0]), (10, 11, hi3[1]), (11, 12, hi3[2])]))
                qv_ref[t, 2, mp] = with_lanes(q_rows, mp, _pick_rows(row, [(9, 10, -ATTN_NULL_SCORE)]))
                diag_q.append(with_lanes(q_rows, mp, _pick_rows(
                    row, [(9, 10, s3[0]), (10, 11, s3[1]), (11, 12, s3[2])])))
            qis.append(qi)
            diag_qs.append(diag_q)

        def visit_block(t, u):
            qi = qis[t]
            if dk is None:
                j = (u - 1) + (u - 1 >= qi).astype(jnp.int32)
                return j, (j > qi).astype(jnp.int32)
            j = qi - dk + (u - 1) if u <= dk else qi + (u - dk)
            valid = (j >= 0) & (j < nblk)
            return jnp.clip(j, 0, nblk - 1), jnp.where(valid, 0 if u <= dk else 1, 2)

        def scores_stage(t, u):
            if u == 0:
                j, qts, bias = qis[t], diag_qs[t], dbias_ref[...]
            else:
                j, var = visit_block(t, u)
                qts, bias = (qv_ref[t, var, 0], qv_ref[t, var, 1]), None
            start = pl.multiple_of(j * blk, blk)
            for mp, kref in enumerate((k1_ref, k2_ref)):
                s = jnp.dot(kref[pl.ds(start, blk), :], qts[mp], preferred_element_type=F32)
                p_ref[t, u, mp] = jnp.exp2(s if bias is None else s + bias).astype(BF16)

        def weighted_values(t, u, acc):
            j = qis[t] if u == 0 else visit_block(t, u)[0]
            vt = vt_ref[:, pl.ds(pl.multiple_of(j * blk, blk), blk)]
            pv = [jnp.dot(vt, p_ref[t, u, mp], preferred_element_type=F32) for mp in range(2)]
            return pv if acc is None else [acc[mp] + pv[mp] for mp in range(2)]

        visits = nblk if dk is None else 2 * dk + 1
        accs = [None] * tiles
        for t in range(tiles):
            scores_stage(t, 0)
        for u in range(visits):
            for t in range(tiles):
                if u + 1 < visits:
                    scores_stage(t, u + 1)
                accs[t] = weighted_values(t, u, accs[t])
        for t in range(tiles):
            finish(t, accs[t])

    heads_by_radius = {}
    for h, dk in shifted_heads.items():
        heads_by_radius.setdefault(dk, []).append(h)
    for dk, heads in heads_by_radius.items():
        in_group = functools.reduce(jnp.logical_or, [head == h for h in heads])
        pl.when(shifted_ok & in_group)(functools.partial(shifted_path, dk))

    def general_tile(t):
        qi = tiles * step + t

        def key_block(n):
            return jnp.where(n == 0, qi, n - (n <= qi).astype(jnp.int32))

        def scores_stage(j, qts, slot, bias=None):
            start = pl.multiple_of(j * blk, blk)
            for mp, kref in enumerate((k1_ref, k2_ref)):
                s = jnp.dot(kref[pl.ds(start, blk), :], qts[mp], preferred_element_type=F32)
                s_ref[slot, mp] = s if bias is None else s + bias

        def softmax_stage(j, slot, pslot):
            const = -(c_blk * jnp.abs(qi - j).astype(F32))
            for mp in range(2):
                s = s_ref[slot, mp]
                m_old = m_ref[mp]
                m_new = jnp.maximum(m_old, jnp.max(s, axis=0, keepdims=True) + const)
                p_ref[t, pslot, mp] = jnp.exp2(s - (m_new - const)).astype(BF16)
                alpha_ref[pslot, mp] = jnp.exp2(m_old - m_new)
                m_ref[mp] = m_new

        def values_stage(j, pslot):
            start = pl.multiple_of(j * blk, blk)
            vt = vt_ref[:, pl.ds(start, blk)]
            for mp in range(2):
                pv = jnp.dot(vt, p_ref[t, pslot, mp], preferred_element_type=F32)
                acc_ref[mp] = acc_ref[mp] * alpha_ref[pslot, mp] + pv

        q_rows = query_rows(t)
        for mp in range(2):
            qv_ref[t, 0, mp] = with_lanes(q_rows, mp, _pick_rows(row, offset_lanes(1.0)))
            qv_ref[t, 1, mp] = with_lanes(q_rows, mp, _pick_rows(row, offset_lanes(-1.0)))
        m_ref[...] = jnp.full(m_ref.shape, -jnp.inf, F32)
        acc_ref[...] = jnp.zeros(acc_ref.shape, F32)
        zeros = jnp.zeros((HEAD_DIM, blk), F32)
        scores_stage(qi, (with_lanes(q_rows, 0, zeros), with_lanes(q_rows, 1, zeros)), 0, dbias_ref[...])
        j1 = key_block(1)
        side = (j1 > qi).astype(jnp.int32)
        scores_stage(j1, (qv_ref[t, side, 0], qv_ref[t, side, 1]), 1)
        for n in range(nblk):
            if n > 0:
                values_stage(key_block(n - 1), (n - 1) % 2)
            softmax_stage(key_block(n), n % ATTN_SLOTS, n % 2)
            if n + 2 < nblk:
                ja = key_block(n + 2)
                side = (ja > qi).astype(jnp.int32)
                scores_stage(ja, (qv_ref[t, side, 0], qv_ref[t, side, 1]), (n + 2) % ATTN_SLOTS)
        values_stage(key_block(nblk - 1), (nblk - 1) % 2)
        finish(t, (acc_ref[0], acc_ref[1]))

    @pl.when(jnp.logical_not(shifted_ok))
    def _():
        for t in range(tiles):
            general_tile(t)


def _attention(q, k, v, lamv, sub_g, q_g, k_g, slopes, lam_init, batch, seq):
    smem = pl.BlockSpec(memory_space=pltpu.SMEM)
    blk = ATTN_BLOCK
    nq = seq // blk
    tiles = ATTN_TILES
    steps = nq // tiles
    k3 = k.reshape(batch, seq, D_MODEL)
    v3 = v.reshape(batch, seq, D_MODEL)
    shifted_heads = {}
    for h in range(N_HEADS):
        slope = 2.0 ** (-8.0 * (h + 1) / N_HEADS)
        dk = math.ceil((ATTN_ZERO_LOG2 / (slope * LOG2E) - 1.0) / blk)
        shifted_heads[h] = dk if 2 * dk + 1 < nq else None
    return pl.pallas_call(
        functools.partial(_attn_kernel, shifted_heads=shifted_heads),
        grid=(batch, N_HEADS, steps),
        in_specs=[
            smem, smem,
            pl.BlockSpec((tiles * blk, V_DIM), lambda b, h, i: (b * steps + i, h)),
            pl.BlockSpec((None, seq, V_DIM), lambda b, h, i: (b, 0, h)),
            pl.BlockSpec((None, seq, V_DIM), lambda b, h, i: (b, 0, h)),
            pl.BlockSpec((4, HEAD_DIM), lambda b, h, i: (0, 0)),
            pl.BlockSpec((1, V_DIM), lambda b, h, i: (0, 0)),
            pl.BlockSpec((1, HEAD_DIM), lambda b, h, i: (0, 0)),
            pl.BlockSpec((1, HEAD_DIM), lambda b, h, i: (0, 0)),
        ],
        out_specs=pl.BlockSpec((tiles * blk, V_DIM), lambda b, h, i: (b * steps + i, h)),
        out_shape=jax.ShapeDtypeStruct((batch * seq, D_MODEL), BF16),
        scratch_shapes=[
            pltpu.VMEM((seq, V_DIM), BF16),
            pltpu.VMEM((seq, V_DIM), BF16),
            pltpu.VMEM((ATTN_VT_ROWS, seq), BF16),
            pltpu.VMEM((blk, blk), F32),
            pltpu.VMEM((8, 128), F32),
            pltpu.SMEM((1,), jnp.int32),
            pltpu.VMEM((tiles, 3, 2, V_DIM, blk), BF16),
            pltpu.VMEM((ATTN_SLOTS, 2, blk, blk), F32),
            pltpu.VMEM((tiles, nq, 2, blk, blk), BF16),
            pltpu.VMEM((2, 2, 1, blk), F32),
            pltpu.VMEM((2, 1, blk), F32),
            pltpu.VMEM((2, ATTN_VT_ROWS, blk), F32),
        ],
        compiler_params=pltpu.CompilerParams(
            dimension_semantics=("arbitrary", "arbitrary", "arbitrary"),
            vmem_limit_bytes=V7X_VMEM_LIMIT),
        name="diff_attn",
    )(slopes, lam_init, q, k3, v3, lamv, sub_g, q_g, k_g)


def _merge_kernel(x_ref, oa_ref, u_ref, sv_ref, ga_ref, gb_ref, sgw_ref, sgb_ref,
                  wa_ref, wb_ref, wo_ref, o_ref, ob_ref):
    for c in range(MERGE_ROWS // CHUNK):
        rows = slice(c * CHUNK, (c + 1) * CHUNK)
        for g in range(N_GROUPS):
            cols = slice(g * GROUP_DIM, (g + 1) * GROUP_DIM)
            mixed = jnp.dot(sgw_ref[g], sv_ref[rows, cols], preferred_element_type=F32)
            ob_ref[rows, cols] = (u_ref[rows, cols].astype(F32) * (mixed + sgb_ref[:, cols])).astype(BF16)
    ya = jnp.dot(oa_ref[...], wa_ref[...], preferred_element_type=F32)
    yb = jnp.dot(ob_ref[...], wb_ref[...], preferred_element_type=F32)
    merged = (ga_ref[...].astype(F32) * ya + gb_ref[...].astype(F32) * yb).astype(BF16)
    o_ref[...] = x_ref[...] + jnp.dot(merged, wo_ref[...], preferred_element_type=F32)


def _merge(x2, oa, u, sv, ga, gb, sgw, sgb_full, wa, wb, wo):
    t = x2.shape[0]
    row = lambda i: (i, 0)
    fixed2 = lambda i: (0, 0)
    tile = pl.BlockSpec((MERGE_ROWS, D_MODEL), row)
    wspec = pl.BlockSpec((D_MODEL, D_MODEL), fixed2, pipeline_mode=pl.Buffered(1))
    return pl.pallas_call(
        _merge_kernel,
        grid=(t // MERGE_ROWS,),
        in_specs=[
            tile, tile, tile, tile, tile, tile,
            pl.BlockSpec((N_GROUPS, CHUNK, CHUNK), lambda i: (0, 0, 0)),
            pl.BlockSpec((CHUNK, D_MODEL), fixed2),
            wspec, wspec, wspec,
        ],
        out_specs=tile,
        out_shape=jax.ShapeDtypeStruct((t, D_MODEL), F32),
        scratch_shapes=[pltpu.VMEM((MERGE_ROWS, D_MODEL), BF16)],
        compiler_params=pltpu.CompilerParams(
            dimension_semantics=("arbitrary",), vmem_limit_bytes=V7X_VMEM_LIMIT),
        name="sgu_merge",
    )(x2, oa, u, sv, ga, gb, sgw, sgb_full, wa, wb, wo)


def _ffn_kernel(x_ref, prev_ref, next_ref, g_ref, wup_ref, cw_ref, cb_ref, wdn_ref,
                o_ref, h_ref, act_ref, *, blocks_per_seq):
    i = pl.program_id(0)
    tm, halo = FFN_ROWS, FFN_HALO
    g = g_ref[...]
    x = x_ref[...]
    keep_prev = (i % blocks_per_seq != 0).astype(F32)
    keep_next = ((i + 1) % blocks_per_seq != 0).astype(F32)
    h_ref[0:halo, :] = (_rms_rows(prev_ref[...], g) * keep_prev).astype(BF16)
    h_ref[halo:halo + tm, :] = _rms_rows(x, g).astype(BF16)
    h_ref[halo + tm:, :] = (_rms_rows(next_ref[...], g) * keep_next).astype(BF16)
    h = h_ref[...]
    ext = tm + 2 * halo

    def conv(cols):
        up = jnp.dot(h, wup_ref[:, cols], preferred_element_type=F32)
        y = (pltpu.roll(up, 1, 0) * cw_ref[0:1, cols] + up * cw_ref[1:2, cols]
             + pltpu.roll(up, ext - 1, 0) * cw_ref[2:3, cols])
        return y[halo:halo + tm] + cb_ref[:, cols]

    for lo in range(0, D_FF, FFN_COLS):
        hi = min(lo + FFN_COLS, D_FF)
        gate = conv(slice(lo, hi))
        val = conv(slice(D_FF + lo, D_FF + hi))
        act_ref[:, lo:hi] = (gate * _sigmoid(gate) * val).astype(BF16)
    o_ref[...] = x + jnp.dot(act_ref[...], wdn_ref[...], preferred_element_type=F32)


def _ffn(x2, ln_g, wup, conv_w, conv_b, wdn, seq):
    t = x2.shape[0]
    tm, halo = FFN_ROWS, FFN_HALO
    r = tm // halo
    last = t // halo - 1
    fixed = lambda i: (0, 0)
    return pl.pallas_call(
        functools.partial(_ffn_kernel, blocks_per_seq=seq // tm),
        grid=(t // tm,),
        in_specs=[
            pl.BlockSpec((tm, D_MODEL), lambda i: (i, 0)),
            pl.BlockSpec((halo, D_MODEL), lambda i: (jnp.maximum(i * r - 1, 0), 0)),
            pl.BlockSpec((halo, D_MODEL), lambda i: (jnp.minimum((i + 1) * r, last), 0)),
            pl.BlockSpec((1, D_MODEL), fixed),
            pl.BlockSpec((D_MODEL, 2 * D_FF), fixed, pipeline_mode=pl.Buffered(1)),
            pl.BlockSpec((3, 2 * D_FF), fixed),
            pl.BlockSpec((1, 2 * D_FF), fixed),
            pl.BlockSpec((D_FF, D_MODEL), fixed, pipeline_mode=pl.Buffered(1)),
        ],
        out_specs=pl.BlockSpec((tm, D_MODEL), lambda i: (i, 0)),
        out_shape=jax.ShapeDtypeStruct((t, D_MODEL), F32),
        scratch_shapes=[pltpu.VMEM((tm + 2 * halo, D_MODEL), BF16),
                        pltpu.VMEM((tm, D_FF), BF16)],
        compiler_params=pltpu.CompilerParams(
            dimension_semantics=("arbitrary",), vmem_limit_bytes=V7X_VMEM_LIMIT),
        name="conv_ffn",
    )(x2, x2, x2, ln_g, wup, conv_w, conv_b, wdn)


def kernel(x, ln_mix_g, w_in, q_norm_g, k_norm_g, lambda_q1, lambda_k1, lambda_q2, lambda_k2,
           subln_g, sg_norm_g, sg_w, sg_b, w_branch_a, w_branch_b, w_out, ln_ffn_g, w_up,
           conv_w, conv_b, w_down):
    batch, seq, d = x.shape
    depth = w_in.shape[0]
    slopes = 2.0 ** (-8.0 * jnp.arange(1, N_HEADS + 1, dtype=F32) / N_HEADS)
    x2 = x.reshape(batch * seq, d)
    for l in range(depth):
        lam_init = jnp.full((1,), 0.8 - 0.6 * math.exp(-0.3 * l), F32)
        q, k, v, u, sv, ga, gb = _projection(
            x2, ln_mix_g[l][None], w_in[l].astype(BF16),
            jnp.tile(q_norm_g[l], 2 * N_HEADS)[None], jnp.tile(k_norm_g[l], 2 * N_HEADS)[None],
            sg_norm_g[l][None])
        lamv = jnp.stack([lambda_q1[l], lambda_k1[l], lambda_q2[l], lambda_k2[l]])
        oa = _attention(q, k, v, lamv, subln_g[l][None], q_norm_g[l][None], k_norm_g[l][None],
                        slopes, lam_init, batch, seq)
        sgb_full = jnp.repeat(sg_b[l].T, GROUP_DIM, axis=1)
        x2 = _merge(x2, oa, u, sv, ga, gb, sg_w[l].astype(BF16), sgb_full,
                    w_branch_a[l].astype(BF16), w_branch_b[l].astype(BF16), w_out[l].astype(BF16))
        x2 = _ffn(x2, ln_ffn_g[l][None], w_up[l].astype(BF16), conv_w[l], conv_b[l][None],
                  w_down[l].astype(BF16), seq)
    return x2.reshape(batch, seq, d)
```

```python
import functools
import math

import jax
import jax.numpy as jnp
from jax import lax
from jax.experimental import pallas as pl
from jax.experimental.pallas import tpu as pltpu

F32 = jnp.float32
BF16 = jnp.bfloat16

D_MODEL = 1024
N_HEADS = 8
HEAD_DIM = 64
V_DIM = 2 * HEAD_DIM
CHUNK = 128
N_GROUPS = 8
GROUP_DIM = D_MODEL // N_GROUPS
D_FF = 2816
EPS = 1e-6
N_SECTIONS = 7

V7X_MXU_DIM = 256
V7X_VMEM_LIMIT = 56 * 1024 * 1024

PROJ_ROWS = 512
PROJ_COLS = 512
ATTN_BLOCK = 256
ATTN_VT_ROWS = V_DIM + 16
ATTN_SLOTS = 4
ATTN_TILES = 4
LOG2E = math.log2(math.e)
ATTN_MAX_SHIFT = 30.0
ATTN_ZERO_LOG2 = 160.0
ATTN_NULL_SCORE = 30000.0
MERGE_ROWS = 512
FFN_ROWS = 512
FFN_HALO = 16
FFN_COLS = 512


def _rms_rows(x, g):
    return x * lax.rsqrt(jnp.mean(x * x, axis=-1, keepdims=True) + EPS) * g


def _gelu_tanh(x):
    c = math.sqrt(2.0 / math.pi)
    return 0.5 * x * (1.0 + jnp.tanh(c * (x + 0.044715 * (x * x * x))))


def _sigmoid(x):
    return 1.0 / (1.0 + jnp.exp(-x))


def _group_mean_matrix(group):
    r = lax.broadcasted_iota(jnp.int32, (V7X_MXU_DIM, V7X_MXU_DIM), 0) // group
    c = lax.broadcasted_iota(jnp.int32, (V7X_MXU_DIM, V7X_MXU_DIM), 1) // group
    return jnp.where(r == c, 1.0 / group, 0.0).astype(BF16)


def _group_rms_scale(a, mean_mat):
    sq = (a * a).astype(BF16)
    w = mean_mat.shape[0]
    ms = jnp.concatenate([jnp.dot(sq[:, lo:lo + w], mean_mat, preferred_element_type=F32)
                          for lo in range(0, a.shape[1], w)], axis=1)
    return lax.rsqrt(ms + EPS)


def _proj_kernel(x_ref, g_ref, w_ref, qg_ref, kg_ref, sgg_ref,
                 q_ref, k_ref, v_ref, u_ref, sv_ref, ga_ref, gb_ref):
    h = _rms_rows(x_ref[...], g_ref[...]).astype(BF16)
    mean64 = _group_mean_matrix(HEAD_DIM)
    mean128 = _group_mean_matrix(GROUP_DIM)
    w = PROJ_COLS

    def section(s, c):
        lo = s * D_MODEL + c * w
        return jnp.dot(h, w_ref[:, lo:lo + w], preferred_element_type=F32)

    for c in range(D_MODEL // w):
        cols = slice(c * w, (c + 1) * w)
        a = section(0, c)
        q_ref[:, cols] = (a * _group_rms_scale(a, mean64) * (qg_ref[:, cols] * (HEAD_DIM ** -0.5 * LOG2E))).astype(BF16)
        a = section(1, c)
        k_ref[:, cols] = (a * _group_rms_scale(a, mean64) * kg_ref[:, cols]).astype(BF16)
        v_ref[:, cols] = section(2, c).astype(BF16)
        u_ref[:, cols] = _gelu_tanh(section(3, c)).astype(BF16)
        a = _gelu_tanh(section(4, c))
        sv_ref[:, cols] = (a * _group_rms_scale(a, mean128) * sgg_ref[:, cols]).astype(BF16)
        ga_ref[:, cols] = _sigmoid(section(5, c)).astype(BF16)
        gb_ref[:, cols] = _sigmoid(section(6, c)).astype(BF16)


def _projection(x2, ln_g, w_in, q_g, k_g, sg_g):
    t = x2.shape[0]
    row = lambda i: (i, 0)
    fixed = lambda i: (0, 0)
    vec = pl.BlockSpec((1, D_MODEL), fixed)
    out = jax.ShapeDtypeStruct((t, D_MODEL), BF16)
    return pl.pallas_call(
        _proj_kernel,
        grid=(t // PROJ_ROWS,),
        in_specs=[
            pl.BlockSpec((PROJ_ROWS, D_MODEL), row),
            vec,
            pl.BlockSpec((D_MODEL, N_SECTIONS * D_MODEL), fixed, pipeline_mode=pl.Buffered(1)),
            vec, vec, vec,
        ],
        out_specs=[pl.BlockSpec((PROJ_ROWS, D_MODEL), row)] * N_SECTIONS,
        out_shape=[out] * N_SECTIONS,
        compiler_params=pltpu.CompilerParams(
            dimension_semantics=("arbitrary",), vmem_limit_bytes=V7X_VMEM_LIMIT),
        name="proj",
    )(x2, ln_g, w_in, q_g, k_g, sg_g)


def _split3(c):
    hi = c.astype(BF16)
    mid = (c - hi.astype(F32)).astype(BF16)
    lo = (c - hi.astype(F32) - mid.astype(F32)).astype(BF16)
    return hi.astype(F32), mid.astype(F32), lo.astype(F32)


def _pick_rows(row, entries, default=0.0):
    out = default
    for lo, hi, val in reversed(entries):
        out = jnp.where((row >= lo) & (row < hi), val, out)
    return out


def _attn_kernel(slopes_ref, lam_init_ref, q_ref, k_ref, v_ref, lamv_ref, subg_ref, qg_ref, kg_ref, o_ref,
                 k1_ref, k2_ref, vt_ref, dbias_ref, shift_ref, flag_ref, qv_ref, s_ref, p_ref,
                 alpha_ref, m_ref, acc_ref, *, shifted_heads):
    head = pl.program_id(1)
    step = pl.program_id(2)
    blk = ATTN_BLOCK
    seq = k_ref.shape[0]
    nblk = seq // blk
    c = jnp.full((1, 1), slopes_ref[head] * LOG2E, F32)
    c3 = _split3(c)
    c_blk = c * float(blk)

    @pl.when(step == 0)
    def _():
        lane_b = lax.broadcasted_iota(jnp.int32, (blk, V_DIM), 1)
        off = lax.broadcasted_iota(jnp.int32, (blk, V_DIM), 0).astype(F32)
        for mp, ref in enumerate((k1_ref, k2_ref)):
            a = lane_b - (HEAD_DIM if mp == 0 else 0)
            own_b = (lane_b < HEAD_DIM) if mp == 0 else (lane_b >= HEAD_DIM)
            fixed = _pick_rows(a, [(0, 1, -c3[0]), (1, 2, -c3[1]), (2, 3, -c3[2]), (3, 6, off), (9, 12, 1.0)])
            for jb in range(nblk):
                cb3 = _split3(c_blk * float(jb))
                aug = _pick_rows(a, [(6, 7, cb3[0]), (7, 8, cb3[1]), (8, 9, cb3[2])], fixed)
                rows = slice(jb * blk, (jb + 1) * blk)
                ref[rows, :] = jnp.where(own_b, k_ref[rows, :].astype(F32), aug).astype(BF16)
        gmax = (jnp.max(jnp.abs(qg_ref[...]), axis=1, keepdims=True)
                * jnp.max(jnp.abs(kg_ref[...]), axis=1, keepdims=True))
        bound = 1.01 * (HEAD_DIM * HEAD_DIM ** -0.5 * LOG2E) * gmax + 0.01
        shift_ref[...] = jnp.broadcast_to(bound, shift_ref.shape)
        flag_ref[0] = (bound[0, 0] <= ATTN_MAX_SHIFT).astype(jnp.int32)
        cols = 512
        for s0 in range(0, seq, cols):
            vt_ref[0:V_DIM, s0:s0 + cols] = v_ref[s0:s0 + cols, :].astype(F32).T.astype(BF16)
        r = lax.broadcasted_iota(jnp.int32, (ATTN_VT_ROWS - V_DIM, seq), 0)
        vt_ref[V_DIM:, :] = jnp.where(r == 0, 1.0, 0.0).astype(BF16)
        d = lax.broadcasted_iota(jnp.int32, (blk, blk), 0) - lax.broadcasted_iota(jnp.int32, (blk, blk), 1)
        dbias_ref[...] = -(c * jnp.abs(d).astype(F32))

    shifted_ok = flag_ref[0] == 1
    tiles = ATTN_TILES
    row = lax.broadcasted_iota(jnp.int32, (HEAD_DIM, blk), 0)
    il = lax.broadcasted_iota(jnp.int32, (HEAD_DIM, blk), 1).astype(F32)
    lamv = lamv_ref[...]
    lam_init = lam_init_ref[0]
    lam = (jnp.exp(jnp.sum(lamv[0:1] * lamv[1:2], axis=-1, keepdims=True))
           - jnp.exp(jnp.sum(lamv[2:3] * lamv[3:4], axis=-1, keepdims=True)) + lam_init)

    def query_rows(t):
        qt = q_ref[t * blk:(t + 1) * blk, :].astype(F32).T
        return qt[0:HEAD_DIM], qt[HEAD_DIM:]

    def with_lanes(q_rows, mp, aug):
        parts = [q_rows[0], aug] if mp == 0 else [aug, q_rows[1]]
        return jnp.concatenate(parts, axis=0).astype(BF16)

    def offset_lanes(sigma):
        return [(0, 3, sigma * il), (3, 4, sigma * c3[0]), (4, 5, sigma * c3[1]), (5, 6, sigma * c3[2])]

    def finish(t, acc):
        a1, a2 = acc
        ot = a1[0:V_DIM] / a1[V_DIM:V_DIM + 1] - lam * (a2[0:V_DIM] / a2[V_DIM:V_DIM + 1])
        ot = ot * lax.rsqrt(jnp.mean(ot * ot, axis=0, keepdims=True) + EPS)
        o_ref[t * blk:(t + 1) * blk, :] = (ot.T * (subg_ref[...] * (1.0 - lam_init))).astype(BF16)

    def shifted_path(dk):
        shift = shift_ref[0:1, 0:1]
        s3 = _split3(-shift)
        qis, diag_qs = [], []
        for t in range(tiles):
            qi = tiles * step + t
            cbi = c_blk * qi.astype(F32)
            q_rows = query_rows(t)
            lo3 = _split3(-cbi - shift)
            hi3 = _split3(cbi - shift)
            diag_q = []
            for mp in range(2):
                qv_ref[t, 0, mp] = with_lanes(q_rows, mp, _pick_rows(row, offset_lanes(1.0) + [
                    (6, 9, 1.0), (9, 10, lo3[0]), (10, 11, lo3[1]), (11, 12, lo3[2])]))
                qv_ref[t, 1, mp] = with_lanes(q_rows, mp, _pick_rows(row, offset_lanes(-1.0) + [
                    (6, 9, -1.0), (9, 10, hi3[0]), (10, 11, hi3[1]), (11, 12, hi3[2])]))
                qv_ref[t, 2, mp] = with_lanes(q_rows, mp, _pick_rows(row, [(9, 10, -ATTN_NULL_SCORE)]))
                diag_q.append(with_lanes(q_rows, mp, _pick_rows(
                    row, [(9, 10, s3[0]), (10, 11, s3[1]), (11, 12, s3[2])])))
            qis.append(qi)
            diag_qs.append(diag_q)

        def visit_block(t, u):
            qi = qis[t]
            if dk is None:
                j = (u - 1) + (u - 1 >= qi).astype(jnp.int32)
                return j, (j > qi).astype(jnp.int32)
            j = qi - dk + (u - 1) if u <= dk else qi + (u - dk)
            valid = (j >= 0) & (j < nblk)
            return jnp.clip(j, 0, nblk - 1), jnp.where(valid, 0 if u <= dk else 1, 2)

        def scores_stage(t, u):
            if u == 0:
                j, qts, bias = qis[t], diag_qs[t], dbias_ref[...]
            else:
                j, var = visit_block(t, u)
                qts, bias = (qv_ref[t, var, 0], qv_ref[t, var, 1]), None
            start = pl.multiple_of(j * blk, blk)
            for mp, kref in enumerate((k1_ref, k2_ref)):
                s = jnp.dot(kref[pl.ds(start, blk), :], qts[mp], preferred_element_type=F32)
                p_ref[t, u, mp] = jnp.exp2(s if bias is None else s + bias).astype(BF16)

        def weighted_values(t, u, acc):
            j = qis[t] if u == 0 else visit_block(t, u)[0]
            vt = vt_ref[:, pl.ds(pl.multiple_of(j * blk, blk), blk)]
            pv = [jnp.dot(vt, p_ref[t, u, mp], preferred_element_type=F32) for mp in range(2)]
            return pv if acc is None else [acc[mp] + pv[mp] for mp in range(2)]

        visits = nblk if dk is None else 2 * dk + 1
        accs = [None] * tiles
        for t in range(tiles):
            scores_stage(t, 0)
        for u in range(visits):
            for t in range(tiles):
                if u + 1 < visits:
                    scores_stage(t, u + 1)
                accs[t] = weighted_values(t, u, accs[t])
        for t in range(tiles):
            finish(t, accs[t])

    heads_by_radius = {}
    for h, dk in shifted_heads.items():
        heads_by_radius.setdefault(dk, []).append(h)
    for dk, heads in heads_by_radius.items():
        in_group = functools.reduce(jnp.logical_or, [head == h for h in heads])
        pl.when(shifted_ok & in_group)(functools.partial(shifted_path, dk))

    def general_tile(t):
        qi = tiles * step + t

        def key_block(n):
            return jnp.where(n == 0, qi, n - (n <= qi).astype(jnp.int32))

        def scores_stage(j, qts, slot, bias=None):
            start = pl.multiple_of(j * blk, blk)
            for mp, kref in enumerate((k1_ref, k2_ref)):
                s = jnp.dot(kref[pl.ds(start, blk), :], qts[mp], preferred_element_type=F32)
                s_ref[slot, mp] = s if bias is None else s + bias

        def softmax_stage(j, slot, pslot):
            const = -(c_blk * jnp.abs(qi - j).astype(F32))
            for mp in range(2):
                s = s_ref[slot, mp]
                m_old = m_ref[mp]
                m_new = jnp.maximum(m_old, jnp.max(s, axis=0, keepdims=True) + const)
                p_ref[t, pslot, mp] = jnp.exp2(s - (m_new - const)).astype(BF16)
                alpha_ref[pslot, mp] = jnp.exp2(m_old - m_new)
                m_ref[mp] = m_new

        def values_stage(j, pslot):
            start = pl.multiple_of(j * blk, blk)
            vt = vt_ref[:, pl.ds(start, blk)]
            for mp in range(2):
                pv = jnp.dot(vt, p_ref[t, pslot, mp], preferred_element_type=F32)
                acc_ref[mp] = acc_ref[mp] * alpha_ref[pslot, mp] + pv

        q_rows = query_rows(t)
        for mp in range(2):
            qv_ref[t, 0, mp] = with_lanes(q_rows, mp, _pick_rows(row, offset_lanes(1.0)))
            qv_ref[t, 1, mp] = with_lanes(q_rows, mp, _pick_rows(row, offset_lanes(-1.0)))
        m_ref[...] = jnp.full(m_ref.shape, -jnp.inf, F32)
        acc_ref[...] = jnp.zeros(acc_ref.shape, F32)
        zeros = jnp.zeros((HEAD_DIM, blk), F32)
        scores_stage(qi, (with_lanes(q_rows, 0, zeros), with_lanes(q_rows, 1, zeros)), 0, dbias_ref[...])
        j1 = key_block(1)
        side = (j1 > qi).astype(jnp.int32)
        scores_stage(j1, (qv_ref[t, side, 0], qv_ref[t, side, 1]), 1)
        for n in range(nblk):
            if n > 0:
                values_stage(key_block(n - 1), (n - 1) % 2)
            softmax_stage(key_block(n), n % ATTN_SLOTS, n % 2)
            if n + 2 < nblk:
                ja = key_block(n + 2)
                side = (ja > qi).astype(jnp.int32)
                scores_stage(ja, (qv_ref[t, side, 0], qv_ref[t, side, 1]), (n + 2) % ATTN_SLOTS)
        values_stage(key_block(nblk - 1), (nblk - 1) % 2)
        finish(t, (acc_ref[0], acc_ref[1]))

    @pl.when(jnp.logical_not(shifted_ok))
    def _():
        for t in range(tiles):
            general_tile(t)


def _attention(q, k, v, lamv, sub_g, q_g, k_g, slopes, lam_init, batch, seq):
    smem = pl.BlockSpec(memory_space=pltpu.SMEM)
    blk = ATTN_BLOCK
    nq = seq // blk
    tiles = ATTN_TILES
    steps = nq // tiles
    k3 = k.reshape(batch, seq, D_MODEL)
    v3 = v.reshape(batch, seq, D_MODEL)
    shifted_heads = {}
    for h in range(N_HEADS):
        slope = 2.0 ** (-8.0 * (h + 1) / N_HEADS)
        dk = math.ceil((ATTN_ZERO_LOG2 / (slope * LOG2E) - 1.0) / blk)
        shifted_heads[h] = dk if 2 * dk + 1 < nq else None
    return pl.pallas_call(
        functools.partial(_attn_kernel, shifted_heads=shifted_heads),
        grid=(batch, N_HEADS, steps),
        in_specs=[
            smem, smem,
            pl.BlockSpec((tiles * blk, V_DIM), lambda b, h, i: (b * steps + i, h)),
            pl.BlockSpec((None, seq, V_DIM), lambda b, h, i: (b, 0, h)),
            pl.BlockSpec((None, seq, V_DIM), lambda b, h, i: (b, 0, h)),
            pl.BlockSpec((4, HEAD_DIM), lambda b, h, i: (0, 0)),
            pl.BlockSpec((1, V_DIM), lambda b, h, i: (0, 0)),
            pl.BlockSpec((1, HEAD_DIM), lambda b, h, i: (0, 0)),
            pl.BlockSpec((1, HEAD_DIM), lambda b, h, i: (0, 0)),
        ],
        out_specs=pl.BlockSpec((tiles * blk, V_DIM), lambda b, h, i: (b * steps + i, h)),
        out_shape=jax.ShapeDtypeStruct((batch * seq, D_MODEL), BF16),
        scratch_shapes=[
            pltpu.VMEM((seq, V_DIM), BF16),
            pltpu.VMEM((seq, V_DIM), BF16),
            pltpu.VMEM((ATTN_VT_ROWS, seq), BF16),
            pltpu.VMEM((blk, blk), F32),
            pltpu.VMEM((8, 128), F32),
            pltpu.SMEM((1,), jnp.int32),
            pltpu.VMEM((tiles, 3, 2, V_DIM, blk), BF16),
            pltpu.VMEM((ATTN_SLOTS, 2, blk, blk), F32),
            pltpu.VMEM((tiles, nq, 2, blk, blk), BF16),
            pltpu.VMEM((2, 2, 1, blk), F32),
            pltpu.VMEM((2, 1, blk), F32),
            pltpu.VMEM((2, ATTN_VT_ROWS, blk), F32),
        ],
        compiler_params=pltpu.CompilerParams(
            dimension_semantics=("arbitrary", "arbitrary", "arbitrary"),
            vmem_limit_bytes=V7X_VMEM_LIMIT),
        name="diff_attn",
    )(slopes, lam_init, q, k3, v3, lamv, sub_g, q_g, k_g)


def _merge_kernel(x_ref, oa_ref, u_ref, sv_ref, ga_ref, gb_ref, sgw_ref, sgb_ref,
                  wa_ref, wb_ref, wo_ref, o_ref, ob_ref):
    for c in range(MERGE_ROWS // CHUNK):
        rows = slice(c * CHUNK, (c + 1) * CHUNK)
        for g in range(N_GROUPS):
            cols = slice(g * GROUP_DIM, (g + 1) * GROUP_DIM)
            mixed = jnp.dot(sgw_ref[g], sv_ref[rows, cols], preferred_element_type=F32)
            ob_ref[rows, cols] = (u_ref[rows, cols].astype(F32) * (mixed + sgb_ref[:, cols])).astype(BF16)
    ya = jnp.dot(oa_ref[...], wa_ref[...], preferred_element_type=F32)
    yb = jnp.dot(ob_ref[...], wb_ref[...], preferred_element_type=F32)
    merged = (ga_ref[...].astype(F32) * ya + gb_ref[...].astype(F32) * yb).astype(BF16)
    o_ref[...] = x_ref[...] + jnp.dot(merged, wo_ref[...], preferred_element_type=F32)


def _merge(x2, oa, u, sv, ga, gb, sgw, sgb_full, wa, wb, wo):
    t = x2.shape[0]
    row = lambda i: (i, 0)
    fixed2 = lambda i: (0, 0)
    tile = pl.BlockSpec((MERGE_ROWS, D_MODEL), row)
    wspec = pl.BlockSpec((D_MODEL, D_MODEL), fixed2, pipeline_mode=pl.Buffered(1))
    return pl.pallas_call(
        _merge_kernel,
        grid=(t // MERGE_ROWS,),
        in_specs=[
            tile, tile, tile, tile, tile, tile,
            pl.BlockSpec((N_GROUPS, CHUNK, CHUNK), lambda i: (0, 0, 0)),
            pl.BlockSpec((CHUNK, D_MODEL), fixed2),
            wspec, wspec, wspec,
        ],
        out_specs=tile,
        out_shape=jax.ShapeDtypeStruct((t, D_MODEL), F32),
        scratch_shapes=[pltpu.VMEM((MERGE_ROWS, D_MODEL), BF16)],
        compiler_params=pltpu.CompilerParams(
            dimension_semantics=("arbitrary",), vmem_limit_bytes=V7X_VMEM_LIMIT),
        name="sgu_merge",
    )(x2, oa, u, sv, ga, gb, sgw, sgb_full, wa, wb, wo)


def _ffn_kernel(x_ref, prev_ref, next_ref, g_ref, wup_ref, cw_ref, cb_ref, wdn_ref,
                o_ref, h_ref, act_ref, *, blocks_per_seq):
    i = pl.program_id(0)
    tm, halo = FFN_ROWS, FFN_HALO
    g = g_ref[...]
    x = x_ref[...]
    keep_prev = (i % blocks_per_seq != 0).astype(F32)
    keep_next = ((i + 1) % blocks_per_seq != 0).astype(F32)
    h_ref[0:halo, :] = (_rms_rows(prev_ref[...], g) * keep_prev).astype(BF16)
    h_ref[halo:halo + tm, :] = _rms_rows(x, g).astype(BF16)
    h_ref[halo + tm:, :] = (_rms_rows(next_ref[...], g) * keep_next).astype(BF16)
    h = h_ref[...]
    ext = tm + 2 * halo

    def conv(cols):
        up = jnp.dot(h, wup_ref[:, cols], preferred_element_type=F32)
        y = (pltpu.roll(up, 1, 0) * cw_ref[0:1, cols] + up * cw_ref[1:2, cols]
             + pltpu.roll(up, ext - 1, 0) * cw_ref[2:3, cols])
        return y[halo:halo + tm] + cb_ref[:, cols]

    for lo in range(0, D_FF, FFN_COLS):
        hi = min(lo + FFN_COLS, D_FF)
        gate = conv(slice(lo, hi))
        val = conv(slice(D_FF + lo, D_FF + hi))
        act_ref[:, lo:hi] = (gate * _sigmoid(gate) * val).astype(BF16)
    o_ref[...] = x + jnp.dot(act_ref[...], wdn_ref[...], preferred_element_type=F32)


def _ffn(x2, ln_g, wup, conv_w, conv_b, wdn, seq):
    t = x2.shape[0]
    tm, halo = FFN_ROWS, FFN_HALO
    r = tm // halo
    last = t // halo - 1
    fixed = lambda i: (0, 0)
    return pl.pallas_call(
        functools.partial(_ffn_kernel, blocks_per_seq=seq // tm),
        grid=(t // tm,),
        in_specs=[
            pl.BlockSpec((tm, D_MODEL), lambda i: (i, 0)),
            pl.BlockSpec((halo, D_MODEL), lambda i: (jnp.maximum(i * r - 1, 0), 0)),
            pl.BlockSpec((halo, D_MODEL), lambda i: (jnp.minimum((i + 1) * r, last), 0)),
            pl.BlockSpec((1, D_MODEL), fixed),
            pl.BlockSpec((D_MODEL, 2 * D_FF), fixed, pipeline_mode=pl.Buffered(1)),
            pl.BlockSpec((3, 2 * D_FF), fixed),
            pl.BlockSpec((1, 2 * D_FF), fixed),
            pl.BlockSpec((D_FF, D_MODEL), fixed, pipeline_mode=pl.Buffered(1)),
        ],
        out_specs=pl.BlockSpec((tm, D_MODEL), lambda i: (i, 0)),
        out_shape=jax.ShapeDtypeStruct((t, D_MODEL), F32),
        scratch_shapes=[pltpu.VMEM((tm + 2 * halo, D_MODEL), BF16),
                        pltpu.VMEM((tm, D_FF), BF16)],
        compiler_params=pltpu.CompilerParams(
            dimension_semantics=("arbitrary",), vmem_limit_bytes=V7X_VMEM_LIMIT),
        name="conv_ffn",
    )(x2, x2, x2, ln_g, wup, conv_w, conv_b, wdn)


def kernel(x, ln_mix_g, w_in, q_norm_g, k_norm_g, lambda_q1, lambda_k1, lambda_q2, lambda_k2,
           subln_g, sg_norm_g, sg_w, sg_b, w_branch_a, w_branch_b, w_out, ln_ffn_g, w_up,
           conv_w, conv_b, w_down):
    batch, seq, d = x.shape
    depth = w_in.shape[0]
    slopes = 2.0 ** (-8.0 * jnp.arange(1, N_HEADS + 1, dtype=F32) / N_HEADS)
    x2 = x.reshape(batch * seq, d)
    for l in range(depth):
        lam_init = jnp.full((1,), 0.8 - 0.6 * math.exp(-0.3 * l), F32)
        q, k, v, u, sv, ga, gb = _projection(
            x2, ln_mix_g[l][None], w_in[l].astype(BF16),
            jnp.tile(q_norm_g[l], 2 * N_HEADS)[None], jnp.tile(k_norm_g[l], 2 * N_HEADS)[None],
            sg_norm_g[l][None])
        lamv = jnp.stack([lambda_q1[l], lambda_k1[l], lambda_q2[l], lambda_k2[l]])
        oa = _attention(q, k, v, lamv, subln_g[l][None], q_norm_g[l][None], k_norm_g[l][None],
                        slopes, lam_init, batch, seq)
        sgb_full = jnp.repeat(sg_b[l].T, GROUP_DIM, axis=1)
        x2 = _merge(x2, oa, u, sv, ga, gb, sg_w[l].astype(BF16), sgb_full,
                    w_branch_a[l].astype(BF16), w_branch_b[l].astype(BF16), w_out[l].astype(BF16))
        x2 = _ffn(x2, ln_ffn_g[l][None], w_up[l].astype(BF16), conv_w[l], conv_b[l][None],
                  w_down[l].astype(BF16), seq)
    return x2.reshape(batch, seq, d)
```

```python
import functools
import math

import jax
import jax.numpy as jnp
from jax import lax
from jax.experimental import pallas as pl
from jax.experimental.pallas import tpu as pltpu

F32 = jnp.float32
BF16 = jnp.bfloat16

D_MODEL = 1024
N_HEADS = 8
HEAD_DIM = 64
V_DIM = 2 * HEAD_DIM
CHUNK = 128
N_GROUPS = 8
GROUP_DIM = D_MODEL // N_GROUPS
D_FF = 2816
EPS = 1e-6
N_SECTIONS = 7

V7X_MXU_DIM = 256
V7X_VMEM_LIMIT = 56 * 1024 * 1024

PROJ_ROWS = 512
PROJ_COLS = 512
ATTN_BLOCK = 256
ATTN_VT_ROWS = V_DIM + 16
ATTN_SLOTS = 4
ATTN_TILES = 4
LOG2E = math.log2(math.e)
ATTN_MAX_SHIFT = 30.0
ATTN_ZERO_LOG2 = 160.0
ATTN_NULL_SCORE = 30000.0
MERGE_ROWS = 512
FFN_ROWS = 512
FFN_HALO = 16
FFN_COLS = 512


def _rms_rows(x, g):
    return x * lax.rsqrt(jnp.mean(x * x, axis=-1, keepdims=True) + EPS) * g


def _gelu_tanh(x):
    c = math.sqrt(2.0 / math.pi)
    return 0.5 * x * (1.0 + jnp.tanh(c * (x + 0.044715 * (x * x * x))))


def _sigmoid(x):
    return 1.0 / (1.0 + jnp.exp(-x))


def _group_mean_matrix(group):
    r = lax.broadcasted_iota(jnp.int32, (V7X_MXU_DIM, V7X_MXU_DIM), 0) // group
    c = lax.broadcasted_iota(jnp.int32, (V7X_MXU_DIM, V7X_MXU_DIM), 1) // group
    return jnp.where(r == c, 1.0 / group, 0.0).astype(BF16)


def _group_rms_scale(a, mean_mat):
    sq = (a * a).astype(BF16)
    w = mean_mat.shape[0]
    ms = jnp.concatenate([jnp.dot(sq[:, lo:lo + w], mean_mat, preferred_element_type=F32)
                          for lo in range(0, a.shape[1], w)], axis=1)
    return lax.rsqrt(ms + EPS)


def _proj_kernel(x_ref, g_ref, w_ref, qg_ref, kg_ref, sgg_ref,
                 q_ref, k_ref, v_ref, u_ref, sv_ref, ga_ref, gb_ref):
    h = _rms_rows(x_ref[...], g_ref[...]).astype(BF16)
    mean64 = _group_mean_matrix(HEAD_DIM)
    mean128 = _group_mean_matrix(GROUP_DIM)
    w = PROJ_COLS

    def section(s, c):
        lo = s * D_MODEL + c * w
        return jnp.dot(h, w_ref[:, lo:lo + w], preferred_element_type=F32)

    for c in range(D_MODEL // w):
        cols = slice(c * w, (c + 1) * w)
        a = section(0, c)
        q_ref[:, cols] = (a * _group_rms_scale(a, mean64) * (qg_ref[:, cols] * (HEAD_DIM ** -0.5 * LOG2E))).astype(BF16)
        a = section(1, c)
        k_ref[:, cols] = (a * _group_rms_scale(a, mean64) * kg_ref[:, cols]).astype(BF16)
        v_ref[:, cols] = section(2, c).astype(BF16)
        u_ref[:, cols] = _gelu_tanh(section(3, c)).astype(BF16)
        a = _gelu_tanh(section(4, c))
        sv_ref[:, cols] = (a * _group_rms_scale(a, mean128) * sgg_ref[:, cols]).astype(BF16)
        ga_ref[:, cols] = _sigmoid(section(5, c)).astype(BF16)
        gb_ref[:, cols] = _sigmoid(section(6, c)).astype(BF16)


def _projection(x2, ln_g, w_in, q_g, k_g, sg_g):
    t = x2.shape[0]
    row = lambda i: (i, 0)
    fixed = lambda i: (0, 0)
    vec = pl.BlockSpec((1, D_MODEL), fixed)
    out = jax.ShapeDtypeStruct((t, D_MODEL), BF16)
    return pl.pallas_call(
        _proj_kernel,
        grid=(t // PROJ_ROWS,),
        in_specs=[
            pl.BlockSpec((PROJ_ROWS, D_MODEL), row),
            vec,
            pl.BlockSpec((D_MODEL, N_SECTIONS * D_MODEL), fixed, pipeline_mode=pl.Buffered(1)),
            vec, vec, vec,
        ],
        out_specs=[pl.BlockSpec((PROJ_ROWS, D_MODEL), row)] * N_SECTIONS,
        out_shape=[out] * N_SECTIONS,
        compiler_params=pltpu.CompilerParams(
            dimension_semantics=("arbitrary",), vmem_limit_bytes=V7X_VMEM_LIMIT),
        name="proj",
    )(x2, ln_g, w_in, q_g, k_g, sg_g)


def _split3(c):
    hi = c.astype(BF16)
    mid = (c - hi.astype(F32)).astype(BF16)
    lo = (c - hi.astype(F32) - mid.astype(F32)).astype(BF16)
    return hi.astype(F32), mid.astype(F32), lo.astype(F32)


def _pick_rows(row, entries, default=0.0):
    out = default
    for lo, hi, val in reversed(entries):
        out = jnp.where((row >= lo) & (row < hi), val, out)
    return out


def _attn_kernel(slopes_ref, lam_init_ref, q_ref, k_ref, v_ref, lamv_ref, subg_ref, qg_ref, kg_ref, o_ref,
                 k1_ref, k2_ref, vt_ref, dbias_ref, shift_ref, flag_ref, qv_ref, s_ref, p_ref,
                 alpha_ref, m_ref, acc_ref, *, shifted_heads):
    head = pl.program_id(1)
    step = pl.program_id(2)
    blk = ATTN_BLOCK
    seq = k_ref.shape[0]
    nblk = seq // blk
    c = jnp.full((1, 1), slopes_ref[head] * LOG2E, F32)
    c3 = _split3(c)
    c_blk = c * float(blk)

    @pl.when(step == 0)
    def _():
        lane_b = lax.broadcasted_iota(jnp.int32, (blk, V_DIM), 1)
        off = lax.broadcasted_iota(jnp.int32, (blk, V_DIM), 0).astype(F32)
        for mp, ref in enumerate((k1_ref, k2_ref)):
            a = lane_b - (HEAD_DIM if mp == 0 else 0)
            own_b = (lane_b < HEAD_DIM) if mp == 0 else (lane_b >= HEAD_DIM)
            fixed = _pick_rows(a, [(0, 1, -c3[0]), (1, 2, -c3[1]), (2, 3, -c3[2]), (3, 6, off), (9, 12, 1.0)])
            for jb in range(nblk):
                cb3 = _split3(c_blk * float(jb))
                block_lanes = _pick_rows(a[0:1], [(6, 7, cb3[0]), (7, 8, cb3[1]), (8, 9, cb3[2])])
                rows = slice(jb * blk, (jb + 1) * blk)
                ref[rows, :] = jnp.where(own_b, k_ref[rows, :], (fixed + block_lanes).astype(BF16))
        gmax = (jnp.max(jnp.abs(qg_ref[...]), axis=1, keepdims=True)
                * jnp.max(jnp.abs(kg_ref[...]), axis=1, keepdims=True))
        bound = 1.01 * (HEAD_DIM * HEAD_DIM ** -0.5 * LOG2E) * gmax + 0.01
        shift_ref[...] = jnp.broadcast_to(bound, shift_ref.shape)
        flag_ref[0] = (bound[0, 0] <= ATTN_MAX_SHIFT).astype(jnp.int32)
        cols = 512
        for s0 in range(0, seq, cols):
            vt_ref[0:V_DIM, s0:s0 + cols] = v_ref[s0:s0 + cols, :].astype(F32).T.astype(BF16)
        r = lax.broadcasted_iota(jnp.int32, (ATTN_VT_ROWS - V_DIM, seq), 0)
        vt_ref[V_DIM:, :] = jnp.where(r == 0, 1.0, 0.0).astype(BF16)
        d = lax.broadcasted_iota(jnp.int32, (blk, blk), 0) - lax.broadcasted_iota(jnp.int32, (blk, blk), 1)
        dbias_ref[...] = -(c * jnp.abs(d).astype(F32))

    shifted_ok = flag_ref[0] == 1
    tiles = ATTN_TILES
    row = lax.broadcasted_iota(jnp.int32, (HEAD_DIM, blk), 0)
    il = lax.broadcasted_iota(jnp.int32, (HEAD_DIM, blk), 1).astype(F32)
    lamv = lamv_ref[...]
    lam_init = lam_init_ref[0]
    lam = (jnp.exp(jnp.sum(lamv[0:1] * lamv[1:2], axis=-1, keepdims=True))
           - jnp.exp(jnp.sum(lamv[2:3] * lamv[3:4], axis=-1, keepdims=True)) + lam_init)

    def query_rows(t):
        qt = q_ref[t * blk:(t + 1) * blk, :].astype(F32).T
        return qt[0:HEAD_DIM], qt[HEAD_DIM:]

    def with_lanes(q_rows, mp, aug):
        parts = [q_rows[0], aug] if mp == 0 else [aug, q_rows[1]]
        return jnp.concatenate(parts, axis=0).astype(BF16)

    def offset_lanes(sigma):
        return [(0, 3, sigma * il), (3, 4, sigma * c3[0]), (4, 5, sigma * c3[1]), (5, 6, sigma * c3[2])]

    def finish(t, outs, sums):
        ot = outs[0] / sums[0] - lam * (outs[1] / sums[1])
        ot = ot * lax.rsqrt(jnp.mean(ot * ot, axis=0, keepdims=True) + EPS)
        o_ref[t * blk:(t + 1) * blk, :] = (ot.T * (subg_ref[...] * (1.0 - lam_init))).astype(BF16)

    def shifted_path(dk):
        shift = shift_ref[0:1, 0:1]
        s3 = _split3(-shift)
        qis, diag_qs = [], []
        for t in range(tiles):
            qi = tiles * step + t
            cbi = c_blk * qi.astype(F32)
            q_rows = query_rows(t)
            lo3 = _split3(-cbi - shift)
            hi3 = _split3(cbi - shift)
            diag_q = []
            for mp in range(2):
                qv_ref[t, 0, mp] = with_lanes(q_rows, mp, _pick_rows(row, offset_lanes(1.0) + [
                    (6, 9, 1.0), (9, 10, lo3[0]), (10, 11, lo3[1]), (11, 12, lo3[2])]))
                qv_ref[t, 1, mp] = with_lanes(q_rows, mp, _pick_rows(row, offset_lanes(-1.0) + [
                    (6, 9, -1.0), (9, 10, hi3[0]), (10, 11, hi3[1]), (11, 12, hi3[2])]))
                qv_ref[t, 2, mp] = with_lanes(q_rows, mp, _pick_rows(row, [(9, 10, -ATTN_NULL_SCORE)]))
                diag_q.append(with_lanes(q_rows, mp, _pick_rows(
                    row, [(9, 10, s3[0]), (10, 11, s3[1]), (11, 12, s3[2])])))
            qis.append(qi)
            diag_qs.append(diag_q)

        def visit_block(t, u):
            qi = qis[t]
            if dk is None:
                j = (u - 1) + (u - 1 >= qi).astype(jnp.int32)
                return j, (j > qi).astype(jnp.int32)
            j = qi - dk + (u - 1) if u <= dk else qi + (u - dk)
            valid = (j >= 0) & (j < nblk)
            return jnp.clip(j, 0, nblk - 1), jnp.where(valid, 0 if u <= dk else 1, 2)

        def scores_stage(t, u):
            if u == 0:
                j, qts, bias = qis[t], diag_qs[t], dbias_ref[...]
            else:
                j, var = visit_block(t, u)
                qts, bias = (qv_ref[t, var, 0], qv_ref[t, var, 1]), None
            start = pl.multiple_of(j * blk, blk)
            sums = []
            for mp, kref in enumerate((k1_ref, k2_ref)):
                s = jnp.dot(kref[pl.ds(start, blk), :], qts[mp], preferred_element_type=F32)
                p = jnp.exp2(s if bias is None else s + bias)
                p_ref[t, u, mp] = p.astype(BF16)
                sums.append(jnp.sum(p.reshape(blk // 8, 8, blk), axis=0))
            lsums[t] = sums if lsums[t] is None else [lsums[t][mp] + sums[mp] for mp in range(2)]

        def weighted_values(t, u):
            j = qis[t] if u == 0 else visit_block(t, u)[0]
            vt = vt_ref[0:V_DIM, pl.ds(pl.multiple_of(j * blk, blk), blk)]
            pv = [jnp.dot(vt, p_ref[t, u, mp], preferred_element_type=F32) for mp in range(2)]
            accs[t] = pv if accs[t] is None else [accs[t][mp] + pv[mp] for mp in range(2)]

        visits = nblk if dk is None else 2 * dk + 1
        accs = [None] * tiles
        lsums = [None] * tiles
        for t in range(tiles):
            scores_stage(t, 0)
        for u in range(visits):
            for t in range(tiles):
                if u + 1 < visits:
                    scores_stage(t, u + 1)
                weighted_values(t, u)
        for t in range(tiles):
            finish(t, accs[t], [jnp.sum(l, axis=0, keepdims=True) for l in lsums[t]])

    heads_by_radius = {}
    for h, dk in shifted_heads.items():
        heads_by_radius.setdefault(dk, []).append(h)
    for dk, heads in heads_by_radius.items():
        in_group = functools.reduce(jnp.logical_or, [head == h for h in heads])
        pl.when(shifted_ok & in_group)(functools.partial(shifted_path, dk))

    def general_tile(t):
        qi = tiles * step + t

        def key_block(n):
            return jnp.where(n == 0, qi, n - (n <= qi).astype(jnp.int32))

        def scores_stage(j, qts, slot, bias=None):
            start = pl.multiple_of(j * blk, blk)
            for mp, kref in enumerate((k1_ref, k2_ref)):
                s = jnp.dot(kref[pl.ds(start, blk), :], qts[mp], preferred_element_type=F32)
                s_ref[slot, mp] = s if bias is None else s + bias

        def softmax_stage(j, slot, pslot):
            const = -(c_blk * jnp.abs(qi - j).astype(F32))
            for mp in range(2):
                s = s_ref[slot, mp]
                m_old = m_ref[mp]
                m_new = jnp.maximum(m_old, jnp.max(s, axis=0, keepdims=True) + const)
                p_ref[t, pslot, mp] = jnp.exp2(s - (m_new - const)).astype(BF16)
                alpha_ref[pslot, mp] = jnp.exp2(m_old - m_new)
                m_ref[mp] = m_new

        def values_stage(j, pslot):
            start = pl.multiple_of(j * blk, blk)
            vt = vt_ref[:, pl.ds(start, blk)]
            for mp in range(2):
                pv = jnp.dot(vt, p_ref[t, pslot, mp], preferred_element_type=F32)
                acc_ref[mp] = acc_ref[mp] * alpha_ref[pslot, mp] + pv

        q_rows = query_rows(t)
        for mp in range(2):
            qv_ref[t, 0, mp] = with_lanes(q_rows, mp, _pick_rows(row, offset_lanes(1.0)))
            qv_ref[t, 1, mp] = with_lanes(q_rows, mp, _pick_rows(row, offset_lanes(-1.0)))
        m_ref[...] = jnp.full(m_ref.shape, -jnp.inf, F32)
        acc_ref[...] = jnp.zeros(acc_ref.shape, F32)
        zeros = jnp.zeros((HEAD_DIM, blk), F32)
        scores_stage(qi, (with_lanes(q_rows, 0, zeros), with_lanes(q_rows, 1, zeros)), 0, dbias_ref[...])
        j1 = key_block(1)
        side = (j1 > qi).astype(jnp.int32)
        scores_stage(j1, (qv_ref[t, side, 0], qv_ref[t, side, 1]), 1)
        for n in range(nblk):
            if n > 0:
                values_stage(key_block(n - 1), (n - 1) % 2)
            softmax_stage(key_block(n), n % ATTN_SLOTS, n % 2)
            if n + 2 < nblk:
                ja = key_block(n + 2)
                side = (ja > qi).astype(jnp.int32)
                scores_stage(ja, (qv_ref[t, side, 0], qv_ref[t, side, 1]), (n + 2) % ATTN_SLOTS)
        values_stage(key_block(nblk - 1), (nblk - 1) % 2)
        finish(t, [acc_ref[mp][0:V_DIM] for mp in range(2)],
               [acc_ref[mp][V_DIM:V_DIM + 1] for mp in range(2)])

    @pl.when(jnp.logical_not(shifted_ok))
    def _():
        for t in range(tiles):
            general_tile(t)


def _attention(q, k, v, lamv, sub_g, q_g, k_g, slopes, lam_init, batch, seq):
    smem = pl.BlockSpec(memory_space=pltpu.SMEM)
    blk = ATTN_BLOCK
    nq = seq // blk
    tiles = ATTN_TILES
    steps = nq // tiles
    k3 = k.reshape(batch, seq, D_MODEL)
    v3 = v.reshape(batch, seq, D_MODEL)
    shifted_heads = {}
    for h in range(N_HEADS):
        slope = 2.0 ** (-8.0 * (h + 1) / N_HEADS)
        dk = math.ceil((ATTN_ZERO_LOG2 / (slope * LOG2E) - 1.0) / blk)
        shifted_heads[h] = dk if 2 * dk + 1 < nq else None
    return pl.pallas_call(
        functools.partial(_attn_kernel, shifted_heads=shifted_heads),
        grid=(batch, N_HEADS, steps),
        in_specs=[
            smem, smem,
            pl.BlockSpec((tiles * blk, V_DIM), lambda b, h, i: (b * steps + i, h)),
            pl.BlockSpec((None, seq, V_DIM), lambda b, h, i: (b, 0, h)),
            pl.BlockSpec((None, seq, V_DIM), lambda b, h, i: (b, 0, h)),
            pl.BlockSpec((4, HEAD_DIM), lambda b, h, i: (0, 0)),
            pl.BlockSpec((1, V_DIM), lambda b, h, i: (0, 0)),
            pl.BlockSpec((1, HEAD_DIM), lambda b, h, i: (0, 0)),
            pl.BlockSpec((1, HEAD_DIM), lambda b, h, i: (0, 0)),
        ],
        out_specs=pl.BlockSpec((tiles * blk, V_DIM), lambda b, h, i: (b * steps + i, h)),
        out_shape=jax.ShapeDtypeStruct((batch * seq, D_MODEL), BF16),
        scratch_shapes=[
            pltpu.VMEM((seq, V_DIM), BF16),
            pltpu.VMEM((seq, V_DIM), BF16),
            pltpu.VMEM((ATTN_VT_ROWS, seq), BF16),
            pltpu.VMEM((blk, blk), F32),
            pltpu.VMEM((8, 128), F32),
            pltpu.SMEM((1,), jnp.int32),
            pltpu.VMEM((tiles, 3, 2, V_DIM, blk), BF16),
            pltpu.VMEM((ATTN_SLOTS, 2, blk, blk), F32),
            pltpu.VMEM((tiles, nq, 2, blk, blk), BF16),
            pltpu.VMEM((2, 2, 1, blk), F32),
            pltpu.VMEM((2, 1, blk), F32),
            pltpu.VMEM((2, ATTN_VT_ROWS, blk), F32),
        ],
        compiler_params=pltpu.CompilerParams(
            dimension_semantics=("arbitrary", "arbitrary", "arbitrary"),
            vmem_limit_bytes=V7X_VMEM_LIMIT),
        name="diff_attn",
    )(slopes, lam_init, q, k3, v3, lamv, sub_g, q_g, k_g)


def _merge_kernel(x_ref, oa_ref, u_ref, sv_ref, ga_ref, gb_ref, sgw_ref, sgb_ref,
                  wa_ref, wb_ref, wo_ref, o_ref, ob_ref):
    for c in range(MERGE_ROWS // CHUNK):
        rows = slice(c * CHUNK, (c + 1) * CHUNK)
        for g in range(N_GROUPS):
            cols = slice(g * GROUP_DIM, (g + 1) * GROUP_DIM)
            mixed = jnp.dot(sgw_ref[g], sv_ref[rows, cols], preferred_element_type=F32)
            ob_ref[rows, cols] = (u_ref[rows, cols].astype(F32) * (mixed + sgb_ref[:, cols])).astype(BF16)
    ya = jnp.dot(oa_ref[...], wa_ref[...], preferred_element_type=F32)
    yb = jnp.dot(ob_ref[...], wb_ref[...], preferred_element_type=F32)
    merged = (ga_ref[...].astype(F32) * ya + gb_ref[...].astype(F32) * yb).astype(BF16)
    o_ref[...] = x_ref[...] + jnp.dot(merged, wo_ref[...], preferred_element_type=F32)


def _merge(x2, oa, u, sv, ga, gb, sgw, sgb_full, wa, wb, wo):
    t = x2.shape[0]
    row = lambda i: (i, 0)
    fixed2 = lambda i: (0, 0)
    tile = pl.BlockSpec((MERGE_ROWS, D_MODEL), row)
    wspec = pl.BlockSpec((D_MODEL, D_MODEL), fixed2, pipeline_mode=pl.Buffered(1))
    return pl.pallas_call(
        _merge_kernel,
        grid=(t // MERGE_ROWS,),
        in_specs=[
            tile, tile, tile, tile, tile, tile,
            pl.BlockSpec((N_GROUPS, CHUNK, CHUNK), lambda i: (0, 0, 0)),
            pl.BlockSpec((CHUNK, D_MODEL), fixed2),
            wspec, wspec, wspec,
        ],
        out_specs=tile,
        out_shape=jax.ShapeDtypeStruct((t, D_MODEL), F32),
        scratch_shapes=[pltpu.VMEM((MERGE_ROWS, D_MODEL), BF16)],
        compiler_params=pltpu.CompilerParams(
            dimension_semantics=("arbitrary",), vmem_limit_bytes=V7X_VMEM_LIMIT),
        name="sgu_merge",
    )(x2, oa, u, sv, ga, gb, sgw, sgb_full, wa, wb, wo)


def _ffn_kernel(x_ref, prev_ref, next_ref, g_ref, wup_ref, cw_ref, cb_ref, wdn_ref,
                o_ref, h_ref, act_ref, *, blocks_per_seq):
    i = pl.program_id(0)
    tm, halo = FFN_ROWS, FFN_HALO
    g = g_ref[...]
    x = x_ref[...]
    keep_prev = (i % blocks_per_seq != 0).astype(F32)
    keep_next = ((i + 1) % blocks_per_seq != 0).astype(F32)
    h_ref[0:halo, :] = (_rms_rows(prev_ref[...], g) * keep_prev).astype(BF16)
    h_ref[halo:halo + tm, :] = _rms_rows(x, g).astype(BF16)
    h_ref[halo + tm:, :] = (_rms_rows(next_ref[...], g) * keep_next).astype(BF16)
    h = h_ref[...]
    ext = tm + 2 * halo

    def conv(cols):
        up = jnp.dot(h, wup_ref[:, cols], preferred_element_type=F32)
        y = (pltpu.roll(up, 1, 0) * cw_ref[0:1, cols] + up * cw_ref[1:2, cols]
             + pltpu.roll(up, ext - 1, 0) * cw_ref[2:3, cols])
        return y[halo:halo + tm] + cb_ref[:, cols]

    for lo in range(0, D_FF, FFN_COLS):
        hi = min(lo + FFN_COLS, D_FF)
        gate = conv(slice(lo, hi))
        val = conv(slice(D_FF + lo, D_FF + hi))
        act_ref[:, lo:hi] = (gate * _sigmoid(gate) * val).astype(BF16)
    o_ref[...] = x + jnp.dot(act_ref[...], wdn_ref[...], preferred_element_type=F32)


def _ffn(x2, ln_g, wup, conv_w, conv_b, wdn, seq):
    t = x2.shape[0]
    tm, halo = FFN_ROWS, FFN_HALO
    r = tm // halo
    last = t // halo - 1
    fixed = lambda i: (0, 0)
    return pl.pallas_call(
        functools.partial(_ffn_kernel, blocks_per_seq=seq // tm),
        grid=(t // tm,),
        in_specs=[
            pl.BlockSpec((tm, D_MODEL), lambda i: (i, 0)),
            pl.BlockSpec((halo, D_MODEL), lambda i: (jnp.maximum(i * r - 1, 0), 0)),
            pl.BlockSpec((halo, D_MODEL), lambda i: (jnp.minimum((i + 1) * r, last), 0)),
            pl.BlockSpec((1, D_MODEL), fixed),
            pl.BlockSpec((D_MODEL, 2 * D_FF), fixed, pipeline_mode=pl.Buffered(1)),
            pl.BlockSpec((3, 2 * D_FF), fixed),
            pl.BlockSpec((1, 2 * D_FF), fixed),
            pl.BlockSpec((D_FF, D_MODEL), fixed, pipeline_mode=pl.Buffered(1)),
        ],
        out_specs=pl.BlockSpec((tm, D_MODEL), lambda i: (i, 0)),
        out_shape=jax.ShapeDtypeStruct((t, D_MODEL), F32),
        scratch_shapes=[pltpu.VMEM((tm + 2 * halo, D_MODEL), BF16),
                        pltpu.VMEM((tm, D_FF), BF16)],
        compiler_params=pltpu.CompilerParams(
            dimension_semantics=("arbitrary",), vmem_limit_bytes=V7X_VMEM_LIMIT),
        name="conv_ffn",
    )(x2, x2, x2, ln_g, wup, conv_w, conv_b, wdn)


def kernel(x, ln_mix_g, w_in, q_norm_g, k_norm_g, lambda_q1, lambda_k1, lambda_q2, lambda_k2,
           subln_g, sg_norm_g, sg_w, sg_b, w_branch_a, w_branch_b, w_out, ln_ffn_g, w_up,
           conv_w, conv_b, w_down):
    batch, seq, d = x.shape
    depth = w_in.shape[0]
    slopes = 2.0 ** (-8.0 * jnp.arange(1, N_HEADS + 1, dtype=F32) / N_HEADS)
    x2 = x.reshape(batch * seq, d)
    for l in range(depth):
        lam_init = jnp.full((1,), 0.8 - 0.6 * math.exp(-0.3 * l), F32)
        q, k, v, u, sv, ga, gb = _projection(
            x2, ln_mix_g[l][None], w_in[l].astype(BF16),
            jnp.tile(q_norm_g[l], 2 * N_HEADS)[None], jnp.tile(k_norm_g[l], 2 * N_HEADS)[None],
            sg_norm_g[l][None])
        lamv = jnp.stack([lambda_q1[l], lambda_k1[l], lambda_q2[l], lambda_k2[l]])
        oa = _attention(q, k, v, lamv, subln_g[l][None], q_norm_g[l][None], k_norm_g[l][None],
                        slopes, lam_init, batch, seq)
        sgb_full = jnp.repeat(sg_b[l].T, GROUP_DIM, axis=1)
        x2 = _merge(x2, oa, u, sv, ga, gb, sg_w[l].astype(BF16), sgb_full,
                    w_branch_a[l].astype(BF16), w_branch_b[l].astype(BF16), w_out[l].astype(BF16))
        x2 = _ffn(x2, ln_ffn_g[l][None], w_up[l].astype(BF16), conv_w[l], conv_b[l][None],
                  w_down[l].astype(BF16), seq)
    return x2.reshape(batch, seq, d)
```

```python
import functools
import math

import jax
import jax.numpy as jnp
from jax import lax
from jax.experimental import pallas as pl
from jax.experimental.pallas import tpu as pltpu

F32 = jnp.float32
BF16 = jnp.bfloat16

D_MODEL = 1024
N_HEADS = 8
HEAD_DIM = 64
V_DIM = 2 * HEAD_DIM
CHUNK = 128
N_GROUPS = 8
GROUP_DIM = D_MODEL // N_GROUPS
D_FF = 2816
EPS = 1e-6
N_SECTIONS = 7

V7X_MXU_DIM = 256
V7X_VMEM_LIMIT = 56 * 1024 * 1024

PROJ_ROWS = 512
PROJ_COLS = 512
ATTN_BLOCK = 256
ATTN_VT_ROWS = V_DIM + 16
ATTN_SLOTS = 4
ATTN_TILES = 4
LOG2E = math.log2(math.e)
ATTN_MAX_SHIFT = 30.0
ATTN_ZERO_LOG2 = 160.0
ATTN_NULL_SCORE = 30000.0
MERGE_ROWS = 512
FFN_ROWS = 1024
FFN_HALO = 16
FFN_COLS = 512


def _rms_rows(x, g):
    return x * lax.rsqrt(jnp.mean(x * x, axis=-1, keepdims=True) + EPS) * g


def _gelu_tanh(x):
    c = math.sqrt(2.0 / math.pi)
    return 0.5 * x * (1.0 + jnp.tanh(c * (x + 0.044715 * (x * x * x))))


def _sigmoid(x):
    return 1.0 / (1.0 + jnp.exp(-x))


def _group_mean_matrix(group):
    r = lax.broadcasted_iota(jnp.int32, (V7X_MXU_DIM, V7X_MXU_DIM), 0) // group
    c = lax.broadcasted_iota(jnp.int32, (V7X_MXU_DIM, V7X_MXU_DIM), 1) // group
    return jnp.where(r == c, 1.0 / group, 0.0).astype(BF16)


def _group_rms_scale(a, mean_mat):
    sq = (a * a).astype(BF16)
    w = mean_mat.shape[0]
    ms = jnp.concatenate([jnp.dot(sq[:, lo:lo + w], mean_mat, preferred_element_type=F32)
                          for lo in range(0, a.shape[1], w)], axis=1)
    return lax.rsqrt(ms + EPS)


def _proj_kernel(x_ref, g_ref, w_ref, qg_ref, kg_ref, sgg_ref,
                 q_ref, k_ref, v_ref, u_ref, sv_ref, ga_ref, gb_ref):
    h = _rms_rows(x_ref[...], g_ref[...]).astype(BF16)
    mean64 = _group_mean_matrix(HEAD_DIM)
    mean128 = _group_mean_matrix(GROUP_DIM)
    w = PROJ_COLS

    def section(s, c):
        lo = s * D_MODEL + c * w
        return jnp.dot(h, w_ref[:, lo:lo + w], preferred_element_type=F32)

    for c in range(D_MODEL // w):
        cols = slice(c * w, (c + 1) * w)
        a = section(0, c)
        q_ref[:, cols] = (a * _group_rms_scale(a, mean64) * (qg_ref[:, cols] * (HEAD_DIM ** -0.5 * LOG2E))).astype(BF16)
        a = section(1, c)
        k_ref[:, cols] = (a * _group_rms_scale(a, mean64) * kg_ref[:, cols]).astype(BF16)
        v_ref[cols, :] = section(2, c).T.astype(BF16)
        u_ref[:, cols] = _gelu_tanh(section(3, c)).astype(BF16)
        a = _gelu_tanh(section(4, c))
        sv_ref[:, cols] = (a * _group_rms_scale(a, mean128) * sgg_ref[:, cols]).astype(BF16)
        ga_ref[:, cols] = _sigmoid(section(5, c)).astype(BF16)
        gb_ref[:, cols] = _sigmoid(section(6, c)).astype(BF16)


def _projection(x2, ln_g, w_in, q_g, k_g, sg_g):
    t = x2.shape[0]
    row = lambda i: (i, 0)
    fixed = lambda i: (0, 0)
    vec = pl.BlockSpec((1, D_MODEL), fixed)
    out = jax.ShapeDtypeStruct((t, D_MODEL), BF16)
    tile = pl.BlockSpec((PROJ_ROWS, D_MODEL), row)
    return pl.pallas_call(
        _proj_kernel,
        grid=(t // PROJ_ROWS,),
        in_specs=[
            pl.BlockSpec((PROJ_ROWS, D_MODEL), row),
            vec,
            pl.BlockSpec((D_MODEL, N_SECTIONS * D_MODEL), fixed, pipeline_mode=pl.Buffered(1)),
            vec, vec, vec,
        ],
        out_specs=[tile, tile, pl.BlockSpec((D_MODEL, PROJ_ROWS), lambda i: (0, i)), tile, tile, tile, tile],
        out_shape=[out, out, jax.ShapeDtypeStruct((D_MODEL, t), BF16), out, out, out, out],
        compiler_params=pltpu.CompilerParams(
            dimension_semantics=("arbitrary",), vmem_limit_bytes=V7X_VMEM_LIMIT),
        name="proj",
    )(x2, ln_g, w_in, q_g, k_g, sg_g)


def _split3(c):
    hi = c.astype(BF16)
    mid = (c - hi.astype(F32)).astype(BF16)
    lo = (c - hi.astype(F32) - mid.astype(F32)).astype(BF16)
    return hi.astype(F32), mid.astype(F32), lo.astype(F32)


def _pick_rows(row, entries, default=0.0):
    out = default
    for lo, hi, val in reversed(entries):
        out = jnp.where((row >= lo) & (row < hi), val, out)
    return out


def _attn_kernel(slopes_ref, lam_init_ref, q_ref, k_ref, v_ref, lamv_ref, subg_ref, qg_ref, kg_ref, o_ref,
                 k1_ref, k2_ref, vt_ref, dbias_ref, shift_ref, flag_ref, qv_ref, s_ref, p_ref,
                 alpha_ref, m_ref, acc_ref, *, shifted_heads):
    head = pl.program_id(1)
    step = pl.program_id(2)
    blk = ATTN_BLOCK
    seq = k_ref.shape[0]
    nblk = seq // blk
    c = jnp.full((1, 1), slopes_ref[head] * LOG2E, F32)
    c3 = _split3(c)
    c_blk = c * float(blk)

    @pl.when(step == 0)
    def _():
        lane_b = lax.broadcasted_iota(jnp.int32, (blk, V_DIM), 1)
        off = lax.broadcasted_iota(jnp.int32, (blk, V_DIM), 0).astype(F32)
        for mp, ref in enumerate((k1_ref, k2_ref)):
            a = lane_b - (HEAD_DIM if mp == 0 else 0)
            own_b = (lane_b < HEAD_DIM) if mp == 0 else (lane_b >= HEAD_DIM)
            fixed = _pick_rows(a, [(0, 1, -c3[0]), (1, 2, -c3[1]), (2, 3, -c3[2]), (3, 6, off), (9, 12, 1.0)])
            for jb in range(nblk):
                cb3 = _split3(c_blk * float(jb))
                block_lanes = _pick_rows(a[0:1], [(6, 7, cb3[0]), (7, 8, cb3[1]), (8, 9, cb3[2])])
                rows = slice(jb * blk, (jb + 1) * blk)
                ref[rows, :] = jnp.where(own_b, k_ref[rows, :], (fixed + block_lanes).astype(BF16))
        gmax = (jnp.max(jnp.abs(qg_ref[...]), axis=1, keepdims=True)
                * jnp.max(jnp.abs(kg_ref[...]), axis=1, keepdims=True))
        bound = 1.01 * (HEAD_DIM * HEAD_DIM ** -0.5 * LOG2E) * gmax + 0.01
        shift_ref[...] = jnp.broadcast_to(bound, shift_ref.shape)
        flag_ref[0] = (bound[0, 0] <= ATTN_MAX_SHIFT).astype(jnp.int32)
        vt_ref[0:V_DIM, :] = v_ref[...]
        r = lax.broadcasted_iota(jnp.int32, (ATTN_VT_ROWS - V_DIM, seq), 0)
        vt_ref[V_DIM:, :] = jnp.where(r == 0, 1.0, 0.0).astype(BF16)
        d = lax.broadcasted_iota(jnp.int32, (blk, blk), 0) - lax.broadcasted_iota(jnp.int32, (blk, blk), 1)
        dbias_ref[...] = -(c * jnp.abs(d).astype(F32))

    shifted_ok = flag_ref[0] == 1
    tiles = ATTN_TILES
    row = lax.broadcasted_iota(jnp.int32, (HEAD_DIM, blk), 0)
    il = lax.broadcasted_iota(jnp.int32, (HEAD_DIM, blk), 1).astype(F32)
    lamv = lamv_ref[...]
    lam_init = lam_init_ref[0]
    lam = (jnp.exp(jnp.sum(lamv[0:1] * lamv[1:2], axis=-1, keepdims=True))
           - jnp.exp(jnp.sum(lamv[2:3] * lamv[3:4], axis=-1, keepdims=True)) + lam_init)

    def query_rows(t):
        qt = q_ref[t * blk:(t + 1) * blk, :].astype(F32).T
        return qt[0:HEAD_DIM], qt[HEAD_DIM:]

    def with_lanes(q_rows, mp, aug):
        parts = [q_rows[0], aug] if mp == 0 else [aug, q_rows[1]]
        return jnp.concatenate(parts, axis=0).astype(BF16)

    def offset_lanes(sigma):
        return [(0, 3, sigma * il), (3, 4, sigma * c3[0]), (4, 5, sigma * c3[1]), (5, 6, sigma * c3[2])]

    def finish(t, outs, sums):
        ot = outs[0] / sums[0] - lam * (outs[1] / sums[1])
        ot = ot * lax.rsqrt(jnp.mean(ot * ot, axis=0, keepdims=True) + EPS)
        o_ref[t * blk:(t + 1) * blk, :] = (ot.T * (subg_ref[...] * (1.0 - lam_init))).astype(BF16)

    def shifted_path(dk):
        shift = shift_ref[0:1, 0:1]
        s3 = _split3(-shift)
        qis, diag_qs = [], []
        for t in range(tiles):
            qi = tiles * step + t
            cbi = c_blk * qi.astype(F32)
            q_rows = query_rows(t)
            lo3 = _split3(-cbi - shift)
            hi3 = _split3(cbi - shift)
            diag_q = []
            for mp in range(2):
                qv_ref[t, 0, mp] = with_lanes(q_rows, mp, _pick_rows(row, offset_lanes(1.0) + [
                    (6, 9, 1.0), (9, 10, lo3[0]), (10, 11, lo3[1]), (11, 12, lo3[2])]))
                qv_ref[t, 1, mp] = with_lanes(q_rows, mp, _pick_rows(row, offset_lanes(-1.0) + [
                    (6, 9, -1.0), (9, 10, hi3[0]), (10, 11, hi3[1]), (11, 12, hi3[2])]))
                qv_ref[t, 2, mp] = with_lanes(q_rows, mp, _pick_rows(row, [(9, 10, -ATTN_NULL_SCORE)]))
                diag_q.append(with_lanes(q_rows, mp, _pick_rows(
                    row, [(9, 10, s3[0]), (10, 11, s3[1]), (11, 12, s3[2])])))
            qis.append(qi)
            diag_qs.append(diag_q)

        def visit_block(t, u):
            qi = qis[t]
            if dk is None:
                j = (u - 1) + (u - 1 >= qi).astype(jnp.int32)
                return j, (j > qi).astype(jnp.int32)
            j = qi - dk + (u - 1) if u <= dk else qi + (u - dk)
            valid = (j >= 0) & (j < nblk)
            return jnp.clip(j, 0, nblk - 1), jnp.where(valid, 0 if u <= dk else 1, 2)

        def scores_stage(t, u):
            if u == 0:
                j, qts, bias = qis[t], diag_qs[t], dbias_ref[...]
            else:
                j, var = visit_block(t, u)
                qts, bias = (qv_ref[t, var, 0], qv_ref[t, var, 1]), None
            start = pl.multiple_of(j * blk, blk)
            sums = []
            for mp, kref in enumerate((k1_ref, k2_ref)):
                s = jnp.dot(kref[pl.ds(start, blk), :], qts[mp], preferred_element_type=F32)
                p = jnp.exp2(s if bias is None else s + bias)
                p_ref[t, u, mp] = p.astype(BF16)
                sums.append(jnp.sum(p.reshape(blk // 8, 8, blk), axis=0))
            lsums[t] = sums if lsums[t] is None else [lsums[t][mp] + sums[mp] for mp in range(2)]

        def weighted_values(t, u):
            j = qis[t] if u == 0 else visit_block(t, u)[0]
            vt = vt_ref[0:V_DIM, pl.ds(pl.multiple_of(j * blk, blk), blk)]
            pv = [jnp.dot(vt, p_ref[t, u, mp], preferred_element_type=F32) for mp in range(2)]
            accs[t] = pv if accs[t] is None else [accs[t][mp] + pv[mp] for mp in range(2)]

        visits = nblk if dk is None else 2 * dk + 1
        accs = [None] * tiles
        lsums = [None] * tiles
        for t in range(tiles):
            scores_stage(t, 0)
        for u in range(visits):
            for t in range(tiles):
                if u + 1 < visits:
                    scores_stage(t, u + 1)
                weighted_values(t, u)
        for t in range(tiles):
            finish(t, accs[t], [jnp.sum(l, axis=0, keepdims=True) for l in lsums[t]])

    heads_by_radius = {}
    for h, dk in shifted_heads.items():
        heads_by_radius.setdefault(dk, []).append(h)
    for dk, heads in heads_by_radius.items():
        in_group = functools.reduce(jnp.logical_or, [head == h for h in heads])
        pl.when(shifted_ok & in_group)(functools.partial(shifted_path, dk))

    def general_tile(t):
        qi = tiles * step + t

        def key_block(n):
            return jnp.where(n == 0, qi, n - (n <= qi).astype(jnp.int32))

        def scores_stage(j, qts, slot, bias=None):
            start = pl.multiple_of(j * blk, blk)
            for mp, kref in enumerate((k1_ref, k2_ref)):
                s = jnp.dot(kref[pl.ds(start, blk), :], qts[mp], preferred_element_type=F32)
                s_ref[slot, mp] = s if bias is None else s + bias

        def softmax_stage(j, slot, pslot):
            const = -(c_blk * jnp.abs(qi - j).astype(F32))
            for mp in range(2):
                s = s_ref[slot, mp]
                m_old = m_ref[mp]
                m_new = jnp.maximum(m_old, jnp.max(s, axis=0, keepdims=True) + const)
                p_ref[t, pslot, mp] = jnp.exp2(s - (m_new - const)).astype(BF16)
                alpha_ref[pslot, mp] = jnp.exp2(m_old - m_new)
                m_ref[mp] = m_new

        def values_stage(j, pslot):
            start = pl.multiple_of(j * blk, blk)
            vt = vt_ref[:, pl.ds(start, blk)]
            for mp in range(2):
                pv = jnp.dot(vt, p_ref[t, pslot, mp], preferred_element_type=F32)
                acc_ref[mp] = acc_ref[mp] * alpha_ref[pslot, mp] + pv

        q_rows = query_rows(t)
        for mp in range(2):
            qv_ref[t, 0, mp] = with_lanes(q_rows, mp, _pick_rows(row, offset_lanes(1.0)))
            qv_ref[t, 1, mp] = with_lanes(q_rows, mp, _pick_rows(row, offset_lanes(-1.0)))
        m_ref[...] = jnp.full(m_ref.shape, -jnp.inf, F32)
        acc_ref[...] = jnp.zeros(acc_ref.shape, F32)
        zeros = jnp.zeros((HEAD_DIM, blk), F32)
        scores_stage(qi, (with_lanes(q_rows, 0, zeros), with_lanes(q_rows, 1, zeros)), 0, dbias_ref[...])
        j1 = key_block(1)
        side = (j1 > qi).astype(jnp.int32)
        scores_stage(j1, (qv_ref[t, side, 0], qv_ref[t, side, 1]), 1)
        for n in range(nblk):
            if n > 0:
                values_stage(key_block(n - 1), (n - 1) % 2)
            softmax_stage(key_block(n), n % ATTN_SLOTS, n % 2)
            if n + 2 < nblk:
                ja = key_block(n + 2)
                side = (ja > qi).astype(jnp.int32)
                scores_stage(ja, (qv_ref[t, side, 0], qv_ref[t, side, 1]), (n + 2) % ATTN_SLOTS)
        values_stage(key_block(nblk - 1), (nblk - 1) % 2)
        finish(t, [acc_ref[mp][0:V_DIM] for mp in range(2)],
               [acc_ref[mp][V_DIM:V_DIM + 1] for mp in range(2)])

    @pl.when(jnp.logical_not(shifted_ok))
    def _():
        for t in range(tiles):
            general_tile(t)


def _attention(q, k, vt, lamv, sub_g, q_g, k_g, slopes, lam_init, batch, seq):
    smem = pl.BlockSpec(memory_space=pltpu.SMEM)
    blk = ATTN_BLOCK
    nq = seq // blk
    tiles = ATTN_TILES
    steps = nq // tiles
    k3 = k.reshape(batch, seq, D_MODEL)
    shifted_heads = {}
    for h in range(N_HEADS):
        slope = 2.0 ** (-8.0 * (h + 1) / N_HEADS)
        dk = math.ceil((ATTN_ZERO_LOG2 / (slope * LOG2E) - 1.0) / blk)
        shifted_heads[h] = dk if 2 * dk + 1 < nq else None
    return pl.pallas_call(
        functools.partial(_attn_kernel, shifted_heads=shifted_heads),
        grid=(batch, N_HEADS, steps),
        in_specs=[
            smem, smem,
            pl.BlockSpec((tiles * blk, V_DIM), lambda b, h, i: (b * steps + i, h)),
            pl.BlockSpec((None, seq, V_DIM), lambda b, h, i: (b, 0, h)),
            pl.BlockSpec((V_DIM, seq), lambda b, h, i: (h, b)),
            pl.BlockSpec((4, HEAD_DIM), lambda b, h, i: (0, 0)),
            pl.BlockSpec((1, V_DIM), lambda b, h, i: (0, 0)),
            pl.BlockSpec((1, HEAD_DIM), lambda b, h, i: (0, 0)),
            pl.BlockSpec((1, HEAD_DIM), lambda b, h, i: (0, 0)),
        ],
        out_specs=pl.BlockSpec((tiles * blk, V_DIM), lambda b, h, i: (b * steps + i, h)),
        out_shape=jax.ShapeDtypeStruct((batch * seq, D_MODEL), BF16),
        scratch_shapes=[
            pltpu.VMEM((seq, V_DIM), BF16),
            pltpu.VMEM((seq, V_DIM), BF16),
            pltpu.VMEM((ATTN_VT_ROWS, seq), BF16),
            pltpu.VMEM((blk, blk), F32),
            pltpu.VMEM((8, 128), F32),
            pltpu.SMEM((1,), jnp.int32),
            pltpu.VMEM((tiles, 3, 2, V_DIM, blk), BF16),
            pltpu.VMEM((ATTN_SLOTS, 2, blk, blk), F32),
            pltpu.VMEM((tiles, nq, 2, blk, blk), BF16),
            pltpu.VMEM((2, 2, 1, blk), F32),
            pltpu.VMEM((2, 1, blk), F32),
            pltpu.VMEM((2, ATTN_VT_ROWS, blk), F32),
        ],
        compiler_params=pltpu.CompilerParams(
            dimension_semantics=("arbitrary", "arbitrary", "arbitrary"),
            vmem_limit_bytes=V7X_VMEM_LIMIT),
        name="diff_attn",
    )(slopes, lam_init, q, k3, vt, lamv, sub_g, q_g, k_g)


def _merge_kernel(x_ref, oa_ref, u_ref, sv_ref, ga_ref, gb_ref, sgw_ref, sgb_ref,
                  wa_ref, wb_ref, wo_ref, o_ref, ob_ref):
    for c in range(MERGE_ROWS // CHUNK):
        rows = slice(c * CHUNK, (c + 1) * CHUNK)
        for g in range(N_GROUPS):
            cols = slice(g * GROUP_DIM, (g + 1) * GROUP_DIM)
            mixed = jnp.dot(sgw_ref[g], sv_ref[rows, cols], preferred_element_type=F32)
            ob_ref[rows, cols] = (u_ref[rows, cols].astype(F32) * (mixed + sgb_ref[:, cols])).astype(BF16)
    ya = jnp.dot(oa_ref[...], wa_ref[...], preferred_element_type=F32)
    yb = jnp.dot(ob_ref[...], wb_ref[...], preferred_element_type=F32)
    merged = (ga_ref[...].astype(F32) * ya + gb_ref[...].astype(F32) * yb).astype(BF16)
    o_ref[...] = x_ref[...] + jnp.dot(merged, wo_ref[...], preferred_element_type=F32)


def _merge(x2, oa, u, sv, ga, gb, sgw, sgb_full, wa, wb, wo):
    t = x2.shape[0]
    row = lambda i: (i, 0)
    fixed2 = lambda i: (0, 0)
    tile = pl.BlockSpec((MERGE_ROWS, D_MODEL), row)
    wspec = pl.BlockSpec((D_MODEL, D_MODEL), fixed2, pipeline_mode=pl.Buffered(1))
    return pl.pallas_call(
        _merge_kernel,
        grid=(t // MERGE_ROWS,),
        in_specs=[
            tile, tile, tile, tile, tile, tile,
            pl.BlockSpec((N_GROUPS, CHUNK, CHUNK), lambda i: (0, 0, 0)),
            pl.BlockSpec((CHUNK, D_MODEL), fixed2),
            wspec, wspec, wspec,
        ],
        out_specs=tile,
        out_shape=jax.ShapeDtypeStruct((t, D_MODEL), F32),
        scratch_shapes=[pltpu.VMEM((MERGE_ROWS, D_MODEL), BF16)],
        compiler_params=pltpu.CompilerParams(
            dimension_semantics=("arbitrary",), vmem_limit_bytes=V7X_VMEM_LIMIT),
        name="sgu_merge",
    )(x2, oa, u, sv, ga, gb, sgw, sgb_full, wa, wb, wo)


def _ffn_kernel(x_ref, prev_ref, next_ref, g_ref, wup_ref, cw_ref, cb_ref, wdn_ref,
                o_ref, h_ref, act_ref, *, blocks_per_seq):
    i = pl.program_id(0)
    tm, halo = FFN_ROWS, FFN_HALO
    g = g_ref[...]
    x = x_ref[...]
    keep_prev = (i % blocks_per_seq != 0).astype(F32)
    keep_next = ((i + 1) % blocks_per_seq != 0).astype(F32)
    h_ref[0:halo, :] = (_rms_rows(prev_ref[...], g) * keep_prev).astype(BF16)
    h_ref[halo:halo + tm, :] = _rms_rows(x, g).astype(BF16)
    h_ref[halo + tm:, :] = (_rms_rows(next_ref[...], g) * keep_next).astype(BF16)
    h = h_ref[...]
    ext = tm + 2 * halo

    def conv(cols):
        up = jnp.dot(h, wup_ref[:, cols], preferred_element_type=F32)
        y = (pltpu.roll(up, 1, 0) * cw_ref[0:1, cols] + up * cw_ref[1:2, cols]
             + pltpu.roll(up, ext - 1, 0) * cw_ref[2:3, cols])
        return y[halo:halo + tm] + cb_ref[:, cols]

    for lo in range(0, D_FF, FFN_COLS):
        hi = min(lo + FFN_COLS, D_FF)
        gate = conv(slice(lo, hi))
        val = conv(slice(D_FF + lo, D_FF + hi))
        act_ref[:, lo:hi] = (gate * _sigmoid(gate) * val).astype(BF16)
    o_ref[...] = x + jnp.dot(act_ref[...], wdn_ref[...], preferred_element_type=F32)


def _ffn(x2, ln_g, wup, conv_w, conv_b, wdn, seq):
    t = x2.shape[0]
    tm, halo = FFN_ROWS, FFN_HALO
    r = tm // halo
    last = t // halo - 1
    fixed = lambda i: (0, 0)
    return pl.pallas_call(
        functools.partial(_ffn_kernel, blocks_per_seq=seq // tm),
        grid=(t // tm,),
        in_specs=[
            pl.BlockSpec((tm, D_MODEL), lambda i: (i, 0)),
            pl.BlockSpec((halo, D_MODEL), lambda i: (jnp.maximum(i * r - 1, 0), 0)),
            pl.BlockSpec((halo, D_MODEL), lambda i: (jnp.minimum((i + 1) * r, last), 0)),
            pl.BlockSpec((1, D_MODEL), fixed),
            pl.BlockSpec((D_MODEL, 2 * D_FF), fixed, pipeline_mode=pl.Buffered(1)),
            pl.BlockSpec((3, 2 * D_FF), fixed),
            pl.BlockSpec((1, 2 * D_FF), fixed),
            pl.BlockSpec((D_FF, D_MODEL), fixed, pipeline_mode=pl.Buffered(1)),
        ],
        out_specs=pl.BlockSpec((tm, D_MODEL), lambda i: (i, 0)),
        out_shape=jax.ShapeDtypeStruct((t, D_MODEL), F32),
        scratch_shapes=[pltpu.VMEM((tm + 2 * halo, D_MODEL), BF16),
                        pltpu.VMEM((tm, D_FF), BF16)],
        compiler_params=pltpu.CompilerParams(
            dimension_semantics=("arbitrary",), vmem_limit_bytes=V7X_VMEM_LIMIT),
        name="conv_ffn",
    )(x2, x2, x2, ln_g, wup, conv_w, conv_b, wdn)


def kernel(x, ln_mix_g, w_in, q_norm_g, k_norm_g, lambda_q1, lambda_k1, lambda_q2, lambda_k2,
           subln_g, sg_norm_g, sg_w, sg_b, w_branch_a, w_branch_b, w_out, ln_ffn_g, w_up,
           conv_w, conv_b, w_down):
    batch, seq, d = x.shape
    depth = w_in.shape[0]
    slopes = 2.0 ** (-8.0 * jnp.arange(1, N_HEADS + 1, dtype=F32) / N_HEADS)
    x2 = x.reshape(batch * seq, d)
    for l in range(depth):
        lam_init = jnp.full((1,), 0.8 - 0.6 * math.exp(-0.3 * l), F32)
        q, k, vt, u, sv, ga, gb = _projection(
            x2, ln_mix_g[l][None], w_in[l].astype(BF16),
            jnp.tile(q_norm_g[l], 2 * N_HEADS)[None], jnp.tile(k_norm_g[l], 2 * N_HEADS)[None],
            sg_norm_g[l][None])
        lamv = jnp.stack([lambda_q1[l], lambda_k1[l], lambda_q2[l], lambda_k2[l]])
        oa = _attention(q, k, vt, lamv, subln_g[l][None], q_norm_g[l][None], k_norm_g[l][None],
                        slopes, lam_init, batch, seq)
        sgb_full = jnp.repeat(sg_b[l].T, GROUP_DIM, axis=1)
        x2 = _merge(x2, oa, u, sv, ga, gb, sg_w[l].astype(BF16), sgb_full,
                    w_branch_a[l].astype(BF16), w_branch_b[l].astype(BF16), w_out[l].astype(BF16))
        x2 = _ffn(x2, ln_ffn_g[l][None], w_up[l].astype(BF16), conv_w[l], conv_b[l][None],
                  w_down[l].astype(BF16), seq)
    return x2.reshape(batch, seq, d)
```

```python
import functools
import math

import jax
import jax.numpy as jnp
from jax import lax
from jax.experimental import pallas as pl
from jax.experimental.pallas import tpu as pltpu

F32 = jnp.float32
BF16 = jnp.bfloat16

D_MODEL = 1024
N_HEADS = 8
HEAD_DIM = 64
V_DIM = 2 * HEAD_DIM
CHUNK = 128
N_GROUPS = 8
GROUP_DIM = D_MODEL // N_GROUPS
D_FF = 2816
EPS = 1e-6
N_SECTIONS = 7

V7X_MXU_DIM = 256
V7X_VMEM_LIMIT = 56 * 1024 * 1024

PROJ_ROWS = 512
PROJ_COLS = 512
ATTN_BLOCK = 256
ATTN_VT_ROWS = V_DIM + 16
ATTN_SLOTS = 4
ATTN_TILES = 4
LOG2E = math.log2(math.e)
ATTN_MAX_SHIFT = 30.0
ATTN_ZERO_LOG2 = 160.0
ATTN_NULL_SCORE = 30000.0
MERGE_ROWS = 512
FFN_ROWS = 1024
FFN_HALO = 16
FFN_COLS = 512


def _rms_rows(x, g):
    return x * lax.rsqrt(jnp.mean(x * x, axis=-1, keepdims=True) + EPS) * g


def _gelu_tanh(x):
    a = -2.0 * math.sqrt(2.0 / math.pi) * LOG2E
    return x / (1.0 + jnp.exp2(x * (a + (a * 0.044715) * (x * x))))


def _sigmoid(x):
    return 1.0 / (1.0 + jnp.exp2(x * -LOG2E))


def _group_mean_matrix(group):
    r = lax.broadcasted_iota(jnp.int32, (V7X_MXU_DIM, V7X_MXU_DIM), 0) // group
    c = lax.broadcasted_iota(jnp.int32, (V7X_MXU_DIM, V7X_MXU_DIM), 1) // group
    return jnp.where(r == c, 1.0 / group, 0.0).astype(BF16)


def _group_rms_scale(a, mean_mat):
    sq = (a * a).astype(BF16)
    w = mean_mat.shape[0]
    ms = jnp.concatenate([jnp.dot(sq[:, lo:lo + w], mean_mat, preferred_element_type=F32)
                          for lo in range(0, a.shape[1], w)], axis=1)
    return lax.rsqrt(ms + EPS)


def _proj_kernel(x_ref, g_ref, w_ref, qg_ref, kg_ref, sgg_ref,
                 q_ref, k_ref, v_ref, u_ref, sv_ref, ga_ref, gb_ref):
    x = x_ref[...]
    h = (x * g_ref[...]).astype(BF16)
    row_scale = lax.rsqrt(jnp.mean(x * x, axis=-1, keepdims=True) + EPS)
    mean64 = _group_mean_matrix(HEAD_DIM)
    mean128 = _group_mean_matrix(GROUP_DIM)
    w = PROJ_COLS

    def section(s, c):
        lo = s * D_MODEL + c * w
        return jnp.dot(h, w_ref[:, lo:lo + w], preferred_element_type=F32) * row_scale

    for c in range(D_MODEL // w):
        cols = slice(c * w, (c + 1) * w)
        u_ref[:, cols] = _gelu_tanh(section(3, c)).astype(BF16)
        a = section(0, c)
        q_ref[:, cols] = (a * _group_rms_scale(a, mean64) * (qg_ref[:, cols] * (HEAD_DIM ** -0.5 * LOG2E))).astype(BF16)
        a = _gelu_tanh(section(4, c))
        sv_ref[:, cols] = (a * _group_rms_scale(a, mean128) * sgg_ref[:, cols]).astype(BF16)
        a = section(1, c)
        k_ref[:, cols] = (a * _group_rms_scale(a, mean64) * kg_ref[:, cols]).astype(BF16)
        ga_ref[:, cols] = _sigmoid(section(5, c)).astype(BF16)
        v_ref[cols, :] = section(2, c).T.astype(BF16)
        gb_ref[:, cols] = _sigmoid(section(6, c)).astype(BF16)


def _projection(x2, ln_g, w_in, q_g, k_g, sg_g):
    t = x2.shape[0]
    row = lambda i: (i, 0)
    fixed = lambda i: (0, 0)
    vec = pl.BlockSpec((1, D_MODEL), fixed)
    out = jax.ShapeDtypeStruct((t, D_MODEL), BF16)
    tile = pl.BlockSpec((PROJ_ROWS, D_MODEL), row)
    return pl.pallas_call(
        _proj_kernel,
        grid=(t // PROJ_ROWS,),
        in_specs=[
            pl.BlockSpec((PROJ_ROWS, D_MODEL), row),
            vec,
            pl.BlockSpec((D_MODEL, N_SECTIONS * D_MODEL), fixed, pipeline_mode=pl.Buffered(1)),
            vec, vec, vec,
        ],
        out_specs=[tile, tile, pl.BlockSpec((D_MODEL, PROJ_ROWS), lambda i: (0, i)), tile, tile, tile, tile],
        out_shape=[out, out, jax.ShapeDtypeStruct((D_MODEL, t), BF16), out, out, out, out],
        compiler_params=pltpu.CompilerParams(
            dimension_semantics=("arbitrary",), vmem_limit_bytes=V7X_VMEM_LIMIT),
        name="proj",
    )(x2, ln_g, w_in, q_g, k_g, sg_g)


def _split3(c):
    hi = c.astype(BF16)
    mid = (c - hi.astype(F32)).astype(BF16)
    lo = (c - hi.astype(F32) - mid.astype(F32)).astype(BF16)
    return hi.astype(F32), mid.astype(F32), lo.astype(F32)


def _pick_rows(row, entries, default=0.0):
    out = default
    for lo, hi, val in reversed(entries):
        out = jnp.where((row >= lo) & (row < hi), val, out)
    return out


def _attn_kernel(slopes_ref, lam_init_ref, q_ref, k_ref, v_ref, lamv_ref, subg_ref, qg_ref, kg_ref, o_ref,
                 k1_ref, k2_ref, vt_ref, dbias_ref, shift_ref, flag_ref, qv_ref, s_ref, p_ref,
                 alpha_ref, m_ref, acc_ref, *, shifted_heads):
    head = pl.program_id(1)
    step = pl.program_id(2)
    blk = ATTN_BLOCK
    seq = k_ref.shape[0]
    nblk = seq // blk
    c = jnp.full((1, 1), slopes_ref[head] * LOG2E, F32)
    c3 = _split3(c)
    c_blk = c * float(blk)

    @pl.when(step == 0)
    def _():
        lane_b = lax.broadcasted_iota(jnp.int32, (blk, V_DIM), 1)
        off = lax.broadcasted_iota(jnp.int32, (blk, V_DIM), 0).astype(F32)
        for mp, ref in enumerate((k1_ref, k2_ref)):
            a = lane_b - (HEAD_DIM if mp == 0 else 0)
            own_b = (lane_b < HEAD_DIM) if mp == 0 else (lane_b >= HEAD_DIM)
            fixed = _pick_rows(a, [(0, 1, -c3[0]), (1, 2, -c3[1]), (2, 3, -c3[2]), (3, 6, off), (9, 12, 1.0)])
            for jb in range(nblk):
                cb3 = _split3(c_blk * float(jb))
                block_lanes = _pick_rows(a[0:1], [(6, 7, cb3[0]), (7, 8, cb3[1]), (8, 9, cb3[2])])
                rows = slice(jb * blk, (jb + 1) * blk)
                ref[rows, :] = jnp.where(own_b, k_ref[rows, :], (fixed + block_lanes).astype(BF16))
        gmax = (jnp.max(jnp.abs(qg_ref[...]), axis=1, keepdims=True)
                * jnp.max(jnp.abs(kg_ref[...]), axis=1, keepdims=True))
        bound = 1.01 * (HEAD_DIM * HEAD_DIM ** -0.5 * LOG2E) * gmax + 0.01
        shift_ref[...] = jnp.broadcast_to(bound, shift_ref.shape)
        flag_ref[0] = (bound[0, 0] <= ATTN_MAX_SHIFT).astype(jnp.int32)
        vt_ref[0:V_DIM, :] = v_ref[...]
        r = lax.broadcasted_iota(jnp.int32, (ATTN_VT_ROWS - V_DIM, seq), 0)
        vt_ref[V_DIM:, :] = jnp.where(r == 0, 1.0, 0.0).astype(BF16)
        d = lax.broadcasted_iota(jnp.int32, (blk, blk), 0) - lax.broadcasted_iota(jnp.int32, (blk, blk), 1)
        dbias_ref[...] = -(c * jnp.abs(d).astype(F32))

    shifted_ok = flag_ref[0] == 1
    tiles = ATTN_TILES
    row = lax.broadcasted_iota(jnp.int32, (HEAD_DIM, blk), 0)
    il = lax.broadcasted_iota(jnp.int32, (HEAD_DIM, blk), 1).astype(F32)
    lamv = lamv_ref[...]
    lam_init = lam_init_ref[0]
    lam = (jnp.exp(jnp.sum(lamv[0:1] * lamv[1:2], axis=-1, keepdims=True))
           - jnp.exp(jnp.sum(lamv[2:3] * lamv[3:4], axis=-1, keepdims=True)) + lam_init)

    def query_rows(t):
        qt = q_ref[t * blk:(t + 1) * blk, :].astype(F32).T
        return qt[0:HEAD_DIM], qt[HEAD_DIM:]

    def with_lanes(q_rows, mp, aug):
        parts = [q_rows[0], aug] if mp == 0 else [aug, q_rows[1]]
        return jnp.concatenate(parts, axis=0).astype(BF16)

    def offset_lanes(sigma):
        return [(0, 3, sigma * il), (3, 4, sigma * c3[0]), (4, 5, sigma * c3[1]), (5, 6, sigma * c3[2])]

    def finish(t, outs, sums):
        ot = outs[0] / sums[0] - lam * (outs[1] / sums[1])
        ot = ot * lax.rsqrt(jnp.mean(ot * ot, axis=0, keepdims=True) + EPS)
        o_ref[t * blk:(t + 1) * blk, :] = (ot.T * (subg_ref[...] * (1.0 - lam_init))).astype(BF16)

    def shifted_path(dk):
        shift = shift_ref[0:1, 0:1]
        s3 = _split3(-shift)
        qis, diag_qs = [], []
        for t in range(tiles):
            qi = tiles * step + t
            cbi = c_blk * qi.astype(F32)
            q_rows = query_rows(t)
            lo3 = _split3(-cbi - shift)
            hi3 = _split3(cbi - shift)
            diag_q = []
            for mp in range(2):
                qv_ref[t, 0, mp] = with_lanes(q_rows, mp, _pick_rows(row, offset_lanes(1.0) + [
                    (6, 9, 1.0), (9, 10, lo3[0]), (10, 11, lo3[1]), (11, 12, lo3[2])]))
                qv_ref[t, 1, mp] = with_lanes(q_rows, mp, _pick_rows(row, offset_lanes(-1.0) + [
                    (6, 9, -1.0), (9, 10, hi3[0]), (10, 11, hi3[1]), (11, 12, hi3[2])]))
                qv_ref[t, 2, mp] = with_lanes(q_rows, mp, _pick_rows(row, [(9, 10, -ATTN_NULL_SCORE)]))
                diag_q.append(with_lanes(q_rows, mp, _pick_rows(
                    row, [(9, 10, s3[0]), (10, 11, s3[1]), (11, 12, s3[2])])))
            qis.append(qi)
            diag_qs.append(diag_q)

        def visit_block(t, u):
            qi = qis[t]
            if dk is None:
                j = (u - 1) + (u - 1 >= qi).astype(jnp.int32)
                return j, (j > qi).astype(jnp.int32)
            j = qi - dk + (u - 1) if u <= dk else qi + (u - dk)
            valid = (j >= 0) & (j < nblk)
            return jnp.clip(j, 0, nblk - 1), jnp.where(valid, 0 if u <= dk else 1, 2)

        def scores_stage(t, u):
            if u == 0:
                j, qts, bias = qis[t], diag_qs[t], dbias_ref[...]
            else:
                j, var = visit_block(t, u)
                qts, bias = (qv_ref[t, var, 0], qv_ref[t, var, 1]), None
            start = pl.multiple_of(j * blk, blk)
            sums = []
            for mp, kref in enumerate((k1_ref, k2_ref)):
                s = jnp.dot(kref[pl.ds(start, blk), :], qts[mp], preferred_element_type=F32)
                p = jnp.exp2(s if bias is None else s + bias)
                p_ref[t, u, mp] = p.astype(BF16)
                sums.append(jnp.sum(p.reshape(blk // 8, 8, blk), axis=0))
            lsums[t] = sums if lsums[t] is None else [lsums[t][mp] + sums[mp] for mp in range(2)]

        def weighted_values(t, u):
            j = qis[t] if u == 0 else visit_block(t, u)[0]
            vt = vt_ref[0:V_DIM, pl.ds(pl.multiple_of(j * blk, blk), blk)]
            pv = [jnp.dot(vt, p_ref[t, u, mp], preferred_element_type=F32) for mp in range(2)]
            accs[t] = pv if accs[t] is None else [accs[t][mp] + pv[mp] for mp in range(2)]

        visits = nblk if dk is None else 2 * dk + 1
        accs = [None] * tiles
        lsums = [None] * tiles
        for t in range(tiles):
            scores_stage(t, 0)
        for u in range(visits):
            for t in range(tiles):
                if u + 1 < visits:
                    scores_stage(t, u + 1)
                weighted_values(t, u)
        for t in range(tiles):
            finish(t, accs[t], [jnp.sum(l, axis=0, keepdims=True) for l in lsums[t]])

    heads_by_radius = {}
    for h, dk in shifted_heads.items():
        heads_by_radius.setdefault(dk, []).append(h)
    for dk, heads in heads_by_radius.items():
        in_group = functools.reduce(jnp.logical_or, [head == h for h in heads])
        pl.when(shifted_ok & in_group)(functools.partial(shifted_path, dk))

    def general_tile(t):
        qi = tiles * step + t

        def key_block(n):
            return jnp.where(n == 0, qi, n - (n <= qi).astype(jnp.int32))

        def scores_stage(j, qts, slot, bias=None):
            start = pl.multiple_of(j * blk, blk)
            for mp, kref in enumerate((k1_ref, k2_ref)):
                s = jnp.dot(kref[pl.ds(start, blk), :], qts[mp], preferred_element_type=F32)
                s_ref[slot, mp] = s if bias is None else s + bias

        def softmax_stage(j, slot, pslot):
            const = -(c_blk * jnp.abs(qi - j).astype(F32))
            for mp in range(2):
                s = s_ref[slot, mp]
                m_old = m_ref[mp]
                m_new = jnp.maximum(m_old, jnp.max(s, axis=0, keepdims=True) + const)
                p_ref[t, pslot, mp] = jnp.exp2(s - (m_new - const)).astype(BF16)
                alpha_ref[pslot, mp] = jnp.exp2(m_old - m_new)
                m_ref[mp] = m_new

        def values_stage(j, pslot):
            start = pl.multiple_of(j * blk, blk)
            vt = vt_ref[:, pl.ds(start, blk)]
            for mp in range(2):
                pv = jnp.dot(vt, p_ref[t, pslot, mp], preferred_element_type=F32)
                acc_ref[mp] = acc_ref[mp] * alpha_ref[pslot, mp] + pv

        q_rows = query_rows(t)
        for mp in range(2):
            qv_ref[t, 0, mp] = with_lanes(q_rows, mp, _pick_rows(row, offset_lanes(1.0)))
            qv_ref[t, 1, mp] = with_lanes(q_rows, mp, _pick_rows(row, offset_lanes(-1.0)))
        m_ref[...] = jnp.full(m_ref.shape, -jnp.inf, F32)
        acc_ref[...] = jnp.zeros(acc_ref.shape, F32)
        zeros = jnp.zeros((HEAD_DIM, blk), F32)
        scores_stage(qi, (with_lanes(q_rows, 0, zeros), with_lanes(q_rows, 1, zeros)), 0, dbias_ref[...])
        j1 = key_block(1)
        side = (j1 > qi).astype(jnp.int32)
        scores_stage(j1, (qv_ref[t, side, 0], qv_ref[t, side, 1]), 1)
        for n in range(nblk):
            if n > 0:
                values_stage(key_block(n - 1), (n - 1) % 2)
            softmax_stage(key_block(n), n % ATTN_SLOTS, n % 2)
            if n + 2 < nblk:
                ja = key_block(n + 2)
                side = (ja > qi).astype(jnp.int32)
                scores_stage(ja, (qv_ref[t, side, 0], qv_ref[t, side, 1]), (n + 2) % ATTN_SLOTS)
        values_stage(key_block(nblk - 1), (nblk - 1) % 2)
        finish(t, [acc_ref[mp][0:V_DIM] for mp in range(2)],
               [acc_ref[mp][V_DIM:V_DIM + 1] for mp in range(2)])

    @pl.when(jnp.logical_not(shifted_ok))
    def _():
        for t in range(tiles):
            general_tile(t)


def _attention(q, k, vt, lamv, sub_g, q_g, k_g, slopes, lam_init, batch, seq):
    smem = pl.BlockSpec(memory_space=pltpu.SMEM)
    blk = ATTN_BLOCK
    nq = seq // blk
    tiles = ATTN_TILES
    steps = nq // tiles
    k3 = k.reshape(batch, seq, D_MODEL)
    shifted_heads = {}
    for h in range(N_HEADS):
        slope = 2.0 ** (-8.0 * (h + 1) / N_HEADS)
        dk = math.ceil((ATTN_ZERO_LOG2 / (slope * LOG2E) - 1.0) / blk)
        shifted_heads[h] = dk if 2 * dk + 1 < nq else None
    return pl.pallas_call(
        functools.partial(_attn_kernel, shifted_heads=shifted_heads),
        grid=(batch, N_HEADS, steps),
        in_specs=[
            smem, smem,
            pl.BlockSpec((tiles * blk, V_DIM), lambda b, h, i: (b * steps + i, h)),
            pl.BlockSpec((None, seq, V_DIM), lambda b, h, i: (b, 0, h)),
            pl.BlockSpec((V_DIM, seq), lambda b, h, i: (h, b)),
            pl.BlockSpec((4, HEAD_DIM), lambda b, h, i: (0, 0)),
            pl.BlockSpec((1, V_DIM), lambda b, h, i: (0, 0)),
            pl.BlockSpec((1, HEAD_DIM), lambda b, h, i: (0, 0)),
            pl.BlockSpec((1, HEAD_DIM), lambda b, h, i: (0, 0)),
        ],
        out_specs=pl.BlockSpec((tiles * blk, V_DIM), lambda b, h, i: (b * steps + i, h)),
        out_shape=jax.ShapeDtypeStruct((batch * seq, D_MODEL), BF16),
        scratch_shapes=[
            pltpu.VMEM((seq, V_DIM), BF16),
            pltpu.VMEM((seq, V_DIM), BF16),
            pltpu.VMEM((ATTN_VT_ROWS, seq), BF16),
            pltpu.VMEM((blk, blk), F32),
            pltpu.VMEM((8, 128), F32),
            pltpu.SMEM((1,), jnp.int32),
            pltpu.VMEM((tiles, 3, 2, V_DIM, blk), BF16),
            pltpu.VMEM((ATTN_SLOTS, 2, blk, blk), F32),
            pltpu.VMEM((tiles, nq, 2, blk, blk), BF16),
            pltpu.VMEM((2, 2, 1, blk), F32),
            pltpu.VMEM((2, 1, blk), F32),
            pltpu.VMEM((2, ATTN_VT_ROWS, blk), F32),
        ],
        compiler_params=pltpu.CompilerParams(
            dimension_semantics=("arbitrary", "arbitrary", "arbitrary"),
            vmem_limit_bytes=V7X_VMEM_LIMIT),
        name="diff_attn",
    )(slopes, lam_init, q, k3, vt, lamv, sub_g, q_g, k_g)


def _merge_kernel(x_ref, oa_ref, u_ref, sv_ref, ga_ref, gb_ref, sgw_ref, sgb_ref,
                  wa_ref, wb_ref, wo_ref, o_ref, ob_ref):
    for c in range(MERGE_ROWS // CHUNK):
        rows = slice(c * CHUNK, (c + 1) * CHUNK)
        for g in range(N_GROUPS):
            cols = slice(g * GROUP_DIM, (g + 1) * GROUP_DIM)
            mixed = jnp.dot(sgw_ref[g], sv_ref[rows, cols], preferred_element_type=F32)
            ob_ref[rows, cols] = (u_ref[rows, cols].astype(F32) * (mixed + sgb_ref[:, cols])).astype(BF16)
    ya = jnp.dot(oa_ref[...], wa_ref[...], preferred_element_type=F32)
    yb = jnp.dot(ob_ref[...], wb_ref[...], preferred_element_type=F32)
    merged = (ga_ref[...].astype(F32) * ya + gb_ref[...].astype(F32) * yb).astype(BF16)
    o_ref[...] = x_ref[...] + jnp.dot(merged, wo_ref[...], preferred_element_type=F32)


def _merge(x2, oa, u, sv, ga, gb, sgw, sgb_full, wa, wb, wo):
    t = x2.shape[0]
    row = lambda i: (i, 0)
    fixed2 = lambda i: (0, 0)
    tile = pl.BlockSpec((MERGE_ROWS, D_MODEL), row)
    wspec = pl.BlockSpec((D_MODEL, D_MODEL), fixed2, pipeline_mode=pl.Buffered(1))
    return pl.pallas_call(
        _merge_kernel,
        grid=(t // MERGE_ROWS,),
        in_specs=[
            tile, tile, tile, tile, tile, tile,
            pl.BlockSpec((N_GROUPS, CHUNK, CHUNK), lambda i: (0, 0, 0)),
            pl.BlockSpec((CHUNK, D_MODEL), fixed2),
            wspec, wspec, wspec,
        ],
        out_specs=tile,
        out_shape=jax.ShapeDtypeStruct((t, D_MODEL), F32),
        scratch_shapes=[pltpu.VMEM((MERGE_ROWS, D_MODEL), BF16)],
        compiler_params=pltpu.CompilerParams(
            dimension_semantics=("arbitrary",), vmem_limit_bytes=V7X_VMEM_LIMIT),
        name="sgu_merge",
    )(x2, oa, u, sv, ga, gb, sgw, sgb_full, wa, wb, wo)


def _ffn_kernel(x_ref, prev_ref, next_ref, g_ref, wup_ref, cw_ref, cb_ref, wdn_ref,
                o_ref, h_ref, act_ref, *, blocks_per_seq):
    i = pl.program_id(0)
    tm, halo = FFN_ROWS, FFN_HALO
    g = g_ref[...]
    x = x_ref[...]
    keep_prev = (i % blocks_per_seq != 0).astype(F32)
    keep_next = ((i + 1) % blocks_per_seq != 0).astype(F32)
    h_ref[0:halo, :] = (_rms_rows(prev_ref[...], g) * keep_prev).astype(BF16)
    h_ref[halo:halo + tm, :] = _rms_rows(x, g).astype(BF16)
    h_ref[halo + tm:, :] = (_rms_rows(next_ref[...], g) * keep_next).astype(BF16)
    h = h_ref[...]
    ext = tm + 2 * halo

    def conv(cols):
        up = jnp.dot(h, wup_ref[:, cols], preferred_element_type=F32)
        y = (pltpu.roll(up, 1, 0) * cw_ref[0:1, cols] + up * cw_ref[1:2, cols]
             + pltpu.roll(up, ext - 1, 0) * cw_ref[2:3, cols])
        return y[halo:halo + tm] + cb_ref[:, cols]

    for lo in range(0, D_FF, FFN_COLS):
        hi = min(lo + FFN_COLS, D_FF)
        gate = conv(slice(lo, hi))
        val = conv(slice(D_FF + lo, D_FF + hi))
        act_ref[:, lo:hi] = (gate * _sigmoid(gate) * val).astype(BF16)
    o_ref[...] = x + jnp.dot(act_ref[...], wdn_ref[...], preferred_element_type=F32)


def _ffn(x2, ln_g, wup, conv_w, conv_b, wdn, seq):
    t = x2.shape[0]
    tm, halo = FFN_ROWS, FFN_HALO
    r = tm // halo
    last = t // halo - 1
    fixed = lambda i: (0, 0)
    return pl.pallas_call(
        functools.partial(_ffn_kernel, blocks_per_seq=seq // tm),
        grid=(t // tm,),
        in_specs=[
            pl.BlockSpec((tm, D_MODEL), lambda i: (i, 0)),
            pl.BlockSpec((halo, D_MODEL), lambda i: (jnp.maximum(i * r - 1, 0), 0)),
            pl.BlockSpec((halo, D_MODEL), lambda i: (jnp.minimum((i + 1) * r, last), 0)),
            pl.BlockSpec((1, D_MODEL), fixed),
            pl.BlockSpec((D_MODEL, 2 * D_FF), fixed, pipeline_mode=pl.Buffered(1)),
            pl.BlockSpec((3, 2 * D_FF), fixed),
            pl.BlockSpec((1, 2 * D_FF), fixed),
            pl.BlockSpec((D_FF, D_MODEL), fixed, pipeline_mode=pl.Buffered(1)),
        ],
        out_specs=pl.BlockSpec((tm, D_MODEL), lambda i: (i, 0)),
        out_shape=jax.ShapeDtypeStruct((t, D_MODEL), F32),
        scratch_shapes=[pltpu.VMEM((tm + 2 * halo, D_MODEL), BF16),
                        pltpu.VMEM((tm, D_FF), BF16)],
        compiler_params=pltpu.CompilerParams(
            dimension_semantics=("arbitrary",), vmem_limit_bytes=V7X_VMEM_LIMIT),
        name="conv_ffn",
    )(x2, x2, x2, ln_g, wup, conv_w, conv_b, wdn)


def kernel(x, ln_mix_g, w_in, q_norm_g, k_norm_g, lambda_q1, lambda_k1, lambda_q2, lambda_k2,
           subln_g, sg_norm_g, sg_w, sg_b, w_branch_a, w_branch_b, w_out, ln_ffn_g, w_up,
           conv_w, conv_b, w_down):
    batch, seq, d = x.shape
    depth = w_in.shape[0]
    slopes = 2.0 ** (-8.0 * jnp.arange(1, N_HEADS + 1, dtype=F32) / N_HEADS)
    x2 = x.reshape(batch * seq, d)
    for l in range(depth):
        lam_init = jnp.full((1,), 0.8 - 0.6 * math.exp(-0.3 * l), F32)
        q, k, vt, u, sv, ga, gb = _projection(
            x2, ln_mix_g[l][None], w_in[l].astype(BF16),
            jnp.tile(q_norm_g[l], 2 * N_HEADS)[None], jnp.tile(k_norm_g[l], 2 * N_HEADS)[None],
            sg_norm_g[l][None])
        lamv = jnp.stack([lambda_q1[l], lambda_k1[l], lambda_q2[l], lambda_k2[l]])
        oa = _attention(q, k, vt, lamv, subln_g[l][None], q_norm_g[l][None], k_norm_g[l][None],
                        slopes, lam_init, batch, seq)
        sgb_full = jnp.repeat(sg_b[l].T, GROUP_DIM, axis=1)
        x2 = _merge(x2, oa, u, sv, ga, gb, sg_w[l].astype(BF16), sgb_full,
                    w_branch_a[l].astype(BF16), w_branch_b[l].astype(BF16), w_out[l].astype(BF16))
        x2 = _ffn(x2, ln_ffn_g[l][None], w_up[l].astype(BF16), conv_w[l], conv_b[l][None],
                  w_down[l].astype(BF16), seq)
    return x2.reshape(batch, seq, d)
```

```python
import functools
import math

import jax
import jax.numpy as jnp
from jax import lax
from jax.experimental import pallas as pl
from jax.experimental.pallas import tpu as pltpu

F32 = jnp.float32
BF16 = jnp.bfloat16

D_MODEL = 1024
N_HEADS = 8
HEAD_DIM = 64
V_DIM = 2 * HEAD_DIM
CHUNK = 128
N_GROUPS = 8
GROUP_DIM = D_MODEL // N_GROUPS
D_FF = 2816
EPS = 1e-6
N_SECTIONS = 7

V7X_MXU_DIM = 256
V7X_VMEM_LIMIT = 56 * 1024 * 1024

WEIGHT_STAGE_COLS = 512
PROJ_ROWS = 512
PROJ_COLS = 512
ATTN_BLOCK = 256
ATTN_VT_ROWS = V_DIM + 16
ATTN_SLOTS = 4
ATTN_TILES = 4
LOG2E = math.log2(math.e)
ATTN_MAX_SHIFT = 30.0
ATTN_ZERO_LOG2 = 160.0
ATTN_NULL_SCORE = 30000.0
MERGE_ROWS = 512
FFN_ROWS = 1024
FFN_HALO = 16
FFN_COLS = 512


def _rms_rows(x, g):
    return x * lax.rsqrt(jnp.mean(x * x, axis=-1, keepdims=True) + EPS) * g


def _gelu_tanh(x):
    c = math.sqrt(2.0 / math.pi)
    return 0.5 * x * (1.0 + jnp.tanh(c * (x + 0.044715 * (x * x * x))))


def _sigmoid(x):
    return 1.0 / (1.0 + jnp.exp(-x))


def _group_mean_matrix(group):
    r = lax.broadcasted_iota(jnp.int32, (V7X_MXU_DIM, V7X_MXU_DIM), 0) // group
    c = lax.broadcasted_iota(jnp.int32, (V7X_MXU_DIM, V7X_MXU_DIM), 1) // group
    return jnp.where(r == c, 1.0 / group, 0.0).astype(BF16)


def _group_rms_scale(a, mean_mat):
    sq = (a * a).astype(BF16)
    w = mean_mat.shape[0]
    ms = jnp.concatenate([jnp.dot(sq[:, lo:lo + w], mean_mat, preferred_element_type=F32)
                          for lo in range(0, a.shape[1], w)], axis=1)
    return lax.rsqrt(ms + EPS)


def _load_weights_bf16(src_hbm, dst_ref, stage_ref, sem, axis):
    chunk = stage_ref.shape[1 + axis]
    n = src_hbm.shape[axis] // chunk

    def window(c):
        return (slice(None), pl.ds(c * chunk, chunk)) if axis == 1 else (pl.ds(c * chunk, chunk), slice(None))

    def copy(c):
        return pltpu.make_async_copy(src_hbm.at[window(c)], stage_ref.at[c % 2], sem.at[c % 2])

    copy(0).start()
    for c in range(n):
        if c + 1 < n:
            copy(c + 1).start()
        copy(c).wait()
        dst_ref[window(c)] = stage_ref[c % 2].astype(BF16)


def _proj_kernel(x_ref, g_ref, w_hbm, qg_ref, kg_ref, sgg_ref,
                 q_ref, k_ref, v_ref, u_ref, sv_ref, ga_ref, gb_ref, w_ref, stage_ref, sem):
    @pl.when(pl.program_id(0) == 0)
    def _():
        _load_weights_bf16(w_hbm, w_ref, stage_ref, sem, axis=1)

    h = _rms_rows(x_ref[...], g_ref[...]).astype(BF16)
    mean64 = _group_mean_matrix(HEAD_DIM)
    mean128 = _group_mean_matrix(GROUP_DIM)
    w = PROJ_COLS

    def section(s, c):
        lo = s * D_MODEL + c * w
        return jnp.dot(h, w_ref[:, lo:lo + w], preferred_element_type=F32)

    for c in range(D_MODEL // w):
        cols = slice(c * w, (c + 1) * w)
        a = section(0, c)
        q_ref[:, cols] = (a * _group_rms_scale(a, mean64) * (qg_ref[:, cols] * (HEAD_DIM ** -0.5 * LOG2E))).astype(BF16)
        a = section(1, c)
        k_ref[:, cols] = (a * _group_rms_scale(a, mean64) * kg_ref[:, cols]).astype(BF16)
        v_ref[cols, :] = section(2, c).T.astype(BF16)
        u_ref[:, cols] = _gelu_tanh(section(3, c)).astype(BF16)
        a = _gelu_tanh(section(4, c))
        sv_ref[:, cols] = (a * _group_rms_scale(a, mean128) * sgg_ref[:, cols]).astype(BF16)
        ga_ref[:, cols] = _sigmoid(section(5, c)).astype(BF16)
        gb_ref[:, cols] = _sigmoid(section(6, c)).astype(BF16)


def _projection(x2, ln_g, w_in, q_g, k_g, sg_g):
    t = x2.shape[0]
    row = lambda i: (i, 0)
    fixed = lambda i: (0, 0)
    vec = pl.BlockSpec((1, D_MODEL), fixed)
    out = jax.ShapeDtypeStruct((t, D_MODEL), BF16)
    tile = pl.BlockSpec((PROJ_ROWS, D_MODEL), row)
    return pl.pallas_call(
        _proj_kernel,
        grid=(t // PROJ_ROWS,),
        in_specs=[
            pl.BlockSpec((PROJ_ROWS, D_MODEL), row),
            vec,
            pl.BlockSpec(memory_space=pl.ANY),
            vec, vec, vec,
        ],
        out_specs=[tile, tile, pl.BlockSpec((D_MODEL, PROJ_ROWS), lambda i: (0, i)), tile, tile, tile, tile],
        out_shape=[out, out, jax.ShapeDtypeStruct((D_MODEL, t), BF16), out, out, out, out],
        scratch_shapes=[pltpu.VMEM((D_MODEL, N_SECTIONS * D_MODEL), BF16),
                        pltpu.VMEM((2, D_MODEL, WEIGHT_STAGE_COLS), F32),
                        pltpu.SemaphoreType.DMA((2,))],
        compiler_params=pltpu.CompilerParams(
            dimension_semantics=("arbitrary",), vmem_limit_bytes=V7X_VMEM_LIMIT),
        name="proj",
    )(x2, ln_g, w_in, q_g, k_g, sg_g)


def _split3(c):
    hi = c.astype(BF16)
    mid = (c - hi.astype(F32)).astype(BF16)
    lo = (c - hi.astype(F32) - mid.astype(F32)).astype(BF16)
    return hi.astype(F32), mid.astype(F32), lo.astype(F32)


def _pick_rows(row, entries, default=0.0):
    out = default
    for lo, hi, val in reversed(entries):
        out = jnp.where((row >= lo) & (row < hi), val, out)
    return out


def _attn_kernel(slopes_ref, lam_init_ref, q_ref, k_ref, v_ref, lamv_ref, subg_ref, qg_ref, kg_ref, o_ref,
                 k1_ref, k2_ref, vt_ref, dbias_ref, shift_ref, flag_ref, qv_ref, s_ref, p_ref,
                 alpha_ref, m_ref, acc_ref, *, shifted_heads):
    head = pl.program_id(1)
    step = pl.program_id(2)
    blk = ATTN_BLOCK
    seq = k_ref.shape[0]
    nblk = seq // blk
    c = jnp.full((1, 1), slopes_ref[head] * LOG2E, F32)
    c3 = _split3(c)
    c_blk = c * float(blk)

    @pl.when(step == 0)
    def _():
        lane_b = lax.broadcasted_iota(jnp.int32, (blk, V_DIM), 1)
        off = lax.broadcasted_iota(jnp.int32, (blk, V_DIM), 0).astype(F32)
        for mp, ref in enumerate((k1_ref, k2_ref)):
            a = lane_b - (HEAD_DIM if mp == 0 else 0)
            own_b = (lane_b < HEAD_DIM) if mp == 0 else (lane_b >= HEAD_DIM)
            fixed = _pick_rows(a, [(0, 1, -c3[0]), (1, 2, -c3[1]), (2, 3, -c3[2]), (3, 6, off), (9, 12, 1.0)])
            for jb in range(nblk):
                cb3 = _split3(c_blk * float(jb))
                block_lanes = _pick_rows(a[0:1], [(6, 7, cb3[0]), (7, 8, cb3[1]), (8, 9, cb3[2])])
                rows = slice(jb * blk, (jb + 1) * blk)
                ref[rows, :] = jnp.where(own_b, k_ref[rows, :], (fixed + block_lanes).astype(BF16))
        gmax = (jnp.max(jnp.abs(qg_ref[...]), axis=1, keepdims=True)
                * jnp.max(jnp.abs(kg_ref[...]), axis=1, keepdims=True))
        bound = 1.01 * (HEAD_DIM * HEAD_DIM ** -0.5 * LOG2E) * gmax + 0.01
        shift_ref[...] = jnp.broadcast_to(bound, shift_ref.shape)
        flag_ref[0] = (bound[0, 0] <= ATTN_MAX_SHIFT).astype(jnp.int32)
        vt_ref[0:V_DIM, :] = v_ref[...]
        r = lax.broadcasted_iota(jnp.int32, (ATTN_VT_ROWS - V_DIM, seq), 0)
        vt_ref[V_DIM:, :] = jnp.where(r == 0, 1.0, 0.0).astype(BF16)
        d = lax.broadcasted_iota(jnp.int32, (blk, blk), 0) - lax.broadcasted_iota(jnp.int32, (blk, blk), 1)
        dbias_ref[...] = -(c * jnp.abs(d).astype(F32))

    shifted_ok = flag_ref[0] == 1
    tiles = ATTN_TILES
    row = lax.broadcasted_iota(jnp.int32, (HEAD_DIM, blk), 0)
    il = lax.broadcasted_iota(jnp.int32, (HEAD_DIM, blk), 1).astype(F32)
    lamv = lamv_ref[...]
    lam_init = lam_init_ref[0]
    lam = (jnp.exp(jnp.sum(lamv[0:1] * lamv[1:2], axis=-1, keepdims=True))
           - jnp.exp(jnp.sum(lamv[2:3] * lamv[3:4], axis=-1, keepdims=True)) + lam_init)

    def query_rows(t):
        qt = q_ref[t * blk:(t + 1) * blk, :].astype(F32).T
        return qt[0:HEAD_DIM], qt[HEAD_DIM:]

    def with_lanes(q_rows, mp, aug):
        parts = [q_rows[0], aug] if mp == 0 else [aug, q_rows[1]]
        return jnp.concatenate(parts, axis=0).astype(BF16)

    def offset_lanes(sigma):
        return [(0, 3, sigma * il), (3, 4, sigma * c3[0]), (4, 5, sigma * c3[1]), (5, 6, sigma * c3[2])]

    def finish(t, ot):
        ot = ot * lax.rsqrt(jnp.mean(ot * ot, axis=0, keepdims=True) + EPS)
        o_ref[t * blk:(t + 1) * blk, :] = (ot.T * (subg_ref[...] * (1.0 - lam_init))).astype(BF16)

    def shifted_path(dk):
        shift = shift_ref[0:1, 0:1]
        s3 = _split3(-shift)
        qis, diag_qs = [], []
        for t in range(tiles):
            qi = tiles * step + t
            cbi = c_blk * qi.astype(F32)
            q_rows = query_rows(t)
            lo3 = _split3(-cbi - shift)
            hi3 = _split3(cbi - shift)
            diag_q = []
            for mp in range(2):
                qv_ref[t, 0, mp] = with_lanes(q_rows, mp, _pick_rows(row, offset_lanes(1.0) + [
                    (6, 9, 1.0), (9, 10, lo3[0]), (10, 11, lo3[1]), (11, 12, lo3[2])]))
                qv_ref[t, 1, mp] = with_lanes(q_rows, mp, _pick_rows(row, offset_lanes(-1.0) + [
                    (6, 9, -1.0), (9, 10, hi3[0]), (10, 11, hi3[1]), (11, 12, hi3[2])]))
                qv_ref[t, 2, mp] = with_lanes(q_rows, mp, _pick_rows(row, [(9, 10, -ATTN_NULL_SCORE)]))
                diag_q.append(with_lanes(q_rows, mp, _pick_rows(
                    row, [(9, 10, s3[0]), (10, 11, s3[1]), (11, 12, s3[2])])))
            qis.append(qi)
            diag_qs.append(diag_q)

        def visit_block(t, u):
            qi = qis[t]
            if dk is None:
                j = (u - 1) + (u - 1 >= qi).astype(jnp.int32)
                return j, (j > qi).astype(jnp.int32)
            j = qi - dk + (u - 1) if u <= dk else qi + (u - dk)
            valid = (j >= 0) & (j < nblk)
            return jnp.clip(j, 0, nblk - 1), jnp.where(valid, 0 if u <= dk else 1, 2)

        def scores_stage(t, u):
            if u == 0:
                j, qts, bias = qis[t], diag_qs[t], dbias_ref[...]
            else:
                j, var = visit_block(t, u)
                qts, bias = (qv_ref[t, var, 0], qv_ref[t, var, 1]), None
            start = pl.multiple_of(j * blk, blk)
            sums = []
            for mp, kref in enumerate((k1_ref, k2_ref)):
                s = jnp.dot(kref[pl.ds(start, blk), :], qts[mp], preferred_element_type=F32)
                p = jnp.exp2(s if bias is None else s + bias)
                p_ref[t, u, mp] = p.astype(BF16)
                sums.append(jnp.sum(p.reshape(blk // 8, 8, blk), axis=0))
            lsums[t] = sums if lsums[t] is None else [lsums[t][mp] + sums[mp] for mp in range(2)]

        def weighted_values(t, u):
            j = qis[t] if u == 0 else visit_block(t, u)[0]
            vt = vt_ref[0:V_DIM, pl.ds(pl.multiple_of(j * blk, blk), blk)]
            mixed = p_ref[t, u, 0] - ratios[t] * p_ref[t, u, 1]
            pv = jnp.dot(vt, mixed, preferred_element_type=F32)
            accs[t] = pv if accs[t] is None else accs[t] + pv

        visits = nblk if dk is None else 2 * dk + 1
        accs = [None] * tiles
        lsums = [None] * tiles
        for u in range(visits):
            for t in range(tiles):
                scores_stage(t, u)
        inv_l1, ratios = [], []
        for t in range(tiles):
            l1, l2 = [jnp.sum(l, axis=0, keepdims=True) for l in lsums[t]]
            inv_l1.append(1.0 / l1)
            ratios.append(jnp.broadcast_to(lam * l1 / l2, (blk, blk)).astype(BF16))
        for u in range(visits):
            for t in range(tiles):
                weighted_values(t, u)
        for t in range(tiles):
            finish(t, accs[t] * inv_l1[t])

    heads_by_radius = {}
    for h, dk in shifted_heads.items():
        heads_by_radius.setdefault(dk, []).append(h)
    for dk, heads in heads_by_radius.items():
        in_group = functools.reduce(jnp.logical_or, [head == h for h in heads])
        pl.when(shifted_ok & in_group)(functools.partial(shifted_path, dk))

    def general_tile(t):
        qi = tiles * step + t

        def key_block(n):
            return jnp.where(n == 0, qi, n - (n <= qi).astype(jnp.int32))

        def scores_stage(j, qts, slot, bias=None):
            start = pl.multiple_of(j * blk, blk)
            for mp, kref in enumerate((k1_ref, k2_ref)):
                s = jnp.dot(kref[pl.ds(start, blk), :], qts[mp], preferred_element_type=F32)
                s_ref[slot, mp] = s if bias is None else s + bias

        def softmax_stage(j, slot, pslot):
            const = -(c_blk * jnp.abs(qi - j).astype(F32))
            for mp in range(2):
                s = s_ref[slot, mp]
                m_old = m_ref[mp]
                m_new = jnp.maximum(m_old, jnp.max(s, axis=0, keepdims=True) + const)
                p_ref[t, pslot, mp] = jnp.exp2(s - (m_new - const)).astype(BF16)
                alpha_ref[pslot, mp] = jnp.exp2(m_old - m_new)
                m_ref[mp] = m_new

        def values_stage(j, pslot):
            start = pl.multiple_of(j * blk, blk)
            vt = vt_ref[:, pl.ds(start, blk)]
            for mp in range(2):
                pv = jnp.dot(vt, p_ref[t, pslot, mp], preferred_element_type=F32)
                acc_ref[mp] = acc_ref[mp] * alpha_ref[pslot, mp] + pv

        q_rows = query_rows(t)
        for mp in range(2):
            qv_ref[t, 0, mp] = with_lanes(q_rows, mp, _pick_rows(row, offset_lanes(1.0)))
            qv_ref[t, 1, mp] = with_lanes(q_rows, mp, _pick_rows(row, offset_lanes(-1.0)))
        m_ref[...] = jnp.full(m_ref.shape, -jnp.inf, F32)
        acc_ref[...] = jnp.zeros(acc_ref.shape, F32)
        zeros = jnp.zeros((HEAD_DIM, blk), F32)
        scores_stage(qi, (with_lanes(q_rows, 0, zeros), with_lanes(q_rows, 1, zeros)), 0, dbias_ref[...])
        j1 = key_block(1)
        side = (j1 > qi).astype(jnp.int32)
        scores_stage(j1, (qv_ref[t, side, 0], qv_ref[t, side, 1]), 1)
        for n in range(nblk):
            if n > 0:
                values_stage(key_block(n - 1), (n - 1) % 2)
            softmax_stage(key_block(n), n % ATTN_SLOTS, n % 2)
            if n + 2 < nblk:
                ja = key_block(n + 2)
                side = (ja > qi).astype(jnp.int32)
                scores_stage(ja, (qv_ref[t, side, 0], qv_ref[t, side, 1]), (n + 2) % ATTN_SLOTS)
        values_stage(key_block(nblk - 1), (nblk - 1) % 2)
        a1, a2 = acc_ref[0], acc_ref[1]
        finish(t, a1[0:V_DIM] / a1[V_DIM:V_DIM + 1] - lam * (a2[0:V_DIM] / a2[V_DIM:V_DIM + 1]))

    @pl.when(jnp.logical_not(shifted_ok))
    def _():
        for t in range(tiles):
            general_tile(t)


def _attention(q, k, vt, lamv, sub_g, q_g, k_g, slopes, lam_init, batch, seq):
    smem = pl.BlockSpec(memory_space=pltpu.SMEM)
    blk = ATTN_BLOCK
    nq = seq // blk
    tiles = ATTN_TILES
    steps = nq // tiles
    k3 = k.reshape(batch, seq, D_MODEL)
    shifted_heads = {}
    for h in range(N_HEADS):
        slope = 2.0 ** (-8.0 * (h + 1) / N_HEADS)
        dk = math.ceil((ATTN_ZERO_LOG2 / (slope * LOG2E) - 1.0) / blk)
        shifted_heads[h] = dk if 2 * dk + 1 < nq else None
    return pl.pallas_call(
        functools.partial(_attn_kernel, shifted_heads=shifted_heads),
        grid=(batch, N_HEADS, steps),
        in_specs=[
            smem, smem,
            pl.BlockSpec((tiles * blk, V_DIM), lambda b, h, i: (b * steps + i, h)),
            pl.BlockSpec((None, seq, V_DIM), lambda b, h, i: (b, 0, h)),
            pl.BlockSpec((V_DIM, seq), lambda b, h, i: (h, b)),
            pl.BlockSpec((4, HEAD_DIM), lambda b, h, i: (0, 0)),
            pl.BlockSpec((1, V_DIM), lambda b, h, i: (0, 0)),
            pl.BlockSpec((1, HEAD_DIM), lambda b, h, i: (0, 0)),
            pl.BlockSpec((1, HEAD_DIM), lambda b, h, i: (0, 0)),
        ],
        out_specs=pl.BlockSpec((tiles * blk, V_DIM), lambda b, h, i: (b * steps + i, h)),
        out_shape=jax.ShapeDtypeStruct((batch * seq, D_MODEL), BF16),
        scratch_shapes=[
            pltpu.VMEM((seq, V_DIM), BF16),
            pltpu.VMEM((seq, V_DIM), BF16),
            pltpu.VMEM((ATTN_VT_ROWS, seq), BF16),
            pltpu.VMEM((blk, blk), F32),
            pltpu.VMEM((8, 128), F32),
            pltpu.SMEM((1,), jnp.int32),
            pltpu.VMEM((tiles, 3, 2, V_DIM, blk), BF16),
            pltpu.VMEM((ATTN_SLOTS, 2, blk, blk), F32),
            pltpu.VMEM((tiles, nq, 2, blk, blk), BF16),
            pltpu.VMEM((2, 2, 1, blk), F32),
            pltpu.VMEM((2, 1, blk), F32),
            pltpu.VMEM((2, ATTN_VT_ROWS, blk), F32),
        ],
        compiler_params=pltpu.CompilerParams(
            dimension_semantics=("arbitrary", "arbitrary", "arbitrary"),
            vmem_limit_bytes=V7X_VMEM_LIMIT),
        name="diff_attn",
    )(slopes, lam_init, q, k3, vt, lamv, sub_g, q_g, k_g)


def _merge_kernel(x_ref, oa_ref, u_ref, sv_ref, ga_ref, gb_ref, sgw_ref, sgb_ref,
                  wa_ref, wb_ref, wo_ref, o_ref, ob_ref):
    for c in range(MERGE_ROWS // CHUNK):
        rows = slice(c * CHUNK, (c + 1) * CHUNK)
        for g in range(N_GROUPS):
            cols = slice(g * GROUP_DIM, (g + 1) * GROUP_DIM)
            mixed = jnp.dot(sgw_ref[g], sv_ref[rows, cols], preferred_element_type=F32)
            ob_ref[rows, cols] = (u_ref[rows, cols].astype(F32) * (mixed + sgb_ref[:, cols])).astype(BF16)
    ya = jnp.dot(oa_ref[...], wa_ref[...], preferred_element_type=F32)
    yb = jnp.dot(ob_ref[...], wb_ref[...], preferred_element_type=F32)
    merged = (ga_ref[...].astype(F32) * ya + gb_ref[...].astype(F32) * yb).astype(BF16)
    o_ref[...] = x_ref[...] + jnp.dot(merged, wo_ref[...], preferred_element_type=F32)


def _merge(x2, oa, u, sv, ga, gb, sgw, sgb_full, wa, wb, wo):
    t = x2.shape[0]
    row = lambda i: (i, 0)
    fixed2 = lambda i: (0, 0)
    tile = pl.BlockSpec((MERGE_ROWS, D_MODEL), row)
    wspec = pl.BlockSpec((D_MODEL, D_MODEL), fixed2, pipeline_mode=pl.Buffered(1))
    return pl.pallas_call(
        _merge_kernel,
        grid=(t // MERGE_ROWS,),
        in_specs=[
            tile, tile, tile, tile, tile, tile,
            pl.BlockSpec((N_GROUPS, CHUNK, CHUNK), lambda i: (0, 0, 0)),
            pl.BlockSpec((CHUNK, D_MODEL), fixed2),
            wspec, wspec, wspec,
        ],
        out_specs=tile,
        out_shape=jax.ShapeDtypeStruct((t, D_MODEL), F32),
        scratch_shapes=[pltpu.VMEM((MERGE_ROWS, D_MODEL), BF16)],
        compiler_params=pltpu.CompilerParams(
            dimension_semantics=("arbitrary",), vmem_limit_bytes=V7X_VMEM_LIMIT),
        name="sgu_merge",
    )(x2, oa, u, sv, ga, gb, sgw, sgb_full, wa, wb, wo)


def _ffn_kernel(x_ref, prev_ref, next_ref, g_ref, wup_hbm, cw_ref, cb_ref, wdn_hbm,
                o_ref, h_ref, act_ref, wup_ref, wdn_ref, up_stage_ref, dn_stage_ref, sem,
                *, blocks_per_seq):
    i = pl.program_id(0)

    @pl.when(i == 0)
    def _():
        _load_weights_bf16(wup_hbm, wup_ref, up_stage_ref, sem, axis=1)
        _load_weights_bf16(wdn_hbm, wdn_ref, dn_stage_ref, sem, axis=0)

    tm, halo = FFN_ROWS, FFN_HALO
    g = g_ref[...]
    x = x_ref[...]
    keep_prev = (i % blocks_per_seq != 0).astype(F32)
    keep_next = ((i + 1) % blocks_per_seq != 0).astype(F32)
    h_ref[0:halo, :] = (_rms_rows(prev_ref[...], g) * keep_prev).astype(BF16)
    h_ref[halo:halo + tm, :] = _rms_rows(x, g).astype(BF16)
    h_ref[halo + tm:, :] = (_rms_rows(next_ref[...], g) * keep_next).astype(BF16)
    h = h_ref[...]
    ext = tm + 2 * halo

    def conv(cols):
        up = jnp.dot(h, wup_ref[:, cols], preferred_element_type=F32)
        y = (pltpu.roll(up, 1, 0) * cw_ref[0:1, cols] + up * cw_ref[1:2, cols]
             + pltpu.roll(up, ext - 1, 0) * cw_ref[2:3, cols])
        return y[halo:halo + tm] + cb_ref[:, cols]

    for lo in range(0, D_FF, FFN_COLS):
        hi = min(lo + FFN_COLS, D_FF)
        gate = conv(slice(lo, hi))
        val = conv(slice(D_FF + lo, D_FF + hi))
        act_ref[:, lo:hi] = (gate * _sigmoid(gate) * val).astype(BF16)
    o_ref[...] = x + jnp.dot(act_ref[...], wdn_ref[...], preferred_element_type=F32)


def _ffn(x2, ln_g, wup, conv_w, conv_b, wdn, seq):
    t = x2.shape[0]
    tm, halo = FFN_ROWS, FFN_HALO
    r = tm // halo
    last = t // halo - 1
    fixed = lambda i: (0, 0)
    return pl.pallas_call(
        functools.partial(_ffn_kernel, blocks_per_seq=seq // tm),
        grid=(t // tm,),
        in_specs=[
            pl.BlockSpec((tm, D_MODEL), lambda i: (i, 0)),
            pl.BlockSpec((halo, D_MODEL), lambda i: (jnp.maximum(i * r - 1, 0), 0)),
            pl.BlockSpec((halo, D_MODEL), lambda i: (jnp.minimum((i + 1) * r, last), 0)),
            pl.BlockSpec((1, D_MODEL), fixed),
            pl.BlockSpec(memory_space=pl.ANY),
            pl.BlockSpec((3, 2 * D_FF), fixed),
            pl.BlockSpec((1, 2 * D_FF), fixed),
            pl.BlockSpec(memory_space=pl.ANY),
        ],
        out_specs=pl.BlockSpec((tm, D_MODEL), lambda i: (i, 0)),
        out_shape=jax.ShapeDtypeStruct((t, D_MODEL), F32),
        scratch_shapes=[pltpu.VMEM((tm + 2 * halo, D_MODEL), BF16),
                        pltpu.VMEM((tm, D_FF), BF16),
                        pltpu.VMEM((D_MODEL, 2 * D_FF), BF16),
                        pltpu.VMEM((D_FF, D_MODEL), BF16),
                        pltpu.VMEM((2, D_MODEL, WEIGHT_STAGE_COLS), F32),
                        pltpu.VMEM((2, WEIGHT_STAGE_COLS // 2, D_MODEL), F32),
                        pltpu.SemaphoreType.DMA((2,))],
        compiler_params=pltpu.CompilerParams(
            dimension_semantics=("arbitrary",), vmem_limit_bytes=V7X_VMEM_LIMIT),
        name="conv_ffn",
    )(x2, x2, x2, ln_g, wup, conv_w, conv_b, wdn)


def kernel(x, ln_mix_g, w_in, q_norm_g, k_norm_g, lambda_q1, lambda_k1, lambda_q2, lambda_k2,
           subln_g, sg_norm_g, sg_w, sg_b, w_branch_a, w_branch_b, w_out, ln_ffn_g, w_up,
           conv_w, conv_b, w_down):
    batch, seq, d = x.shape
    depth = w_in.shape[0]
    slopes = 2.0 ** (-8.0 * jnp.arange(1, N_HEADS + 1, dtype=F32) / N_HEADS)
    x2 = x.reshape(batch * seq, d)
    for l in range(depth):
        lam_init = jnp.full((1,), 0.8 - 0.6 * math.exp(-0.3 * l), F32)
        q, k, vt, u, sv, ga, gb = _projection(
            x2, ln_mix_g[l][None], w_in[l],
            jnp.tile(q_norm_g[l], 2 * N_HEADS)[None], jnp.tile(k_norm_g[l], 2 * N_HEADS)[None],
            sg_norm_g[l][None])
        lamv = jnp.stack([lambda_q1[l], lambda_k1[l], lambda_q2[l], lambda_k2[l]])
        oa = _attention(q, k, vt, lamv, subln_g[l][None], q_norm_g[l][None], k_norm_g[l][None],
                        slopes, lam_init, batch, seq)
        sgb_full = jnp.repeat(sg_b[l].T, GROUP_DIM, axis=1)
        x2 = _merge(x2, oa, u, sv, ga, gb, sg_w[l].astype(BF16), sgb_full,
                    w_branch_a[l].astype(BF16), w_branch_b[l].astype(BF16), w_out[l].astype(BF16))
        x2 = _ffn(x2, ln_ffn_g[l][None], w_up[l], conv_w[l], conv_b[l][None], w_down[l], seq)
    return x2.reshape(batch, seq, d)
```

```python
import functools
import math

import jax
import jax.numpy as jnp
from jax import lax
from jax.experimental import pallas as pl
from jax.experimental.pallas import tpu as pltpu

F32 = jnp.float32
BF16 = jnp.bfloat16

D_MODEL = 1024
N_HEADS = 8
HEAD_DIM = 64
V_DIM = 2 * HEAD_DIM
CHUNK = 128
N_GROUPS = 8
GROUP_DIM = D_MODEL // N_GROUPS
D_FF = 2816
EPS = 1e-6
N_SECTIONS = 7

V7X_MXU_DIM = 256
V7X_VMEM_LIMIT = 56 * 1024 * 1024

WEIGHT_STAGE_COLS = 512
PROJ_ROWS = 512
PROJ_COLS = 512
ATTN_BLOCK = 256
ATTN_VT_ROWS = V_DIM + 16
ATTN_SLOTS = 4
ATTN_TILES = 4
LOG2E = math.log2(math.e)
ATTN_MAX_SHIFT = 30.0
ATTN_ZERO_LOG2 = 160.0
ATTN_NULL_SCORE = 30000.0
MERGE_ROWS = 512
FFN_ROWS = 1024
FFN_HALO = 16
FFN_COLS = 512


def _rms_rows(x, g):
    return x * lax.rsqrt(jnp.mean(x * x, axis=-1, keepdims=True) + EPS) * g


def _gelu_tanh(x):
    c = math.sqrt(2.0 / math.pi)
    return 0.5 * x * (1.0 + jnp.tanh(c * (x + 0.044715 * (x * x * x))))


def _sigmoid(x):
    return 1.0 / (1.0 + jnp.exp(-x))


def _group_mean_matrix(group):
    r = lax.broadcasted_iota(jnp.int32, (V7X_MXU_DIM, V7X_MXU_DIM), 0) // group
    c = lax.broadcasted_iota(jnp.int32, (V7X_MXU_DIM, V7X_MXU_DIM), 1) // group
    return jnp.where(r == c, 1.0 / group, 0.0).astype(BF16)


def _group_rms_scale(a, mean_mat):
    sq = (a * a).astype(BF16)
    w = mean_mat.shape[0]
    ms = jnp.concatenate([jnp.dot(sq[:, lo:lo + w], mean_mat, preferred_element_type=F32)
                          for lo in range(0, a.shape[1], w)], axis=1)
    return lax.rsqrt(ms + EPS)


def _load_weights_bf16(src_hbm, dst_ref, stage_ref, sem, axis):
    chunk = stage_ref.shape[1 + axis]
    n = src_hbm.shape[axis] // chunk

    def window(c):
        return (slice(None), pl.ds(c * chunk, chunk)) if axis == 1 else (pl.ds(c * chunk, chunk), slice(None))

    def copy(c):
        return pltpu.make_async_copy(src_hbm.at[window(c)], stage_ref.at[c % 2], sem.at[c % 2])

    copy(0).start()
    for c in range(n):
        if c + 1 < n:
            copy(c + 1).start()
        copy(c).wait()
        dst_ref[window(c)] = stage_ref[c % 2].astype(BF16)


def _proj_kernel(x_ref, g_ref, w_hbm, qg_ref, kg_ref, sgg_ref,
                 q_ref, k_ref, v_ref, u_ref, sv_ref, ga_ref, gb_ref, w_ref, stage_ref, sem):
    @pl.when(pl.program_id(0) == 0)
    def _():
        _load_weights_bf16(w_hbm, w_ref, stage_ref, sem, axis=1)

    h = _rms_rows(x_ref[...], g_ref[...]).astype(BF16)
    mean64 = _group_mean_matrix(HEAD_DIM)
    mean128 = _group_mean_matrix(GROUP_DIM)
    w = PROJ_COLS

    def section(s, c):
        lo = s * D_MODEL + c * w
        return jnp.dot(h, w_ref[:, lo:lo + w], preferred_element_type=F32)

    for c in range(D_MODEL // w):
        cols = slice(c * w, (c + 1) * w)
        a = section(0, c)
        q_ref[:, cols] = (a * _group_rms_scale(a, mean64) * (qg_ref[:, cols] * (HEAD_DIM ** -0.5 * LOG2E))).astype(BF16)
        a = section(1, c)
        k_ref[:, cols] = (a * _group_rms_scale(a, mean64) * kg_ref[:, cols]).astype(BF16)
        v_ref[cols, :] = section(2, c).T.astype(BF16)
        u_ref[:, cols] = _gelu_tanh(section(3, c)).astype(BF16)
        a = _gelu_tanh(section(4, c))
        sv_ref[:, cols] = (a * _group_rms_scale(a, mean128) * sgg_ref[:, cols]).astype(BF16)
        ga_ref[:, cols] = _sigmoid(section(5, c)).astype(BF16)
        gb_ref[:, cols] = _sigmoid(section(6, c)).astype(BF16)


def _projection(x2, ln_g, w_in, q_g, k_g, sg_g):
    t = x2.shape[0]
    row = lambda i: (i, 0)
    fixed = lambda i: (0, 0)
    vec = pl.BlockSpec((1, D_MODEL), fixed)
    out = jax.ShapeDtypeStruct((t, D_MODEL), BF16)
    tile = pl.BlockSpec((PROJ_ROWS, D_MODEL), row)
    return pl.pallas_call(
        _proj_kernel,
        grid=(t // PROJ_ROWS,),
        in_specs=[
            pl.BlockSpec((PROJ_ROWS, D_MODEL), row),
            vec,
            pl.BlockSpec(memory_space=pl.ANY),
            vec, vec, vec,
        ],
        out_specs=[tile, tile, pl.BlockSpec((D_MODEL, PROJ_ROWS), lambda i: (0, i)), tile, tile, tile, tile],
        out_shape=[out, out, jax.ShapeDtypeStruct((D_MODEL, t), BF16), out, out, out, out],
        scratch_shapes=[pltpu.VMEM((D_MODEL, N_SECTIONS * D_MODEL), BF16),
                        pltpu.VMEM((2, D_MODEL, WEIGHT_STAGE_COLS), F32),
                        pltpu.SemaphoreType.DMA((2,))],
        compiler_params=pltpu.CompilerParams(
            dimension_semantics=("arbitrary",), vmem_limit_bytes=V7X_VMEM_LIMIT),
        name="proj",
    )(x2, ln_g, w_in, q_g, k_g, sg_g)


def _split3(c):
    hi = c.astype(BF16)
    mid = (c - hi.astype(F32)).astype(BF16)
    lo = (c - hi.astype(F32) - mid.astype(F32)).astype(BF16)
    return hi.astype(F32), mid.astype(F32), lo.astype(F32)


def _pick_rows(row, entries, default=0.0):
    out = default
    for lo, hi, val in reversed(entries):
        out = jnp.where((row >= lo) & (row < hi), val, out)
    return out


def _attn_kernel(slopes_ref, lam_init_ref, q_ref, k_ref, v_ref, lamv_ref, subg_ref, qg_ref, kg_ref, o_ref,
                 k1_ref, k2_ref, vt_ref, dbias_ref, shift_ref, flag_ref, qv_ref, s_ref, p_ref,
                 alpha_ref, m_ref, acc_ref, *, shifted_heads):
    head = pl.program_id(1)
    step = pl.program_id(2)
    blk = ATTN_BLOCK
    seq = k_ref.shape[0]
    nblk = seq // blk
    c = jnp.full((1, 1), slopes_ref[head] * LOG2E, F32)
    c3 = _split3(c)
    c_blk = c * float(blk)

    @pl.when(step == 0)
    def _():
        lane_b = lax.broadcasted_iota(jnp.int32, (blk, V_DIM), 1)
        off = lax.broadcasted_iota(jnp.int32, (blk, V_DIM), 0).astype(F32)
        for mp, ref in enumerate((k1_ref, k2_ref)):
            a = lane_b - (HEAD_DIM if mp == 0 else 0)
            own_b = (lane_b < HEAD_DIM) if mp == 0 else (lane_b >= HEAD_DIM)
            fixed = _pick_rows(a, [(0, 1, -c3[0]), (1, 2, -c3[1]), (2, 3, -c3[2]), (3, 6, off), (9, 12, 1.0)])
            for jb in range(nblk):
                cb3 = _split3(c_blk * float(jb))
                block_lanes = _pick_rows(a[0:1], [(6, 7, cb3[0]), (7, 8, cb3[1]), (8, 9, cb3[2])])
                rows = slice(jb * blk, (jb + 1) * blk)
                ref[rows, :] = jnp.where(own_b, k_ref[rows, :], (fixed + block_lanes).astype(BF16))
        gmax = (jnp.max(jnp.abs(qg_ref[...]), axis=1, keepdims=True)
                * jnp.max(jnp.abs(kg_ref[...]), axis=1, keepdims=True))
        bound = 1.01 * (HEAD_DIM * HEAD_DIM ** -0.5 * LOG2E) * gmax + 0.01
        shift_ref[...] = jnp.broadcast_to(bound, shift_ref.shape)
        flag_ref[0] = (bound[0, 0] <= ATTN_MAX_SHIFT).astype(jnp.int32)
        vt_ref[0:V_DIM, :] = v_ref[...]
        r = lax.broadcasted_iota(jnp.int32, (ATTN_VT_ROWS - V_DIM, seq), 0)
        vt_ref[V_DIM:, :] = jnp.where(r == 0, 1.0, 0.0).astype(BF16)
        d = lax.broadcasted_iota(jnp.int32, (blk, blk), 0) - lax.broadcasted_iota(jnp.int32, (blk, blk), 1)
        dbias_ref[...] = -(c * jnp.abs(d).astype(F32))

    shifted_ok = flag_ref[0] == 1
    tiles = ATTN_TILES
    row = lax.broadcasted_iota(jnp.int32, (HEAD_DIM, blk), 0)
    il = lax.broadcasted_iota(jnp.int32, (HEAD_DIM, blk), 1).astype(F32)
    lamv = lamv_ref[...]
    lam_init = lam_init_ref[0]
    lam = (jnp.exp(jnp.sum(lamv[0:1] * lamv[1:2], axis=-1, keepdims=True))
           - jnp.exp(jnp.sum(lamv[2:3] * lamv[3:4], axis=-1, keepdims=True)) + lam_init)

    def query_rows(t):
        qt = q_ref[t * blk:(t + 1) * blk, :].astype(F32).T
        return qt[0:HEAD_DIM], qt[HEAD_DIM:]

    def with_lanes(q_rows, mp, aug):
        parts = [q_rows[0], aug] if mp == 0 else [aug, q_rows[1]]
        return jnp.concatenate(parts, axis=0).astype(BF16)

    def offset_lanes(sigma):
        return [(0, 3, sigma * il), (3, 4, sigma * c3[0]), (4, 5, sigma * c3[1]), (5, 6, sigma * c3[2])]

    def finish(t, ot):
        ot = ot * lax.rsqrt(jnp.mean(ot * ot, axis=0, keepdims=True) + EPS)
        o_ref[t * blk:(t + 1) * blk, :] = (ot.T * (subg_ref[...] * (1.0 - lam_init))).astype(BF16)

    def shifted_path(dk):
        shift = shift_ref[0:1, 0:1]
        s3 = _split3(-shift)
        qis, diag_qs = [], []
        for t in range(tiles):
            qi = tiles * step + t
            cbi = c_blk * qi.astype(F32)
            q_rows = query_rows(t)
            lo3 = _split3(-cbi - shift)
            hi3 = _split3(cbi - shift)
            diag_q = []
            for mp in range(2):
                qv_ref[t, 0, mp] = with_lanes(q_rows, mp, _pick_rows(row, offset_lanes(1.0) + [
                    (6, 9, 1.0), (9, 10, lo3[0]), (10, 11, lo3[1]), (11, 12, lo3[2])]))
                qv_ref[t, 1, mp] = with_lanes(q_rows, mp, _pick_rows(row, offset_lanes(-1.0) + [
                    (6, 9, -1.0), (9, 10, hi3[0]), (10, 11, hi3[1]), (11, 12, hi3[2])]))
                qv_ref[t, 2, mp] = with_lanes(q_rows, mp, _pick_rows(row, [(9, 10, -ATTN_NULL_SCORE)]))
                diag_q.append(with_lanes(q_rows, mp, _pick_rows(
                    row, [(9, 10, s3[0]), (10, 11, s3[1]), (11, 12, s3[2])])))
            qis.append(qi)
            diag_qs.append(diag_q)

        def visit_block(t, u):
            qi = qis[t]
            if dk is None:
                j = (u - 1) + (u - 1 >= qi).astype(jnp.int32)
                return j, (j > qi).astype(jnp.int32)
            j = qi - dk + (u - 1) if u <= dk else qi + (u - dk)
            valid = (j >= 0) & (j < nblk)
            return jnp.clip(j, 0, nblk - 1), jnp.where(valid, 0 if u <= dk else 1, 2)

        def scores_stage(t, u):
            if u == 0:
                j, qts, bias = qis[t], diag_qs[t], dbias_ref[...]
            else:
                j, var = visit_block(t, u)
                qts, bias = (qv_ref[t, var, 0], qv_ref[t, var, 1]), None
            start = pl.multiple_of(j * blk, blk)
            sums = []
            for mp, kref in enumerate((k1_ref, k2_ref)):
                s = jnp.dot(kref[pl.ds(start, blk), :], qts[mp], preferred_element_type=F32)
                p = jnp.exp2(s if bias is None else s + bias)
                p_ref[t, u, mp] = p.astype(BF16)
                sums.append(jnp.sum(p.reshape(blk // 8, 8, blk), axis=0))
            lsums[t] = sums if lsums[t] is None else [lsums[t][mp] + sums[mp] for mp in range(2)]

        def weighted_values(t, u):
            j = qis[t] if u == 0 else visit_block(t, u)[0]
            vt = vt_ref[0:V_DIM, pl.ds(pl.multiple_of(j * blk, blk), blk)]
            mixed = p_ref[t, u, 0] - ratios[t] * p_ref[t, u, 1]
            pv = jnp.dot(vt, mixed, preferred_element_type=F32)
            accs[t] = pv if accs[t] is None else accs[t] + pv

        visits = nblk if dk is None else 2 * dk + 1
        accs = [None] * tiles
        lsums = [None] * tiles
        inv_l1, ratios = [None] * tiles, [None] * tiles

        def row_sums_done(t):
            l1, l2 = [jnp.sum(l, axis=0, keepdims=True) for l in lsums[t]]
            inv_l1[t] = 1.0 / l1
            ratios[t] = jnp.broadcast_to(lam * l1 / l2, (blk, blk)).astype(BF16)

        first, second = range(tiles // 2), range(tiles // 2, tiles)
        for u in range(visits):
            for t in first:
                scores_stage(t, u)
        for t in first:
            row_sums_done(t)
        for u in range(visits):
            for t in second:
                scores_stage(t, u)
            for t in first:
                weighted_values(t, u)
        for t in first:
            finish(t, accs[t] * inv_l1[t])
        for t in second:
            row_sums_done(t)
        for u in range(visits):
            for t in second:
                weighted_values(t, u)
        for t in second:
            finish(t, accs[t] * inv_l1[t])

    heads_by_radius = {}
    for h, dk in shifted_heads.items():
        heads_by_radius.setdefault(dk, []).append(h)
    for dk, heads in heads_by_radius.items():
        in_group = functools.reduce(jnp.logical_or, [head == h for h in heads])
        pl.when(shifted_ok & in_group)(functools.partial(shifted_path, dk))

    def general_tile(t):
        qi = tiles * step + t

        def key_block(n):
            return jnp.where(n == 0, qi, n - (n <= qi).astype(jnp.int32))

        def scores_stage(j, qts, slot, bias=None):
            start = pl.multiple_of(j * blk, blk)
            for mp, kref in enumerate((k1_ref, k2_ref)):
                s = jnp.dot(kref[pl.ds(start, blk), :], qts[mp], preferred_element_type=F32)
                s_ref[slot, mp] = s if bias is None else s + bias

        def softmax_stage(j, slot, pslot):
            const = -(c_blk * jnp.abs(qi - j).astype(F32))
            for mp in range(2):
                s = s_ref[slot, mp]
                m_old = m_ref[mp]
                m_new = jnp.maximum(m_old, jnp.max(s, axis=0, keepdims=True) + const)
                p_ref[t, pslot, mp] = jnp.exp2(s - (m_new - const)).astype(BF16)
                alpha_ref[pslot, mp] = jnp.exp2(m_old - m_new)
                m_ref[mp] = m_new

        def values_stage(j, pslot):
            start = pl.multiple_of(j * blk, blk)
            vt = vt_ref[:, pl.ds(start, blk)]
            for mp in range(2):
                pv = jnp.dot(vt, p_ref[t, pslot, mp], preferred_element_type=F32)
                acc_ref[mp] = acc_ref[mp] * alpha_ref[pslot, mp] + pv

        q_rows = query_rows(t)
        for mp in range(2):
            qv_ref[t, 0, mp] = with_lanes(q_rows, mp, _pick_rows(row, offset_lanes(1.0)))
            qv_ref[t, 1, mp] = with_lanes(q_rows, mp, _pick_rows(row, offset_lanes(-1.0)))
        m_ref[...] = jnp.full(m_ref.shape, -jnp.inf, F32)
        acc_ref[...] = jnp.zeros(acc_ref.shape, F32)
        zeros = jnp.zeros((HEAD_DIM, blk), F32)
        scores_stage(qi, (with_lanes(q_rows, 0, zeros), with_lanes(q_rows, 1, zeros)), 0, dbias_ref[...])
        j1 = key_block(1)
        side = (j1 > qi).astype(jnp.int32)
        scores_stage(j1, (qv_ref[t, side, 0], qv_ref[t, side, 1]), 1)
        for n in range(nblk):
            if n > 0:
                values_stage(key_block(n - 1), (n - 1) % 2)
            softmax_stage(key_block(n), n % ATTN_SLOTS, n % 2)
            if n + 2 < nblk:
                ja = key_block(n + 2)
                side = (ja > qi).astype(jnp.int32)
                scores_stage(ja, (qv_ref[t, side, 0], qv_ref[t, side, 1]), (n + 2) % ATTN_SLOTS)
        values_stage(key_block(nblk - 1), (nblk - 1) % 2)
        a1, a2 = acc_ref[0], acc_ref[1]
        finish(t, a1[0:V_DIM] / a1[V_DIM:V_DIM + 1] - lam * (a2[0:V_DIM] / a2[V_DIM:V_DIM + 1]))

    @pl.when(jnp.logical_not(shifted_ok))
    def _():
        for t in range(tiles):
            general_tile(t)


def _attention(q, k, vt, lamv, sub_g, q_g, k_g, slopes, lam_init, batch, seq):
    smem = pl.BlockSpec(memory_space=pltpu.SMEM)
    blk = ATTN_BLOCK
    nq = seq // blk
    tiles = ATTN_TILES
    steps = nq // tiles
    k3 = k.reshape(batch, seq, D_MODEL)
    shifted_heads = {}
    for h in range(N_HEADS):
        slope = 2.0 ** (-8.0 * (h + 1) / N_HEADS)
        dk = math.ceil((ATTN_ZERO_LOG2 / (slope * LOG2E) - 1.0) / blk)
        shifted_heads[h] = dk if 2 * dk + 1 < nq else None
    return pl.pallas_call(
        functools.partial(_attn_kernel, shifted_heads=shifted_heads),
        grid=(batch, N_HEADS, steps),
        in_specs=[
            smem, smem,
            pl.BlockSpec((tiles * blk, V_DIM), lambda b, h, i: (b * steps + i, h)),
            pl.BlockSpec((None, seq, V_DIM), lambda b, h, i: (b, 0, h)),
            pl.BlockSpec((V_DIM, seq), lambda b, h, i: (h, b)),
            pl.BlockSpec((4, HEAD_DIM), lambda b, h, i: (0, 0)),
            pl.BlockSpec((1, V_DIM), lambda b, h, i: (0, 0)),
            pl.BlockSpec((1, HEAD_DIM), lambda b, h, i: (0, 0)),
            pl.BlockSpec((1, HEAD_DIM), lambda b, h, i: (0, 0)),
        ],
        out_specs=pl.BlockSpec((tiles * blk, V_DIM), lambda b, h, i: (b * steps + i, h)),
        out_shape=jax.ShapeDtypeStruct((batch * seq, D_MODEL), BF16),
        scratch_shapes=[
            pltpu.VMEM((seq, V_DIM), BF16),
            pltpu.VMEM((seq, V_DIM), BF16),
            pltpu.VMEM((ATTN_VT_ROWS, seq), BF16),
            pltpu.VMEM((blk, blk), F32),
            pltpu.VMEM((8, 128), F32),
            pltpu.SMEM((1,), jnp.int32),
            pltpu.VMEM((tiles, 3, 2, V_DIM, blk), BF16),
            pltpu.VMEM((ATTN_SLOTS, 2, blk, blk), F32),
            pltpu.VMEM((tiles, nq, 2, blk, blk), BF16),
            pltpu.VMEM((2, 2, 1, blk), F32),
            pltpu.VMEM((2, 1, blk), F32),
            pltpu.VMEM((2, ATTN_VT_ROWS, blk), F32),
        ],
        compiler_params=pltpu.CompilerParams(
            dimension_semantics=("arbitrary", "arbitrary", "arbitrary"),
            vmem_limit_bytes=V7X_VMEM_LIMIT),
        name="diff_attn",
    )(slopes, lam_init, q, k3, vt, lamv, sub_g, q_g, k_g)


def _merge_kernel(x_ref, oa_ref, u_ref, sv_ref, ga_ref, gb_ref, sgw_ref, sgb_ref,
                  wa_ref, wb_ref, wo_ref, o_ref, ob_ref):
    for c in range(MERGE_ROWS // CHUNK):
        rows = slice(c * CHUNK, (c + 1) * CHUNK)
        for g in range(N_GROUPS):
            cols = slice(g * GROUP_DIM, (g + 1) * GROUP_DIM)
            mixed = jnp.dot(sgw_ref[g], sv_ref[rows, cols], preferred_element_type=F32)
            ob_ref[rows, cols] = (u_ref[rows, cols].astype(F32) * (mixed + sgb_ref[:, cols])).astype(BF16)
    ya = jnp.dot(oa_ref[...], wa_ref[...], preferred_element_type=F32)
    yb = jnp.dot(ob_ref[...], wb_ref[...], preferred_element_type=F32)
    merged = (ga_ref[...].astype(F32) * ya + gb_ref[...].astype(F32) * yb).astype(BF16)
    o_ref[...] = x_ref[...] + jnp.dot(merged, wo_ref[...], preferred_element_type=F32)


def _merge(x2, oa, u, sv, ga, gb, sgw, sgb_full, wa, wb, wo):
    t = x2.shape[0]
    row = lambda i: (i, 0)
    fixed2 = lambda i: (0, 0)
    tile = pl.BlockSpec((MERGE_ROWS, D_MODEL), row)
    wspec = pl.BlockSpec((D_MODEL, D_MODEL), fixed2, pipeline_mode=pl.Buffered(1))
    return pl.pallas_call(
        _merge_kernel,
        grid=(t // MERGE_ROWS,),
        in_specs=[
            tile, tile, tile, tile, tile, tile,
            pl.BlockSpec((N_GROUPS, CHUNK, CHUNK), lambda i: (0, 0, 0)),
            pl.BlockSpec((CHUNK, D_MODEL), fixed2),
            wspec, wspec, wspec,
        ],
        out_specs=tile,
        out_shape=jax.ShapeDtypeStruct((t, D_MODEL), F32),
        scratch_shapes=[pltpu.VMEM((MERGE_ROWS, D_MODEL), BF16)],
        compiler_params=pltpu.CompilerParams(
            dimension_semantics=("arbitrary",), vmem_limit_bytes=V7X_VMEM_LIMIT),
        name="sgu_merge",
    )(x2, oa, u, sv, ga, gb, sgw, sgb_full, wa, wb, wo)


def _ffn_kernel(x_ref, prev_ref, next_ref, g_ref, wup_hbm, cw_ref, cb_ref, wdn_hbm,
                o_ref, h_ref, act_ref, wup_ref, wdn_ref, up_stage_ref, dn_stage_ref, sem,
                *, blocks_per_seq):
    i = pl.program_id(0)

    @pl.when(i == 0)
    def _():
        _load_weights_bf16(wup_hbm, wup_ref, up_stage_ref, sem, axis=1)
        _load_weights_bf16(wdn_hbm, wdn_ref, dn_stage_ref, sem, axis=0)

    tm, halo = FFN_ROWS, FFN_HALO
    g = g_ref[...]
    x = x_ref[...]
    keep_prev = (i % blocks_per_seq != 0).astype(F32)
    keep_next = ((i + 1) % blocks_per_seq != 0).astype(F32)
    h_ref[0:halo, :] = (_rms_rows(prev_ref[...], g) * keep_prev).astype(BF16)
    h_ref[halo:halo + tm, :] = _rms_rows(x, g).astype(BF16)
    h_ref[halo + tm:, :] = (_rms_rows(next_ref[...], g) * keep_next).astype(BF16)
    h = h_ref[...]
    ext = tm + 2 * halo

    def conv(cols):
        up = jnp.dot(h, wup_ref[:, cols], preferred_element_type=F32)
        y = (pltpu.roll(up, 1, 0) * cw_ref[0:1, cols] + up * cw_ref[1:2, cols]
             + pltpu.roll(up, ext - 1, 0) * cw_ref[2:3, cols])
        return y[halo:halo + tm] + cb_ref[:, cols]

    for lo in range(0, D_FF, FFN_COLS):
        hi = min(lo + FFN_COLS, D_FF)
        gate = conv(slice(lo, hi))
        val = conv(slice(D_FF + lo, D_FF + hi))
        act_ref[:, lo:hi] = (gate * _sigmoid(gate) * val).astype(BF16)
    o_ref[...] = x + jnp.dot(act_ref[...], wdn_ref[...], preferred_element_type=F32)


def _ffn(x2, ln_g, wup, conv_w, conv_b, wdn, seq):
    t = x2.shape[0]
    tm, halo = FFN_ROWS, FFN_HALO
    r = tm // halo
    last = t // halo - 1
    fixed = lambda i: (0, 0)
    return pl.pallas_call(
        functools.partial(_ffn_kernel, blocks_per_seq=seq // tm),
        grid=(t // tm,),
        in_specs=[
            pl.BlockSpec((tm, D_MODEL), lambda i: (i, 0)),
            pl.BlockSpec((halo, D_MODEL), lambda i: (jnp.maximum(i * r - 1, 0), 0)),
            pl.BlockSpec((halo, D_MODEL), lambda i: (jnp.minimum((i + 1) * r, last), 0)),
            pl.BlockSpec((1, D_MODEL), fixed),
            pl.BlockSpec(memory_space=pl.ANY),
            pl.BlockSpec((3, 2 * D_FF), fixed),
            pl.BlockSpec((1, 2 * D_FF), fixed),
            pl.BlockSpec(memory_space=pl.ANY),
        ],
        out_specs=pl.BlockSpec((tm, D_MODEL), lambda i: (i, 0)),
        out_shape=jax.ShapeDtypeStruct((t, D_MODEL), F32),
        scratch_shapes=[pltpu.VMEM((tm + 2 * halo, D_MODEL), BF16),
                        pltpu.VMEM((tm, D_FF), BF16),
                        pltpu.VMEM((D_MODEL, 2 * D_FF), BF16),
                        pltpu.VMEM((D_FF, D_MODEL), BF16),
                        pltpu.VMEM((2, D_MODEL, WEIGHT_STAGE_COLS), F32),
                        pltpu.VMEM((2, WEIGHT_STAGE_COLS // 2, D_MODEL), F32),
                        pltpu.SemaphoreType.DMA((2,))],
        compiler_params=pltpu.CompilerParams(
            dimension_semantics=("arbitrary",), vmem_limit_bytes=V7X_VMEM_LIMIT),
        name="conv_ffn",
    )(x2, x2, x2, ln_g, wup, conv_w, conv_b, wdn)


def kernel(x, ln_mix_g, w_in, q_norm_g, k_norm_g, lambda_q1, lambda_k1, lambda_q2, lambda_k2,
           subln_g, sg_norm_g, sg_w, sg_b, w_branch_a, w_branch_b, w_out, ln_ffn_g, w_up,
           conv_w, conv_b, w_down):
    batch, seq, d = x.shape
    depth = w_in.shape[0]
    slopes = 2.0 ** (-8.0 * jnp.arange(1, N_HEADS + 1, dtype=F32) / N_HEADS)
    x2 = x.reshape(batch * seq, d)
    for l in range(depth):
        lam_init = jnp.full((1,), 0.8 - 0.6 * math.exp(-0.3 * l), F32)
        q, k, vt, u, sv, ga, gb = _projection(
            x2, ln_mix_g[l][None], w_in[l],
            jnp.tile(q_norm_g[l], 2 * N_HEADS)[None], jnp.tile(k_norm_g[l], 2 * N_HEADS)[None],
            sg_norm_g[l][None])
        lamv = jnp.stack([lambda_q1[l], lambda_k1[l], lambda_q2[l], lambda_k2[l]])
        oa = _attention(q, k, vt, lamv, subln_g[l][None], q_norm_g[l][None], k_norm_g[l][None],
                        slopes, lam_init, batch, seq)
        sgb_full = jnp.repeat(sg_b[l].T, GROUP_DIM, axis=1)
        x2 = _merge(x2, oa, u, sv, ga, gb, sg_w[l].astype(BF16), sgb_full,
                    w_branch_a[l].astype(BF16), w_branch_b[l].astype(BF16), w_out[l].astype(BF16))
        x2 = _ffn(x2, ln_ffn_g[l][None], w_up[l], conv_w[l], conv_b[l][None], w_down[l], seq)
    return x2.reshape(batch, seq, d)
```

```python
import functools
import math

import jax
import jax.numpy as jnp
from jax import lax
from jax.experimental import pallas as pl
from jax.experimental.pallas import tpu as pltpu

F32 = jnp.float32
BF16 = jnp.bfloat16

D_MODEL = 1024
N_HEADS = 8
HEAD_DIM = 64
V_DIM = 2 * HEAD_DIM
CHUNK = 128
N_GROUPS = 8
GROUP_DIM = D_MODEL // N_GROUPS
D_FF = 2816
EPS = 1e-6
N_SECTIONS = 7

V7X_MXU_DIM = 256
V7X_VMEM_LIMIT = 56 * 1024 * 1024

WEIGHT_STAGE_COLS = 512
PROJ_ROWS = 512
PROJ_COLS = 512
ATTN_BLOCK = 256
ATTN_VT_ROWS = V_DIM + 16
ATTN_SLOTS = 4
ATTN_TILES = 4
LOG2E = math.log2(math.e)
ATTN_MAX_SHIFT = 30.0
ATTN_ZERO_LOG2 = 160.0
ATTN_NULL_SCORE = 30000.0
MERGE_ROWS = 512
FFN_ROWS = 1024
FFN_HALO = 16
FFN_COLS = 512


def _rms_rows(x, g):
    return x * lax.rsqrt(jnp.mean(x * x, axis=-1, keepdims=True) + EPS) * g


def _gelu_tanh(x):
    c = math.sqrt(2.0 / math.pi)
    return 0.5 * x * (1.0 + jnp.tanh(c * (x + 0.044715 * (x * x * x))))


def _sigmoid(x):
    return 1.0 / (1.0 + jnp.exp(-x))


def _group_mean_matrix(group):
    r = lax.broadcasted_iota(jnp.int32, (V7X_MXU_DIM, V7X_MXU_DIM), 0) // group
    c = lax.broadcasted_iota(jnp.int32, (V7X_MXU_DIM, V7X_MXU_DIM), 1) // group
    return jnp.where(r == c, 1.0 / group, 0.0).astype(BF16)


def _group_rms_scale(a, mean_mat):
    sq = (a * a).astype(BF16)
    w = mean_mat.shape[0]
    ms = jnp.concatenate([jnp.dot(sq[:, lo:lo + w], mean_mat, preferred_element_type=F32)
                          for lo in range(0, a.shape[1], w)], axis=1)
    return lax.rsqrt(ms + EPS)


def _load_weights_bf16(src_hbm, dst_ref, stage_ref, sem, axis):
    chunk = stage_ref.shape[1 + axis]
    n = src_hbm.shape[axis] // chunk

    def window(c):
        return (slice(None), pl.ds(c * chunk, chunk)) if axis == 1 else (pl.ds(c * chunk, chunk), slice(None))

    def copy(c):
        return pltpu.make_async_copy(src_hbm.at[window(c)], stage_ref.at[c % 2], sem.at[c % 2])

    copy(0).start()
    for c in range(n):
        if c + 1 < n:
            copy(c + 1).start()
        copy(c).wait()
        dst_ref[window(c)] = stage_ref[c % 2].astype(BF16)


def _proj_kernel(x_ref, g_ref, w_hbm, qg_ref, kg_ref, sgg_ref,
                 q_ref, k_ref, v_ref, u_ref, sv_ref, ga_ref, gb_ref, w_ref, stage_ref, sem):
    @pl.when(pl.program_id(0) == 0)
    def _():
        _load_weights_bf16(w_hbm, w_ref, stage_ref, sem, axis=1)

    h = _rms_rows(x_ref[...], g_ref[...]).astype(BF16)
    mean64 = _group_mean_matrix(HEAD_DIM)
    mean128 = _group_mean_matrix(GROUP_DIM)
    w = PROJ_COLS

    def section(s, c):
        lo = s * D_MODEL + c * w
        return jnp.dot(h, w_ref[:, lo:lo + w], preferred_element_type=F32)

    for c in range(D_MODEL // w):
        cols = slice(c * w, (c + 1) * w)
        a = section(0, c)
        q_ref[:, cols] = (a * _group_rms_scale(a, mean64) * (qg_ref[:, cols] * (HEAD_DIM ** -0.5 * LOG2E))).astype(BF16)
        a = section(1, c)
        k_ref[:, cols] = (a * _group_rms_scale(a, mean64) * kg_ref[:, cols]).astype(BF16)
        v_ref[cols, :] = section(2, c).T.astype(BF16)
        u_ref[:, cols] = _gelu_tanh(section(3, c)).astype(BF16)
        a = _gelu_tanh(section(4, c))
        sv_ref[:, cols] = (a * _group_rms_scale(a, mean128) * sgg_ref[:, cols]).astype(BF16)
        ga_ref[:, cols] = _sigmoid(section(5, c)).astype(BF16)
        gb_ref[:, cols] = _sigmoid(section(6, c)).astype(BF16)


def _projection(x2, ln_g, w_in, q_g, k_g, sg_g):
    t = x2.shape[0]
    row = lambda i: (i, 0)
    fixed = lambda i: (0, 0)
    vec = pl.BlockSpec((1, D_MODEL), fixed)
    out = jax.ShapeDtypeStruct((t, D_MODEL), BF16)
    tile = pl.BlockSpec((PROJ_ROWS, D_MODEL), row)
    return pl.pallas_call(
        _proj_kernel,
        grid=(t // PROJ_ROWS,),
        in_specs=[
            pl.BlockSpec((PROJ_ROWS, D_MODEL), row),
            vec,
            pl.BlockSpec(memory_space=pl.ANY),
            vec, vec, vec,
        ],
        out_specs=[tile, tile, pl.BlockSpec((D_MODEL, PROJ_ROWS), lambda i: (0, i)), tile, tile, tile, tile],
        out_shape=[out, out, jax.ShapeDtypeStruct((D_MODEL, t), BF16), out, out, out, out],
        scratch_shapes=[pltpu.VMEM((D_MODEL, N_SECTIONS * D_MODEL), BF16),
                        pltpu.VMEM((2, D_MODEL, WEIGHT_STAGE_COLS), F32),
                        pltpu.SemaphoreType.DMA((2,))],
        compiler_params=pltpu.CompilerParams(
            dimension_semantics=("arbitrary",), vmem_limit_bytes=V7X_VMEM_LIMIT),
        name="proj",
    )(x2, ln_g, w_in, q_g, k_g, sg_g)


def _split3(c):
    hi = c.astype(BF16)
    mid = (c - hi.astype(F32)).astype(BF16)
    lo = (c - hi.astype(F32) - mid.astype(F32)).astype(BF16)
    return hi.astype(F32), mid.astype(F32), lo.astype(F32)


def _pick_rows(row, entries, default=0.0):
    out = default
    for lo, hi, val in reversed(entries):
        out = jnp.where((row >= lo) & (row < hi), val, out)
    return out


def _attn_kernel(slopes_ref, lam_init_ref, q_ref, k_ref, v_ref, lamv_ref, subg_ref, qg_ref, kg_ref, o_ref,
                 k1_ref, k2_ref, vt_ref, dbias_ref, shift_ref, flag_ref, qv_ref, s_ref, p_ref,
                 alpha_ref, m_ref, acc_ref, *, shifted_heads):
    head = pl.program_id(1)
    step = pl.program_id(2)
    blk = ATTN_BLOCK
    seq = k_ref.shape[0]
    nblk = seq // blk
    c = jnp.full((1, 1), slopes_ref[head] * LOG2E, F32)
    c3 = _split3(c)
    c_blk = c * float(blk)

    @pl.when(step == 0)
    def _():
        lane_b = lax.broadcasted_iota(jnp.int32, (blk, V_DIM), 1)
        off = lax.broadcasted_iota(jnp.int32, (blk, V_DIM), 0).astype(F32)
        for mp, ref in enumerate((k1_ref, k2_ref)):
            a = lane_b - (HEAD_DIM if mp == 0 else 0)
            own_b = (lane_b < HEAD_DIM) if mp == 0 else (lane_b >= HEAD_DIM)
            fixed = _pick_rows(a, [(0, 1, -c3[0]), (1, 2, -c3[1]), (2, 3, -c3[2]), (3, 6, off), (9, 12, 1.0)])
            for jb in range(nblk):
                cb3 = _split3(c_blk * float(jb))
                block_lanes = _pick_rows(a[0:1], [(6, 7, cb3[0]), (7, 8, cb3[1]), (8, 9, cb3[2])])
                rows = slice(jb * blk, (jb + 1) * blk)
                ref[rows, :] = jnp.where(own_b, k_ref[rows, :], (fixed + block_lanes).astype(BF16))
        gmax = (jnp.max(jnp.abs(qg_ref[...]), axis=1, keepdims=True)
                * jnp.max(jnp.abs(kg_ref[...]), axis=1, keepdims=True))
        bound = 1.01 * (HEAD_DIM * HEAD_DIM ** -0.5 * LOG2E) * gmax + 0.01
        shift_ref[...] = jnp.broadcast_to(bound, shift_ref.shape)
        flag_ref[0] = (bound[0, 0] <= ATTN_MAX_SHIFT).astype(jnp.int32)
        vt_ref[0:V_DIM, :] = v_ref[...]
        r = lax.broadcasted_iota(jnp.int32, (ATTN_VT_ROWS - V_DIM, seq), 0)
        vt_ref[V_DIM:, :] = jnp.where(r == 0, 1.0, 0.0).astype(BF16)
        d = lax.broadcasted_iota(jnp.int32, (blk, blk), 0) - lax.broadcasted_iota(jnp.int32, (blk, blk), 1)
        dbias_ref[...] = -(c * jnp.abs(d).astype(F32))

    shifted_ok = flag_ref[0] == 1
    tiles = ATTN_TILES
    row = lax.broadcasted_iota(jnp.int32, (HEAD_DIM, blk), 0)
    il = lax.broadcasted_iota(jnp.int32, (HEAD_DIM, blk), 1).astype(F32)
    lamv = lamv_ref[...]
    lam_init = lam_init_ref[0]
    lam = (jnp.exp(jnp.sum(lamv[0:1] * lamv[1:2], axis=-1, keepdims=True))
           - jnp.exp(jnp.sum(lamv[2:3] * lamv[3:4], axis=-1, keepdims=True)) + lam_init)

    def query_rows(t):
        qt = q_ref[t * blk:(t + 1) * blk, :].astype(F32).T
        return qt[0:HEAD_DIM], qt[HEAD_DIM:]

    def with_lanes(q_rows, mp, aug):
        parts = [q_rows[0], aug] if mp == 0 else [aug, q_rows[1]]
        return jnp.concatenate(parts, axis=0).astype(BF16)

    def offset_lanes(sigma):
        return [(0, 3, sigma * il), (3, 4, sigma * c3[0]), (4, 5, sigma * c3[1]), (5, 6, sigma * c3[2])]

    def finish(t, ot):
        ot = ot * lax.rsqrt(jnp.mean(ot * ot, axis=0, keepdims=True) + EPS)
        o_ref[t * blk:(t + 1) * blk, :] = (ot.T * (subg_ref[...] * (1.0 - lam_init))).astype(BF16)

    def shifted_path(dk):
        shift = shift_ref[0:1, 0:1]
        s3 = _split3(-shift)
        qis, diag_qs = [], []
        for t in range(tiles):
            qi = tiles * step + t
            cbi = c_blk * qi.astype(F32)
            q_rows = query_rows(t)
            lo3 = _split3(-cbi - shift)
            hi3 = _split3(cbi - shift)
            diag_q = []
            for mp in range(2):
                qv_ref[t, 0, mp] = with_lanes(q_rows, mp, _pick_rows(row, offset_lanes(1.0) + [
                    (6, 9, 1.0), (9, 10, lo3[0]), (10, 11, lo3[1]), (11, 12, lo3[2])]))
                qv_ref[t, 1, mp] = with_lanes(q_rows, mp, _pick_rows(row, offset_lanes(-1.0) + [
                    (6, 9, -1.0), (9, 10, hi3[0]), (10, 11, hi3[1]), (11, 12, hi3[2])]))
                qv_ref[t, 2, mp] = with_lanes(q_rows, mp, _pick_rows(row, [(9, 10, -ATTN_NULL_SCORE)]))
                diag_q.append(with_lanes(q_rows, mp, _pick_rows(
                    row, [(9, 10, s3[0]), (10, 11, s3[1]), (11, 12, s3[2])])))
            qis.append(qi)
            diag_qs.append(diag_q)

        def visit_block(t, u):
            qi = qis[t]
            if dk is None:
                j = (u - 1) + (u - 1 >= qi).astype(jnp.int32)
                return j, (j > qi).astype(jnp.int32)
            j = qi - dk + (u - 1) if u <= dk else qi + (u - dk)
            valid = (j >= 0) & (j < nblk)
            return jnp.clip(j, 0, nblk - 1), jnp.where(valid, 0 if u <= dk else 1, 2)

        def scores_stage(t, u):
            if u == 0:
                j, qts, bias = qis[t], diag_qs[t], dbias_ref[...]
            else:
                j, var = visit_block(t, u)
                qts, bias = (qv_ref[t, var, 0], qv_ref[t, var, 1]), None
            start = pl.multiple_of(j * blk, blk)
            sums = []
            for mp, kref in enumerate((k1_ref, k2_ref)):
                s = jnp.dot(kref[pl.ds(start, blk), :], qts[mp], preferred_element_type=F32)
                p = jnp.exp2(s if bias is None else s + bias)
                p_ref[t, u, mp] = p.astype(BF16)
                sums.append(jnp.sum(p.reshape(blk // 8, 8, blk), axis=0))
            lsums[t] = sums if lsums[t] is None else [lsums[t][mp] + sums[mp] for mp in range(2)]

        def weighted_values(t, u):
            j = qis[t] if u == 0 else visit_block(t, u)[0]
            vt = vt_ref[0:V_DIM, pl.ds(pl.multiple_of(j * blk, blk), blk)]
            mixed = p_ref[t, u, 0] - ratios[t] * p_ref[t, u, 1]
            pv = jnp.dot(vt, mixed, preferred_element_type=F32)
            accs[t] = pv if accs[t] is None else accs[t] + pv

        visits = nblk if dk is None else 2 * dk + 1
        accs = [None] * tiles
        lsums = [None] * tiles
        for u in range(visits):
            for t in range(tiles):
                scores_stage(t, u)
        inv_l1, ratios = [], []
        for t in range(tiles):
            l1, l2 = [jnp.sum(l, axis=0, keepdims=True) for l in lsums[t]]
            inv_l1.append(1.0 / l1)
            ratios.append(jnp.broadcast_to(lam * l1 / l2, (blk, blk)).astype(BF16))
        for u in range(visits):
            for t in range(tiles):
                weighted_values(t, u)
        for t in range(tiles):
            finish(t, accs[t] * inv_l1[t])

    heads_by_radius = {}
    for h, dk in shifted_heads.items():
        heads_by_radius.setdefault(dk, []).append(h)
    for dk, heads in heads_by_radius.items():
        in_group = functools.reduce(jnp.logical_or, [head == h for h in heads])
        pl.when(shifted_ok & in_group)(functools.partial(shifted_path, dk))

    def general_tile(t):
        qi = tiles * step + t

        def key_block(n):
            return jnp.where(n == 0, qi, n - (n <= qi).astype(jnp.int32))

        def scores_stage(j, qts, slot, bias=None):
            start = pl.multiple_of(j * blk, blk)
            for mp, kref in enumerate((k1_ref, k2_ref)):
                s = jnp.dot(kref[pl.ds(start, blk), :], qts[mp], preferred_element_type=F32)
                s_ref[slot, mp] = s if bias is None else s + bias

        def softmax_stage(j, slot, pslot):
            const = -(c_blk * jnp.abs(qi - j).astype(F32))
            for mp in range(2):
                s = s_ref[slot, mp]
                m_old = m_ref[mp]
                m_new = jnp.maximum(m_old, jnp.max(s, axis=0, keepdims=True) + const)
                p_ref[t, pslot, mp] = jnp.exp2(s - (m_new - const)).astype(BF16)
                alpha_ref[pslot, mp] = jnp.exp2(m_old - m_new)
                m_ref[mp] = m_new

        def values_stage(j, pslot):
            start = pl.multiple_of(j * blk, blk)
            vt = vt_ref[:, pl.ds(start, blk)]
            for mp in range(2):
                pv = jnp.dot(vt, p_ref[t, pslot, mp], preferred_element_type=F32)
                acc_ref[mp] = acc_ref[mp] * alpha_ref[pslot, mp] + pv

        q_rows = query_rows(t)
        for mp in range(2):
            qv_ref[t, 0, mp] = with_lanes(q_rows, mp, _pick_rows(row, offset_lanes(1.0)))
            qv_ref[t, 1, mp] = with_lanes(q_rows, mp, _pick_rows(row, offset_lanes(-1.0)))
        m_ref[...] = jnp.full(m_ref.shape, -jnp.inf, F32)
        acc_ref[...] = jnp.zeros(acc_ref.shape, F32)
        zeros = jnp.zeros((HEAD_DIM, blk), F32)
        scores_stage(qi, (with_lanes(q_rows, 0, zeros), with_lanes(q_rows, 1, zeros)), 0, dbias_ref[...])
        j1 = key_block(1)
        side = (j1 > qi).astype(jnp.int32)
        scores_stage(j1, (qv_ref[t, side, 0], qv_ref[t, side, 1]), 1)
        for n in range(nblk):
            if n > 0:
                values_stage(key_block(n - 1), (n - 1) % 2)
            softmax_stage(key_block(n), n % ATTN_SLOTS, n % 2)
            if n + 2 < nblk:
                ja = key_block(n + 2)
                side = (ja > qi).astype(jnp.int32)
                scores_stage(ja, (qv_ref[t, side, 0], qv_ref[t, side, 1]), (n + 2) % ATTN_SLOTS)
        values_stage(key_block(nblk - 1), (nblk - 1) % 2)
        a1, a2 = acc_ref[0], acc_ref[1]
        finish(t, a1[0:V_DIM] / a1[V_DIM:V_DIM + 1] - lam * (a2[0:V_DIM] / a2[V_DIM:V_DIM + 1]))

    @pl.when(jnp.logical_not(shifted_ok))
    def _():
        for t in range(tiles):
            general_tile(t)


def _attention(q, k, vt, lamv, sub_g, q_g, k_g, slopes, lam_init, batch, seq):
    smem = pl.BlockSpec(memory_space=pltpu.SMEM)
    blk = ATTN_BLOCK
    nq = seq // blk
    tiles = ATTN_TILES
    steps = nq // tiles
    k3 = k.reshape(batch, seq, D_MODEL)
    shifted_heads = {}
    for h in range(N_HEADS):
        slope = 2.0 ** (-8.0 * (h + 1) / N_HEADS)
        dk = math.ceil((ATTN_ZERO_LOG2 / (slope * LOG2E) - 1.0) / blk)
        shifted_heads[h] = dk if 2 * dk + 1 < nq else None
    return pl.pallas_call(
        functools.partial(_attn_kernel, shifted_heads=shifted_heads),
        grid=(batch, N_HEADS, steps),
        in_specs=[
            smem, smem,
            pl.BlockSpec((tiles * blk, V_DIM), lambda b, h, i: (b * steps + i, h)),
            pl.BlockSpec((None, seq, V_DIM), lambda b, h, i: (b, 0, h)),
            pl.BlockSpec((V_DIM, seq), lambda b, h, i: (h, b)),
            pl.BlockSpec((4, HEAD_DIM), lambda b, h, i: (0, 0)),
            pl.BlockSpec((1, V_DIM), lambda b, h, i: (0, 0)),
            pl.BlockSpec((1, HEAD_DIM), lambda b, h, i: (0, 0)),
            pl.BlockSpec((1, HEAD_DIM), lambda b, h, i: (0, 0)),
        ],
        out_specs=pl.BlockSpec((tiles * blk, V_DIM), lambda b, h, i: (b * steps + i, h)),
        out_shape=jax.ShapeDtypeStruct((batch * seq, D_MODEL), BF16),
        scratch_shapes=[
            pltpu.VMEM((seq, V_DIM), BF16),
            pltpu.VMEM((seq, V_DIM), BF16),
            pltpu.VMEM((ATTN_VT_ROWS, seq), BF16),
            pltpu.VMEM((blk, blk), F32),
            pltpu.VMEM((8, 128), F32),
            pltpu.SMEM((1,), jnp.int32),
            pltpu.VMEM((tiles, 3, 2, V_DIM, blk), BF16),
            pltpu.VMEM((ATTN_SLOTS, 2, blk, blk), F32),
            pltpu.VMEM((tiles, nq, 2, blk, blk), BF16),
            pltpu.VMEM((2, 2, 1, blk), F32),
            pltpu.VMEM((2, 1, blk), F32),
            pltpu.VMEM((2, ATTN_VT_ROWS, blk), F32),
        ],
        compiler_params=pltpu.CompilerParams(
            dimension_semantics=("arbitrary", "arbitrary", "arbitrary"),
            vmem_limit_bytes=V7X_VMEM_LIMIT),
        name="diff_attn",
    )(slopes, lam_init, q, k3, vt, lamv, sub_g, q_g, k_g)


def _merge_kernel(x_ref, oa_ref, u_ref, sv_ref, ga_ref, gb_ref, sgw_ref, sgb_ref,
                  wa_ref, wb_ref, wo_ref, o_ref, ob_ref):
    for c in range(MERGE_ROWS // CHUNK):
        rows = slice(c * CHUNK, (c + 1) * CHUNK)
        for g in range(N_GROUPS):
            cols = slice(g * GROUP_DIM, (g + 1) * GROUP_DIM)
            mixed = jnp.dot(sgw_ref[g], sv_ref[rows, cols], preferred_element_type=F32)
            ob_ref[rows, cols] = (u_ref[rows, cols].astype(F32) * (mixed + sgb_ref[:, cols])).astype(BF16)
    ya = jnp.dot(oa_ref[...], wa_ref[...], preferred_element_type=F32)
    yb = jnp.dot(ob_ref[...], wb_ref[...], preferred_element_type=F32)
    merged = (ga_ref[...].astype(F32) * ya + gb_ref[...].astype(F32) * yb).astype(BF16)
    o_ref[...] = x_ref[...] + jnp.dot(merged, wo_ref[...], preferred_element_type=F32)


def _merge(x2, oa, u, sv, ga, gb, sgw, sgb_full, wa, wb, wo):
    t = x2.shape[0]
    row = lambda i: (i, 0)
    fixed2 = lambda i: (0, 0)
    tile = pl.BlockSpec((MERGE_ROWS, D_MODEL), row)
    wspec = pl.BlockSpec((D_MODEL, D_MODEL), fixed2, pipeline_mode=pl.Buffered(1))
    return pl.pallas_call(
        _merge_kernel,
        grid=(t // MERGE_ROWS,),
        in_specs=[
            tile, tile, tile, tile, tile, tile,
            pl.BlockSpec((N_GROUPS, CHUNK, CHUNK), lambda i: (0, 0, 0)),
            pl.BlockSpec((CHUNK, D_MODEL), fixed2),
            wspec, wspec, wspec,
        ],
        out_specs=tile,
        out_shape=jax.ShapeDtypeStruct((t, D_MODEL), F32),
        scratch_shapes=[pltpu.VMEM((MERGE_ROWS, D_MODEL), BF16)],
        compiler_params=pltpu.CompilerParams(
            dimension_semantics=("arbitrary",), vmem_limit_bytes=V7X_VMEM_LIMIT),
        name="sgu_merge",
    )(x2, oa, u, sv, ga, gb, sgw, sgb_full, wa, wb, wo)


def _ffn_kernel(x_ref, prev_ref, next_ref, g_ref, wup_hbm, cw_ref, cb_ref, wdn_hbm,
                o_ref, h_ref, act_ref, wup_ref, wdn_ref, up_stage_ref, dn_stage_ref, sem,
                *, blocks_per_seq):
    i = pl.program_id(0)

    @pl.when(i == 0)
    def _():
        _load_weights_bf16(wup_hbm, wup_ref, up_stage_ref, sem, axis=1)
        _load_weights_bf16(wdn_hbm, wdn_ref, dn_stage_ref, sem, axis=0)

    tm, halo = FFN_ROWS, FFN_HALO
    g = g_ref[...]
    x = x_ref[...]
    keep_prev = (i % blocks_per_seq != 0).astype(F32)
    keep_next = ((i + 1) % blocks_per_seq != 0).astype(F32)
    h_ref[0:halo, :] = (_rms_rows(prev_ref[...], g) * keep_prev).astype(BF16)
    h_ref[halo:halo + tm, :] = _rms_rows(x, g).astype(BF16)
    h_ref[halo + tm:, :] = (_rms_rows(next_ref[...], g) * keep_next).astype(BF16)
    h = h_ref[...]
    ext = tm + 2 * halo

    def conv(cols):
        up = jnp.dot(h, wup_ref[:, cols], preferred_element_type=F32)
        y = (pltpu.roll(up, 1, 0) * cw_ref[0:1, cols] + up * cw_ref[1:2, cols]
             + pltpu.roll(up, ext - 1, 0) * cw_ref[2:3, cols])
        return y[halo:halo + tm] + cb_ref[:, cols]

    for lo in range(0, D_FF, FFN_COLS):
        hi = min(lo + FFN_COLS, D_FF)
        gate = conv(slice(lo, hi))
        val = conv(slice(D_FF + lo, D_FF + hi))
        act_ref[:, lo:hi] = (gate * _sigmoid(gate) * val).astype(BF16)
    o_ref[...] = x + jnp.dot(act_ref[...], wdn_ref[...], preferred_element_type=F32)


def _ffn(x2, ln_g, wup, conv_w, conv_b, wdn, seq):
    t = x2.shape[0]
    tm, halo = FFN_ROWS, FFN_HALO
    r = tm // halo
    last = t // halo - 1
    fixed = lambda i: (0, 0)
    return pl.pallas_call(
        functools.partial(_ffn_kernel, blocks_per_seq=seq // tm),
        grid=(t // tm,),
        in_specs=[
            pl.BlockSpec((tm, D_MODEL), lambda i: (i, 0)),
            pl.BlockSpec((halo, D_MODEL), lambda i: (jnp.maximum(i * r - 1, 0), 0)),
            pl.BlockSpec((halo, D_MODEL), lambda i: (jnp.minimum((i + 1) * r, last), 0)),
            pl.BlockSpec((1, D_MODEL), fixed),
            pl.BlockSpec(memory_space=pl.ANY),
            pl.BlockSpec((3, 2 * D_FF), fixed),
            pl.BlockSpec((1, 2 * D_FF), fixed),
            pl.BlockSpec(memory_space=pl.ANY),
        ],
        out_specs=pl.BlockSpec((tm, D_MODEL), lambda i: (i, 0)),
        out_shape=jax.ShapeDtypeStruct((t, D_MODEL), F32),
        scratch_shapes=[pltpu.VMEM((tm + 2 * halo, D_MODEL), BF16),
                        pltpu.VMEM((tm, D_FF), BF16),
                        pltpu.VMEM((D_MODEL, 2 * D_FF), BF16),
                        pltpu.VMEM((D_FF, D_MODEL), BF16),
                        pltpu.VMEM((2, D_MODEL, WEIGHT_STAGE_COLS), F32),
                        pltpu.VMEM((2, WEIGHT_STAGE_COLS // 2, D_MODEL), F32),
                        pltpu.SemaphoreType.DMA((2,))],
        compiler_params=pltpu.CompilerParams(
            dimension_semantics=("arbitrary",), vmem_limit_bytes=V7X_VMEM_LIMIT),
        name="conv_ffn",
    )(x2, x2, x2, ln_g, wup, conv_w, conv_b, wdn)


def kernel(x, ln_mix_g, w_in, q_norm_g, k_norm_g, lambda_q1, lambda_k1, lambda_q2, lambda_k2,
           subln_g, sg_norm_g, sg_w, sg_b, w_branch_a, w_branch_b, w_out, ln_ffn_g, w_up,
           conv_w, conv_b, w_down):
    batch, seq, d = x.shape
    depth = w_in.shape[0]
    slopes = 2.0 ** (-8.0 * jnp.arange(1, N_HEADS + 1, dtype=F32) / N_HEADS)
    x2 = x.reshape(batch * seq, d)
    for l in range(depth):
        lam_init = jnp.full((1,), 0.8 - 0.6 * math.exp(-0.3 * l), F32)
        q, k, vt, u, sv, ga, gb = _projection(
            x2, ln_mix_g[l][None], w_in[l],
            jnp.tile(q_norm_g[l], 2 * N_HEADS)[None], jnp.tile(k_norm_g[l], 2 * N_HEADS)[None],
            sg_norm_g[l][None])
        lamv = jnp.stack([lambda_q1[l], lambda_k1[l], lambda_q2[l], lambda_k2[l]])
        oa = _attention(q, k, vt, lamv, subln_g[l][None], q_norm_g[l][None], k_norm_g[l][None],
                        slopes, lam_init, batch, seq)
        sgb_full = jnp.repeat(sg_b[l].T, GROUP_DIM, axis=1)
        x2 = _merge(x2, oa, u, sv, ga, gb, sg_w[l].astype(BF16), sgb_full,
                    w_branch_a[l].astype(BF16), w_branch_b[l].astype(BF16), w_out[l].astype(BF16))
        x2 = _ffn(x2, ln_ffn_g[l][None], w_up[l], conv_w[l], conv_b[l][None], w_down[l], seq)
    return x2.reshape(batch, seq, d)
```

```python
import functools
import math

import jax
import jax.numpy as jnp
from jax import lax
from jax.experimental import pallas as pl
from jax.experimental.pallas import tpu as pltpu

F32 = jnp.float32
BF16 = jnp.bfloat16

D_MODEL = 1024
N_HEADS = 8
HEAD_DIM = 64
V_DIM = 2 * HEAD_DIM
CHUNK = 128
N_GROUPS = 8
GROUP_DIM = D_MODEL // N_GROUPS
D_FF = 2816
EPS = 1e-6
N_SECTIONS = 7

V7X_MXU_DIM = 256
V7X_VMEM_LIMIT = 56 * 1024 * 1024

WEIGHT_STAGE_COLS = 512
PROJ_ROWS = 512
PROJ_COLS = 512
ATTN_BLOCK = 256
ATTN_VT_ROWS = V_DIM + 16
ATTN_SLOTS = 4
ATTN_TILES = 4
LOG2E = math.log2(math.e)
ATTN_MAX_SHIFT = 30.0
ATTN_ZERO_LOG2 = 160.0
ATTN_NULL_SCORE = 30000.0
MERGE_ROWS = 512
FFN_ROWS = 1024
FFN_HALO = 16
FFN_COLS = 512


def _rms_rows(x, g):
    return x * lax.rsqrt(jnp.mean(x * x, axis=-1, keepdims=True) + EPS) * g


def _gelu_tanh(x):
    c = math.sqrt(2.0 / math.pi)
    return 0.5 * x * (1.0 + jnp.tanh(c * (x + 0.044715 * (x * x * x))))


def _sigmoid(x):
    return 1.0 / (1.0 + jnp.exp(-x))


def _group_mean_matrix(group):
    r = lax.broadcasted_iota(jnp.int32, (V7X_MXU_DIM, V7X_MXU_DIM), 0) // group
    c = lax.broadcasted_iota(jnp.int32, (V7X_MXU_DIM, V7X_MXU_DIM), 1) // group
    return jnp.where(r == c, 1.0 / group, 0.0).astype(BF16)


def _group_rms_scale(a, mean_mat):
    sq = (a * a).astype(BF16)
    w = mean_mat.shape[0]
    ms = jnp.concatenate([jnp.dot(sq[:, lo:lo + w], mean_mat, preferred_element_type=F32)
                          for lo in range(0, a.shape[1], w)], axis=1)
    return lax.rsqrt(ms + EPS)


def _load_weights_bf16(src_hbm, dst_ref, stage_ref, sem, axis):
    chunk = stage_ref.shape[1 + axis]
    n = src_hbm.shape[axis] // chunk

    def window(c):
        return (slice(None), pl.ds(c * chunk, chunk)) if axis == 1 else (pl.ds(c * chunk, chunk), slice(None))

    def copy(c):
        return pltpu.make_async_copy(src_hbm.at[window(c)], stage_ref.at[c % 2], sem.at[c % 2])

    copy(0).start()
    for c in range(n):
        if c + 1 < n:
            copy(c + 1).start()
        copy(c).wait()
        dst_ref[window(c)] = stage_ref[c % 2].astype(BF16)


def _proj_kernel(x_ref, g_ref, w_hbm, qg_ref, kg_ref, sgg_ref,
                 q_ref, k_ref, v_ref, u_ref, sv_ref, ga_ref, gb_ref, w_ref, stage_ref, sem):
    @pl.when(pl.program_id(0) == 0)
    def _():
        _load_weights_bf16(w_hbm, w_ref, stage_ref, sem, axis=1)

    h = _rms_rows(x_ref[...], g_ref[...]).astype(BF16)
    mean64 = _group_mean_matrix(HEAD_DIM)
    mean128 = _group_mean_matrix(GROUP_DIM)
    w = PROJ_COLS

    def section(s, c):
        lo = s * D_MODEL + c * w
        return jnp.dot(h, w_ref[:, lo:lo + w], preferred_element_type=F32)

    for c in range(D_MODEL // w):
        cols = slice(c * w, (c + 1) * w)
        a = section(0, c)
        q_ref[:, cols] = (a * _group_rms_scale(a, mean64) * (qg_ref[:, cols] * (HEAD_DIM ** -0.5 * LOG2E))).astype(BF16)
        a = section(1, c)
        k_ref[:, cols] = (a * _group_rms_scale(a, mean64) * kg_ref[:, cols]).astype(BF16)
        v_ref[cols, :] = section(2, c).T.astype(BF16)
        u_ref[:, cols] = _gelu_tanh(section(3, c)).astype(BF16)
        a = _gelu_tanh(section(4, c))
        sv_ref[:, cols] = (a * _group_rms_scale(a, mean128) * sgg_ref[:, cols]).astype(BF16)
        ga_ref[:, cols] = _sigmoid(section(5, c)).astype(BF16)
        gb_ref[:, cols] = _sigmoid(section(6, c)).astype(BF16)


def _projection(x2, ln_g, w_in, q_g, k_g, sg_g):
    t = x2.shape[0]
    row = lambda i: (i, 0)
    fixed = lambda i: (0, 0)
    vec = pl.BlockSpec((1, D_MODEL), fixed)
    out = jax.ShapeDtypeStruct((t, D_MODEL), BF16)
    tile = pl.BlockSpec((PROJ_ROWS, D_MODEL), row)
    return pl.pallas_call(
        _proj_kernel,
        grid=(t // PROJ_ROWS,),
        in_specs=[
            pl.BlockSpec((PROJ_ROWS, D_MODEL), row),
            vec,
            pl.BlockSpec(memory_space=pl.ANY),
            vec, vec, vec,
        ],
        out_specs=[tile, tile, pl.BlockSpec((D_MODEL, PROJ_ROWS), lambda i: (0, i)), tile, tile, tile, tile],
        out_shape=[out, out, jax.ShapeDtypeStruct((D_MODEL, t), BF16), out, out, out, out],
        scratch_shapes=[pltpu.VMEM((D_MODEL, N_SECTIONS * D_MODEL), BF16),
                        pltpu.VMEM((2, D_MODEL, WEIGHT_STAGE_COLS), F32),
                        pltpu.SemaphoreType.DMA((2,))],
        compiler_params=pltpu.CompilerParams(
            dimension_semantics=("arbitrary",), vmem_limit_bytes=V7X_VMEM_LIMIT),
        name="proj",
    )(x2, ln_g, w_in, q_g, k_g, sg_g)


def _split3(c):
    hi = c.astype(BF16)
    mid = (c - hi.astype(F32)).astype(BF16)
    lo = (c - hi.astype(F32) - mid.astype(F32)).astype(BF16)
    return hi.astype(F32), mid.astype(F32), lo.astype(F32)


def _pick_rows(row, entries, default=0.0):
    out = default
    for lo, hi, val in reversed(entries):
        out = jnp.where((row >= lo) & (row < hi), val, out)
    return out


def _attn_kernel(slopes_ref, lam_init_ref, q_ref, k_ref, v_ref, lamv_ref, subg_ref, qg_ref, kg_ref, o_ref,
                 k1_ref, k2_ref, vt_ref, dbias_ref, shift_ref, flag_ref, qv_ref, s_ref, p_ref,
                 alpha_ref, m_ref, acc_ref, *, shifted_heads):
    head = pl.program_id(1)
    step = pl.program_id(2)
    blk = ATTN_BLOCK
    seq = k_ref.shape[0]
    nblk = seq // blk
    c = jnp.full((1, 1), slopes_ref[head] * LOG2E, F32)
    c3 = _split3(c)
    c_blk = c * float(blk)

    @pl.when(step == 0)
    def _():
        lane_b = lax.broadcasted_iota(jnp.int32, (blk, V_DIM), 1)
        off = lax.broadcasted_iota(jnp.int32, (blk, V_DIM), 0).astype(F32)
        for mp, ref in enumerate((k1_ref, k2_ref)):
            a = lane_b - (HEAD_DIM if mp == 0 else 0)
            own_b = (lane_b < HEAD_DIM) if mp == 0 else (lane_b >= HEAD_DIM)
            fixed = _pick_rows(a, [(0, 1, -c3[0]), (1, 2, -c3[1]), (2, 3, -c3[2]), (3, 6, off), (9, 12, 1.0)])
            for jb in range(nblk):
                cb3 = _split3(c_blk * float(jb))
                block_lanes = _pick_rows(a[0:1], [(6, 7, cb3[0]), (7, 8, cb3[1]), (8, 9, cb3[2])])
                rows = slice(jb * blk, (jb + 1) * blk)
                ref[rows, :] = jnp.where(own_b, k_ref[rows, :], (fixed + block_lanes).astype(BF16))
        gmax = (jnp.max(jnp.abs(qg_ref[...]), axis=1, keepdims=True)
                * jnp.max(jnp.abs(kg_ref[...]), axis=1, keepdims=True))
        bound = 1.01 * (HEAD_DIM * HEAD_DIM ** -0.5 * LOG2E) * gmax + 0.01
        shift_ref[...] = jnp.broadcast_to(bound, shift_ref.shape)
        flag_ref[0] = (bound[0, 0] <= ATTN_MAX_SHIFT).astype(jnp.int32)
        vt_ref[0:V_DIM, :] = v_ref[...]
        r = lax.broadcasted_iota(jnp.int32, (ATTN_VT_ROWS - V_DIM, seq), 0)
        vt_ref[V_DIM:, :] = jnp.where(r == 0, 1.0, 0.0).astype(BF16)
        d = lax.broadcasted_iota(jnp.int32, (blk, blk), 0) - lax.broadcasted_iota(jnp.int32, (blk, blk), 1)
        dbias_ref[...] = -(c * jnp.abs(d).astype(F32))

    shifted_ok = flag_ref[0] == 1
    tiles = ATTN_TILES
    row = lax.broadcasted_iota(jnp.int32, (HEAD_DIM, blk), 0)
    il = lax.broadcasted_iota(jnp.int32, (HEAD_DIM, blk), 1).astype(F32)
    lamv = lamv_ref[...]
    lam_init = lam_init_ref[0]
    lam = (jnp.exp(jnp.sum(lamv[0:1] * lamv[1:2], axis=-1, keepdims=True))
           - jnp.exp(jnp.sum(lamv[2:3] * lamv[3:4], axis=-1, keepdims=True)) + lam_init)

    def query_rows(t):
        qt = q_ref[t * blk:(t + 1) * blk, :].astype(F32).T
        return qt[0:HEAD_DIM], qt[HEAD_DIM:]

    def with_lanes(q_rows, mp, aug):
        parts = [q_rows[0], aug] if mp == 0 else [aug, q_rows[1]]
        return jnp.concatenate(parts, axis=0).astype(BF16)

    def offset_lanes(sigma):
        return [(0, 3, sigma * il), (3, 4, sigma * c3[0]), (4, 5, sigma * c3[1]), (5, 6, sigma * c3[2])]

    def finish(t, ot):
        ot = ot * lax.rsqrt(jnp.mean(ot * ot, axis=0, keepdims=True) + EPS)
        o_ref[t * blk:(t + 1) * blk, :] = (ot.T * (subg_ref[...] * (1.0 - lam_init))).astype(BF16)

    def shifted_path(dk):
        shift = shift_ref[0:1, 0:1]
        s3 = _split3(-shift)
        qis, diag_qs = [], []
        for t in range(tiles):
            qi = tiles * step + t
            cbi = c_blk * qi.astype(F32)
            q_rows = query_rows(t)
            lo3 = _split3(-cbi - shift)
            hi3 = _split3(cbi - shift)
            diag_q = []
            for mp in range(2):
                qv_ref[t, 0, mp] = with_lanes(q_rows, mp, _pick_rows(row, offset_lanes(1.0) + [
                    (6, 9, 1.0), (9, 10, lo3[0]), (10, 11, lo3[1]), (11, 12, lo3[2])]))
                qv_ref[t, 1, mp] = with_lanes(q_rows, mp, _pick_rows(row, offset_lanes(-1.0) + [
                    (6, 9, -1.0), (9, 10, hi3[0]), (10, 11, hi3[1]), (11, 12, hi3[2])]))
                qv_ref[t, 2, mp] = with_lanes(q_rows, mp, _pick_rows(row, [(9, 10, -ATTN_NULL_SCORE)]))
                diag_q.append(with_lanes(q_rows, mp, _pick_rows(
                    row, [(9, 10, s3[0]), (10, 11, s3[1]), (11, 12, s3[2])])))
            qis.append(qi)
            diag_qs.append(diag_q)

        def visit_block(t, u):
            qi = qis[t]
            if dk is None:
                j = (u - 1) + (u - 1 >= qi).astype(jnp.int32)
                return j, (j > qi).astype(jnp.int32)
            j = qi - dk + (u - 1) if u <= dk else qi + (u - dk)
            valid = (j >= 0) & (j < nblk)
            return jnp.clip(j, 0, nblk - 1), jnp.where(valid, 0 if u <= dk else 1, 2)

        def scores_stage(t, u):
            if u == 0:
                j, qts, bias = qis[t], diag_qs[t], dbias_ref[...]
            else:
                j, var = visit_block(t, u)
                qts, bias = (qv_ref[t, var, 0], qv_ref[t, var, 1]), None
            start = pl.multiple_of(j * blk, blk)
            sums = []
            for mp, kref in enumerate((k1_ref, k2_ref)):
                s = jnp.dot(kref[pl.ds(start, blk), :], qts[mp], preferred_element_type=F32)
                p = jnp.exp2((s if bias is None else s + bias).astype(BF16))
                p_ref[t, u, mp] = p
                sums.append(jnp.sum(p.astype(F32).reshape(blk // 8, 8, blk), axis=0))
            lsums[t] = sums if lsums[t] is None else [lsums[t][mp] + sums[mp] for mp in range(2)]

        def weighted_values(t, u):
            j = qis[t] if u == 0 else visit_block(t, u)[0]
            vt = vt_ref[0:V_DIM, pl.ds(pl.multiple_of(j * blk, blk), blk)]
            mixed = p_ref[t, u, 0] - ratios[t] * p_ref[t, u, 1]
            pv = jnp.dot(vt, mixed, preferred_element_type=F32)
            accs[t] = pv if accs[t] is None else accs[t] + pv

        visits = nblk if dk is None else 2 * dk + 1
        accs = [None] * tiles
        lsums = [None] * tiles
        for u in range(visits):
            for t in range(tiles):
                scores_stage(t, u)
        inv_l1, ratios = [], []
        for t in range(tiles):
            l1, l2 = [jnp.sum(l, axis=0, keepdims=True) for l in lsums[t]]
            inv_l1.append(1.0 / l1)
            ratios.append(jnp.broadcast_to(lam * l1 / l2, (blk, blk)).astype(BF16))
        for u in range(visits):
            for t in range(tiles):
                weighted_values(t, u)
        for t in range(tiles):
            finish(t, accs[t] * inv_l1[t])

    heads_by_radius = {}
    for h, dk in shifted_heads.items():
        heads_by_radius.setdefault(dk, []).append(h)
    for dk, heads in heads_by_radius.items():
        in_group = functools.reduce(jnp.logical_or, [head == h for h in heads])
        pl.when(shifted_ok & in_group)(functools.partial(shifted_path, dk))

    def general_tile(t):
        qi = tiles * step + t

        def key_block(n):
            return jnp.where(n == 0, qi, n - (n <= qi).astype(jnp.int32))

        def scores_stage(j, qts, slot, bias=None):
            start = pl.multiple_of(j * blk, blk)
            for mp, kref in enumerate((k1_ref, k2_ref)):
                s = jnp.dot(kref[pl.ds(start, blk), :], qts[mp], preferred_element_type=F32)
                s_ref[slot, mp] = s if bias is None else s + bias

        def softmax_stage(j, slot, pslot):
            const = -(c_blk * jnp.abs(qi - j).astype(F32))
            for mp in range(2):
                s = s_ref[slot, mp]
                m_old = m_ref[mp]
                m_new = jnp.maximum(m_old, jnp.max(s, axis=0, keepdims=True) + const)
                p_ref[t, pslot, mp] = jnp.exp2(s - (m_new - const)).astype(BF16)
                alpha_ref[pslot, mp] = jnp.exp2(m_old - m_new)
                m_ref[mp] = m_new

        def values_stage(j, pslot):
            start = pl.multiple_of(j * blk, blk)
            vt = vt_ref[:, pl.ds(start, blk)]
            for mp in range(2):
                pv = jnp.dot(vt, p_ref[t, pslot, mp], preferred_element_type=F32)
                acc_ref[mp] = acc_ref[mp] * alpha_ref[pslot, mp] + pv

        q_rows = query_rows(t)
        for mp in range(2):
            qv_ref[t, 0, mp] = with_lanes(q_rows, mp, _pick_rows(row, offset_lanes(1.0)))
            qv_ref[t, 1, mp] = with_lanes(q_rows, mp, _pick_rows(row, offset_lanes(-1.0)))
        m_ref[...] = jnp.full(m_ref.shape, -jnp.inf, F32)
        acc_ref[...] = jnp.zeros(acc_ref.shape, F32)
        zeros = jnp.zeros((HEAD_DIM, blk), F32)
        scores_stage(qi, (with_lanes(q_rows, 0, zeros), with_lanes(q_rows, 1, zeros)), 0, dbias_ref[...])
        j1 = key_block(1)
        side = (j1 > qi).astype(jnp.int32)
        scores_stage(j1, (qv_ref[t, side, 0], qv_ref[t, side, 1]), 1)
        for n in range(nblk):
            if n > 0:
                values_stage(key_block(n - 1), (n - 1) % 2)
            softmax_stage(key_block(n), n % ATTN_SLOTS, n % 2)
            if n + 2 < nblk:
                ja = key_block(n + 2)
                side = (ja > qi).astype(jnp.int32)
                scores_stage(ja, (qv_ref[t, side, 0], qv_ref[t, side, 1]), (n + 2) % ATTN_SLOTS)
        values_stage(key_block(nblk - 1), (nblk - 1) % 2)
        a1, a2 = acc_ref[0], acc_ref[1]
        finish(t, a1[0:V_DIM] / a1[V_DIM:V_DIM + 1] - lam * (a2[0:V_DIM] / a2[V_DIM:V_DIM + 1]))

    @pl.when(jnp.logical_not(shifted_ok))
    def _():
        for t in range(tiles):
            general_tile(t)


def _attention(q, k, vt, lamv, sub_g, q_g, k_g, slopes, lam_init, batch, seq):
    smem = pl.BlockSpec(memory_space=pltpu.SMEM)
    blk = ATTN_BLOCK
    nq = seq // blk
    tiles = ATTN_TILES
    steps = nq // tiles
    k3 = k.reshape(batch, seq, D_MODEL)
    shifted_heads = {}
    for h in range(N_HEADS):
        slope = 2.0 ** (-8.0 * (h + 1) / N_HEADS)
        dk = math.ceil((ATTN_ZERO_LOG2 / (slope * LOG2E) - 1.0) / blk)
        shifted_heads[h] = dk if 2 * dk + 1 < nq else None
    return pl.pallas_call(
        functools.partial(_attn_kernel, shifted_heads=shifted_heads),
        grid=(batch, N_HEADS, steps),
        in_specs=[
            smem, smem,
            pl.BlockSpec((tiles * blk, V_DIM), lambda b, h, i: (b * steps + i, h)),
            pl.BlockSpec((None, seq, V_DIM), lambda b, h, i: (b, 0, h)),
            pl.BlockSpec((V_DIM, seq), lambda b, h, i: (h, b)),
            pl.BlockSpec((4, HEAD_DIM), lambda b, h, i: (0, 0)),
            pl.BlockSpec((1, V_DIM), lambda b, h, i: (0, 0)),
            pl.BlockSpec((1, HEAD_DIM), lambda b, h, i: (0, 0)),
            pl.BlockSpec((1, HEAD_DIM), lambda b, h, i: (0, 0)),
        ],
        out_specs=pl.BlockSpec((tiles * blk, V_DIM), lambda b, h, i: (b * steps + i, h)),
        out_shape=jax.ShapeDtypeStruct((batch * seq, D_MODEL), BF16),
        scratch_shapes=[
            pltpu.VMEM((seq, V_DIM), BF16),
            pltpu.VMEM((seq, V_DIM), BF16),
            pltpu.VMEM((ATTN_VT_ROWS, seq), BF16),
            pltpu.VMEM((blk, blk), F32),
            pltpu.VMEM((8, 128), F32),
            pltpu.SMEM((1,), jnp.int32),
            pltpu.VMEM((tiles, 3, 2, V_DIM, blk), BF16),
            pltpu.VMEM((ATTN_SLOTS, 2, blk, blk), F32),
            pltpu.VMEM((tiles, nq, 2, blk, blk), BF16),
            pltpu.VMEM((2, 2, 1, blk), F32),
            pltpu.VMEM((2, 1, blk), F32),
            pltpu.VMEM((2, ATTN_VT_ROWS, blk), F32),
        ],
        compiler_params=pltpu.CompilerParams(
            dimension_semantics=("arbitrary", "arbitrary", "arbitrary"),
            vmem_limit_bytes=V7X_VMEM_LIMIT),
        name="diff_attn",
    )(slopes, lam_init, q, k3, vt, lamv, sub_g, q_g, k_g)


def _merge_kernel(x_ref, oa_ref, u_ref, sv_ref, ga_ref, gb_ref, sgw_ref, sgb_ref,
                  wa_ref, wb_ref, wo_ref, o_ref, ob_ref):
    for c in range(MERGE_ROWS // CHUNK):
        rows = slice(c * CHUNK, (c + 1) * CHUNK)
        for g in range(N_GROUPS):
            cols = slice(g * GROUP_DIM, (g + 1) * GROUP_DIM)
            mixed = jnp.dot(sgw_ref[g], sv_ref[rows, cols], preferred_element_type=F32)
            ob_ref[rows, cols] = (u_ref[rows, cols].astype(F32) * (mixed + sgb_ref[:, cols])).astype(BF16)
    ya = jnp.dot(oa_ref[...], wa_ref[...], preferred_element_type=F32)
    yb = jnp.dot(ob_ref[...], wb_ref[...], preferred_element_type=F32)
    merged = (ga_ref[...].astype(F32) * ya + gb_ref[...].astype(F32) * yb).astype(BF16)
    o_ref[...] = x_ref[...] + jnp.dot(merged, wo_ref[...], preferred_element_type=F32)


def _merge(x2, oa, u, sv, ga, gb, sgw, sgb_full, wa, wb, wo):
    t = x2.shape[0]
    row = lambda i: (i, 0)
    fixed2 = lambda i: (0, 0)
    tile = pl.BlockSpec((MERGE_ROWS, D_MODEL), row)
    wspec = pl.BlockSpec((D_MODEL, D_MODEL), fixed2, pipeline_mode=pl.Buffered(1))
    return pl.pallas_call(
        _merge_kernel,
        grid=(t // MERGE_ROWS,),
        in_specs=[
            tile, tile, tile, tile, tile, tile,
            pl.BlockSpec((N_GROUPS, CHUNK, CHUNK), lambda i: (0, 0, 0)),
            pl.BlockSpec((CHUNK, D_MODEL), fixed2),
            wspec, wspec, wspec,
        ],
        out_specs=tile,
        out_shape=jax.ShapeDtypeStruct((t, D_MODEL), F32),
        scratch_shapes=[pltpu.VMEM((MERGE_ROWS, D_MODEL), BF16)],
        compiler_params=pltpu.CompilerParams(
            dimension_semantics=("arbitrary",), vmem_limit_bytes=V7X_VMEM_LIMIT),
        name="sgu_merge",
    )(x2, oa, u, sv, ga, gb, sgw, sgb_full, wa, wb, wo)


def _ffn_kernel(x_ref, prev_ref, next_ref, g_ref, wup_hbm, cw_ref, cb_ref, wdn_hbm,
                o_ref, h_ref, act_ref, wup_ref, wdn_ref, up_stage_ref, dn_stage_ref, sem,
                *, blocks_per_seq):
    i = pl.program_id(0)

    @pl.when(i == 0)
    def _():
        _load_weights_bf16(wup_hbm, wup_ref, up_stage_ref, sem, axis=1)
        _load_weights_bf16(wdn_hbm, wdn_ref, dn_stage_ref, sem, axis=0)

    tm, halo = FFN_ROWS, FFN_HALO
    g = g_ref[...]
    x = x_ref[...]
    keep_prev = (i % blocks_per_seq != 0).astype(F32)
    keep_next = ((i + 1) % blocks_per_seq != 0).astype(F32)
    h_ref[0:halo, :] = (_rms_rows(prev_ref[...], g) * keep_prev).astype(BF16)
    h_ref[halo:halo + tm, :] = _rms_rows(x, g).astype(BF16)
    h_ref[halo + tm:, :] = (_rms_rows(next_ref[...], g) * keep_next).astype(BF16)
    h = h_ref[...]
    ext = tm + 2 * halo

    def conv(cols):
        up = jnp.dot(h, wup_ref[:, cols], preferred_element_type=F32)
        y = (pltpu.roll(up, 1, 0) * cw_ref[0:1, cols] + up * cw_ref[1:2, cols]
             + pltpu.roll(up, ext - 1, 0) * cw_ref[2:3, cols])
        return y[halo:halo + tm] + cb_ref[:, cols]

    for lo in range(0, D_FF, FFN_COLS):
        hi = min(lo + FFN_COLS, D_FF)
        gate = conv(slice(lo, hi))
        val = conv(slice(D_FF + lo, D_FF + hi))
        act_ref[:, lo:hi] = (gate * _sigmoid(gate) * val).astype(BF16)
    o_ref[...] = x + jnp.dot(act_ref[...], wdn_ref[...], preferred_element_type=F32)


def _ffn(x2, ln_g, wup, conv_w, conv_b, wdn, seq):
    t = x2.shape[0]
    tm, halo = FFN_ROWS, FFN_HALO
    r = tm // halo
    last = t // halo - 1
    fixed = lambda i: (0, 0)
    return pl.pallas_call(
        functools.partial(_ffn_kernel, blocks_per_seq=seq // tm),
        grid=(t // tm,),
        in_specs=[
            pl.BlockSpec((tm, D_MODEL), lambda i: (i, 0)),
            pl.BlockSpec((halo, D_MODEL), lambda i: (jnp.maximum(i * r - 1, 0), 0)),
            pl.BlockSpec((halo, D_MODEL), lambda i: (jnp.minimum((i + 1) * r, last), 0)),
            pl.BlockSpec((1, D_MODEL), fixed),
            pl.BlockSpec(memory_space=pl.ANY),
            pl.BlockSpec((3, 2 * D_FF), fixed),
            pl.BlockSpec((1, 2 * D_FF), fixed),
            pl.BlockSpec(memory_space=pl.ANY),
        ],
        out_specs=pl.BlockSpec((tm, D_MODEL), lambda i: (i, 0)),
        out_shape=jax.ShapeDtypeStruct((t, D_MODEL), F32),
        scratch_shapes=[pltpu.VMEM((tm + 2 * halo, D_MODEL), BF16),
                        pltpu.VMEM((tm, D_FF), BF16),
                        pltpu.VMEM((D_MODEL, 2 * D_FF), BF16),
                        pltpu.VMEM((D_FF, D_MODEL), BF16),
                        pltpu.VMEM((2, D_MODEL, WEIGHT_STAGE_COLS), F32),
                        pltpu.VMEM((2, WEIGHT_STAGE_COLS // 2, D_MODEL), F32),
                        pltpu.SemaphoreType.DMA((2,))],
        compiler_params=pltpu.CompilerParams(
            dimension_semantics=("arbitrary",), vmem_limit_bytes=V7X_VMEM_LIMIT),
        name="conv_ffn",
    )(x2, x2, x2, ln_g, wup, conv_w, conv_b, wdn)


def kernel(x, ln_mix_g, w_in, q_norm_g, k_norm_g, lambda_q1, lambda_k1, lambda_q2, lambda_k2,
           subln_g, sg_norm_g, sg_w, sg_b, w_branch_a, w_branch_b, w_out, ln_ffn_g, w_up,
           conv_w, conv_b, w_down):
    batch, seq, d = x.shape
    depth = w_in.shape[0]
    slopes = 2.0 ** (-8.0 * jnp.arange(1, N_HEADS + 1, dtype=F32) / N_HEADS)
    x2 = x.reshape(batch * seq, d)
    for l in range(depth):
        lam_init = jnp.full((1,), 0.8 - 0.6 * math.exp(-0.3 * l), F32)
        q, k, vt, u, sv, ga, gb = _projection(
            x2, ln_mix_g[l][None], w_in[l],
            jnp.tile(q_norm_g[l], 2 * N_HEADS)[None], jnp.tile(k_norm_g[l], 2 * N_HEADS)[None],
            sg_norm_g[l][None])
        lamv = jnp.stack([lambda_q1[l], lambda_k1[l], lambda_q2[l], lambda_k2[l]])
        oa = _attention(q, k, vt, lamv, subln_g[l][None], q_norm_g[l][None], k_norm_g[l][None],
                        slopes, lam_init, batch, seq)
        sgb_full = jnp.repeat(sg_b[l].T, GROUP_DIM, axis=1)
        x2 = _merge(x2, oa, u, sv, ga, gb, sg_w[l].astype(BF16), sgb_full,
                    w_branch_a[l].astype(BF16), w_branch_b[l].astype(BF16), w_out[l].astype(BF16))
        x2 = _ffn(x2, ln_ffn_g[l][None], w_up[l], conv_w[l], conv_b[l][None], w_down[l], seq)
    return x2.reshape(batch, seq, d)
```

```python
import functools
import math

import jax
import jax.numpy as jnp
from jax import lax
from jax.experimental import pallas as pl
from jax.experimental.pallas import tpu as pltpu

F32 = jnp.float32
BF16 = jnp.bfloat16

D_MODEL = 1024
N_HEADS = 8
HEAD_DIM = 64
V_DIM = 2 * HEAD_DIM
CHUNK = 128
N_GROUPS = 8
GROUP_DIM = D_MODEL // N_GROUPS
D_FF = 2816
EPS = 1e-6
N_SECTIONS = 7

V7X_MXU_DIM = 256
V7X_VMEM_LIMIT = 56 * 1024 * 1024

WEIGHT_STAGE_COLS = 512
PROJ_ROWS = 512
PROJ_COLS = 512
ATTN_BLOCK = 256
ATTN_VT_ROWS = V_DIM + 16
ATTN_SLOTS = 4
ATTN_TILES = 8
LOG2E = math.log2(math.e)
ATTN_MAX_SHIFT = 30.0
ATTN_ZERO_LOG2 = 160.0
ATTN_NULL_SCORE = 30000.0
MERGE_ROWS = 512
FFN_ROWS = 1024
FFN_HALO = 16
FFN_COLS = 512


def _rms_rows(x, g):
    return x * lax.rsqrt(jnp.mean(x * x, axis=-1, keepdims=True) + EPS) * g


def _gelu_tanh(x):
    c = math.sqrt(2.0 / math.pi)
    return 0.5 * x * (1.0 + jnp.tanh(c * (x + 0.044715 * (x * x * x))))


def _sigmoid(x):
    return 1.0 / (1.0 + jnp.exp(-x))


def _group_mean_matrix(group):
    r = lax.broadcasted_iota(jnp.int32, (V7X_MXU_DIM, V7X_MXU_DIM), 0) // group
    c = lax.broadcasted_iota(jnp.int32, (V7X_MXU_DIM, V7X_MXU_DIM), 1) // group
    return jnp.where(r == c, 1.0 / group, 0.0).astype(BF16)


def _group_rms_scale(a, mean_mat):
    sq = (a * a).astype(BF16)
    w = mean_mat.shape[0]
    ms = jnp.concatenate([jnp.dot(sq[:, lo:lo + w], mean_mat, preferred_element_type=F32)
                          for lo in range(0, a.shape[1], w)], axis=1)
    return lax.rsqrt(ms + EPS)


def _load_weights_bf16(src_hbm, dst_ref, stage_ref, sem, axis):
    chunk = stage_ref.shape[1 + axis]
    n = src_hbm.shape[axis] // chunk

    def window(c):
        return (slice(None), pl.ds(c * chunk, chunk)) if axis == 1 else (pl.ds(c * chunk, chunk), slice(None))

    def copy(c):
        return pltpu.make_async_copy(src_hbm.at[window(c)], stage_ref.at[c % 2], sem.at[c % 2])

    copy(0).start()
    for c in range(n):
        if c + 1 < n:
            copy(c + 1).start()
        copy(c).wait()
        dst_ref[window(c)] = stage_ref[c % 2].astype(BF16)


def _proj_kernel(x_ref, g_ref, w_hbm, qg_ref, kg_ref, sgg_ref,
                 q_ref, k_ref, v_ref, u_ref, sv_ref, ga_ref, gb_ref, w_ref, stage_ref, sem):
    @pl.when(pl.program_id(0) == 0)
    def _():
        _load_weights_bf16(w_hbm, w_ref, stage_ref, sem, axis=1)

    h = _rms_rows(x_ref[...], g_ref[...]).astype(BF16)
    mean64 = _group_mean_matrix(HEAD_DIM)
    mean128 = _group_mean_matrix(GROUP_DIM)
    w = PROJ_COLS

    def section(s, c):
        lo = s * D_MODEL + c * w
        return jnp.dot(h, w_ref[:, lo:lo + w], preferred_element_type=F32)

    for c in range(D_MODEL // w):
        cols = slice(c * w, (c + 1) * w)
        a = section(0, c)
        q_ref[:, cols] = (a * _group_rms_scale(a, mean64) * (qg_ref[:, cols] * (HEAD_DIM ** -0.5 * LOG2E))).astype(BF16)
        a = section(1, c)
        k_ref[:, cols] = (a * _group_rms_scale(a, mean64) * kg_ref[:, cols]).astype(BF16)
        v_ref[cols, :] = section(2, c).T.astype(BF16)
        u_ref[:, cols] = _gelu_tanh(section(3, c)).astype(BF16)
        a = _gelu_tanh(section(4, c))
        sv_ref[:, cols] = (a * _group_rms_scale(a, mean128) * sgg_ref[:, cols]).astype(BF16)
        ga_ref[:, cols] = _sigmoid(section(5, c)).astype(BF16)
        gb_ref[:, cols] = _sigmoid(section(6, c)).astype(BF16)


def _projection(x2, ln_g, w_in, q_g, k_g, sg_g):
    t = x2.shape[0]
    row = lambda i: (i, 0)
    fixed = lambda i: (0, 0)
    vec = pl.BlockSpec((1, D_MODEL), fixed)
    out = jax.ShapeDtypeStruct((t, D_MODEL), BF16)
    tile = pl.BlockSpec((PROJ_ROWS, D_MODEL), row)
    return pl.pallas_call(
        _proj_kernel,
        grid=(t // PROJ_ROWS,),
        in_specs=[
            pl.BlockSpec((PROJ_ROWS, D_MODEL), row),
            vec,
            pl.BlockSpec(memory_space=pl.ANY),
            vec, vec, vec,
        ],
        out_specs=[tile, tile, pl.BlockSpec((D_MODEL, PROJ_ROWS), lambda i: (0, i)), tile, tile, tile, tile],
        out_shape=[out, out, jax.ShapeDtypeStruct((D_MODEL, t), BF16), out, out, out, out],
        scratch_shapes=[pltpu.VMEM((D_MODEL, N_SECTIONS * D_MODEL), BF16),
                        pltpu.VMEM((2, D_MODEL, WEIGHT_STAGE_COLS), F32),
                        pltpu.SemaphoreType.DMA((2,))],
        compiler_params=pltpu.CompilerParams(
            dimension_semantics=("arbitrary",), vmem_limit_bytes=V7X_VMEM_LIMIT),
        name="proj",
    )(x2, ln_g, w_in, q_g, k_g, sg_g)


def _split3(c):
    hi = c.astype(BF16)
    mid = (c - hi.astype(F32)).astype(BF16)
    lo = (c - hi.astype(F32) - mid.astype(F32)).astype(BF16)
    return hi.astype(F32), mid.astype(F32), lo.astype(F32)


def _pick_rows(row, entries, default=0.0):
    out = default
    for lo, hi, val in reversed(entries):
        out = jnp.where((row >= lo) & (row < hi), val, out)
    return out


def _attn_kernel(slopes_ref, lam_init_ref, q_ref, k_ref, v_ref, lamv_ref, subg_ref, qg_ref, kg_ref, o_ref,
                 k1_ref, k2_ref, vt_ref, dbias_ref, shift_ref, flag_ref, qv_ref, s_ref, p_ref,
                 alpha_ref, m_ref, acc_ref, *, shifted_heads):
    head = pl.program_id(1)
    step = pl.program_id(2)
    blk = ATTN_BLOCK
    seq = k_ref.shape[0]
    nblk = seq // blk
    c = jnp.full((1, 1), slopes_ref[head] * LOG2E, F32)
    c3 = _split3(c)
    c_blk = c * float(blk)

    @pl.when(step == 0)
    def _():
        lane_b = lax.broadcasted_iota(jnp.int32, (blk, V_DIM), 1)
        off = lax.broadcasted_iota(jnp.int32, (blk, V_DIM), 0).astype(F32)
        for mp, ref in enumerate((k1_ref, k2_ref)):
            a = lane_b - (HEAD_DIM if mp == 0 else 0)
            own_b = (lane_b < HEAD_DIM) if mp == 0 else (lane_b >= HEAD_DIM)
            fixed = _pick_rows(a, [(0, 1, -c3[0]), (1, 2, -c3[1]), (2, 3, -c3[2]), (3, 6, off), (9, 12, 1.0)])
            for jb in range(nblk):
                cb3 = _split3(c_blk * float(jb))
                block_lanes = _pick_rows(a[0:1], [(6, 7, cb3[0]), (7, 8, cb3[1]), (8, 9, cb3[2])])
                rows = slice(jb * blk, (jb + 1) * blk)
                ref[rows, :] = jnp.where(own_b, k_ref[rows, :], (fixed + block_lanes).astype(BF16))
        gmax = (jnp.max(jnp.abs(qg_ref[...]), axis=1, keepdims=True)
                * jnp.max(jnp.abs(kg_ref[...]), axis=1, keepdims=True))
        bound = 1.01 * (HEAD_DIM * HEAD_DIM ** -0.5 * LOG2E) * gmax + 0.01
        shift_ref[...] = jnp.broadcast_to(bound, shift_ref.shape)
        flag_ref[0] = (bound[0, 0] <= ATTN_MAX_SHIFT).astype(jnp.int32)
        vt_ref[0:V_DIM, :] = v_ref[...]
        r = lax.broadcasted_iota(jnp.int32, (ATTN_VT_ROWS - V_DIM, seq), 0)
        vt_ref[V_DIM:, :] = jnp.where(r == 0, 1.0, 0.0).astype(BF16)
        d = lax.broadcasted_iota(jnp.int32, (blk, blk), 0) - lax.broadcasted_iota(jnp.int32, (blk, blk), 1)
        dbias_ref[...] = -(c * jnp.abs(d).astype(F32))

    shifted_ok = flag_ref[0] == 1
    tiles = ATTN_TILES
    row = lax.broadcasted_iota(jnp.int32, (HEAD_DIM, blk), 0)
    il = lax.broadcasted_iota(jnp.int32, (HEAD_DIM, blk), 1).astype(F32)
    lamv = lamv_ref[...]
    lam_init = lam_init_ref[0]
    lam = (jnp.exp(jnp.sum(lamv[0:1] * lamv[1:2], axis=-1, keepdims=True))
           - jnp.exp(jnp.sum(lamv[2:3] * lamv[3:4], axis=-1, keepdims=True)) + lam_init)

    def query_rows(t):
        qt = q_ref[t * blk:(t + 1) * blk, :].astype(F32).T
        return qt[0:HEAD_DIM], qt[HEAD_DIM:]

    def with_lanes(q_rows, mp, aug):
        parts = [q_rows[0], aug] if mp == 0 else [aug, q_rows[1]]
        return jnp.concatenate(parts, axis=0).astype(BF16)

    def offset_lanes(sigma):
        return [(0, 3, sigma * il), (3, 4, sigma * c3[0]), (4, 5, sigma * c3[1]), (5, 6, sigma * c3[2])]

    def finish(t, ot):
        ot = ot * lax.rsqrt(jnp.mean(ot * ot, axis=0, keepdims=True) + EPS)
        o_ref[t * blk:(t + 1) * blk, :] = (ot.T * (subg_ref[...] * (1.0 - lam_init))).astype(BF16)

    def shifted_path(dk):
        shift = shift_ref[0:1, 0:1]
        s3 = _split3(-shift)
        qis, diag_qs = [], []
        for t in range(tiles):
            qi = tiles * step + t
            cbi = c_blk * qi.astype(F32)
            q_rows = query_rows(t)
            lo3 = _split3(-cbi - shift)
            hi3 = _split3(cbi - shift)
            diag_q = []
            for mp in range(2):
                qv_ref[t, 0, mp] = with_lanes(q_rows, mp, _pick_rows(row, offset_lanes(1.0) + [
                    (6, 9, 1.0), (9, 10, lo3[0]), (10, 11, lo3[1]), (11, 12, lo3[2])]))
                qv_ref[t, 1, mp] = with_lanes(q_rows, mp, _pick_rows(row, offset_lanes(-1.0) + [
                    (6, 9, -1.0), (9, 10, hi3[0]), (10, 11, hi3[1]), (11, 12, hi3[2])]))
                qv_ref[t, 2, mp] = with_lanes(q_rows, mp, _pick_rows(row, [(9, 10, -ATTN_NULL_SCORE)]))
                diag_q.append(with_lanes(q_rows, mp, _pick_rows(
                    row, [(9, 10, s3[0]), (10, 11, s3[1]), (11, 12, s3[2])])))
            qis.append(qi)
            diag_qs.append(diag_q)

        def visit_block(t, u):
            qi = qis[t]
            if dk is None:
                j = (u - 1) + (u - 1 >= qi).astype(jnp.int32)
                return j, (j > qi).astype(jnp.int32)
            j = qi - dk + (u - 1) if u <= dk else qi + (u - dk)
            valid = (j >= 0) & (j < nblk)
            return jnp.clip(j, 0, nblk - 1), jnp.where(valid, 0 if u <= dk else 1, 2)

        def scores_stage(t, u):
            if u == 0:
                j, qts, bias = qis[t], diag_qs[t], dbias_ref[...]
            else:
                j, var = visit_block(t, u)
                qts, bias = (qv_ref[t, var, 0], qv_ref[t, var, 1]), None
            start = pl.multiple_of(j * blk, blk)
            sums = []
            for mp, kref in enumerate((k1_ref, k2_ref)):
                s = jnp.dot(kref[pl.ds(start, blk), :], qts[mp], preferred_element_type=F32)
                p = jnp.exp2(s if bias is None else s + bias)
                p_ref[t, u, mp] = p.astype(BF16)
                sums.append(jnp.sum(p.reshape(blk // 8, 8, blk), axis=0))
            lsums[t] = sums if lsums[t] is None else [lsums[t][mp] + sums[mp] for mp in range(2)]

        def weighted_values(t, u):
            j = qis[t] if u == 0 else visit_block(t, u)[0]
            vt = vt_ref[0:V_DIM, pl.ds(pl.multiple_of(j * blk, blk), blk)]
            mixed = p_ref[t, u, 0] - ratios[t] * p_ref[t, u, 1]
            pv = jnp.dot(vt, mixed, preferred_element_type=F32)
            accs[t] = pv if accs[t] is None else accs[t] + pv

        visits = nblk if dk is None else 2 * dk + 1
        accs = [None] * tiles
        lsums = [None] * tiles
        for u in range(visits):
            for t in range(tiles):
                scores_stage(t, u)
        inv_l1, ratios = [], []
        for t in range(tiles):
            l1, l2 = [jnp.sum(l, axis=0, keepdims=True) for l in lsums[t]]
            inv_l1.append(1.0 / l1)
            ratios.append(jnp.broadcast_to(lam * l1 / l2, (blk, blk)).astype(BF16))
        for u in range(visits):
            for t in range(tiles):
                weighted_values(t, u)
        for t in range(tiles):
            finish(t, accs[t] * inv_l1[t])

    heads_by_radius = {}
    for h, dk in shifted_heads.items():
        heads_by_radius.setdefault(dk, []).append(h)
    for dk, heads in heads_by_radius.items():
        in_group = functools.reduce(jnp.logical_or, [head == h for h in heads])
        pl.when(shifted_ok & in_group)(functools.partial(shifted_path, dk))

    def general_tile(t):
        qi = tiles * step + t

        def key_block(n):
            return jnp.where(n == 0, qi, n - (n <= qi).astype(jnp.int32))

        def scores_stage(j, qts, slot, bias=None):
            start = pl.multiple_of(j * blk, blk)
            for mp, kref in enumerate((k1_ref, k2_ref)):
                s = jnp.dot(kref[pl.ds(start, blk), :], qts[mp], preferred_element_type=F32)
                s_ref[slot, mp] = s if bias is None else s + bias

        def softmax_stage(j, slot, pslot):
            const = -(c_blk * jnp.abs(qi - j).astype(F32))
            for mp in range(2):
                s = s_ref[slot, mp]
                m_old = m_ref[mp]
                m_new = jnp.maximum(m_old, jnp.max(s, axis=0, keepdims=True) + const)
                p_ref[t, pslot, mp] = jnp.exp2(s - (m_new - const)).astype(BF16)
                alpha_ref[pslot, mp] = jnp.exp2(m_old - m_new)
                m_ref[mp] = m_new

        def values_stage(j, pslot):
            start = pl.multiple_of(j * blk, blk)
            vt = vt_ref[:, pl.ds(start, blk)]
            for mp in range(2):
                pv = jnp.dot(vt, p_ref[t, pslot, mp], preferred_element_type=F32)
                acc_ref[mp] = acc_ref[mp] * alpha_ref[pslot, mp] + pv

        q_rows = query_rows(t)
        for mp in range(2):
            qv_ref[t, 0, mp] = with_lanes(q_rows, mp, _pick_rows(row, offset_lanes(1.0)))
            qv_ref[t, 1, mp] = with_lanes(q_rows, mp, _pick_rows(row, offset_lanes(-1.0)))
        m_ref[...] = jnp.full(m_ref.shape, -jnp.inf, F32)
        acc_ref[...] = jnp.zeros(acc_ref.shape, F32)
        zeros = jnp.zeros((HEAD_DIM, blk), F32)
        scores_stage(qi, (with_lanes(q_rows, 0, zeros), with_lanes(q_rows, 1, zeros)), 0, dbias_ref[...])
        j1 = key_block(1)
        side = (j1 > qi).astype(jnp.int32)
        scores_stage(j1, (qv_ref[t, side, 0], qv_ref[t, side, 1]), 1)
        for n in range(nblk):
            if n > 0:
                values_stage(key_block(n - 1), (n - 1) % 2)
            softmax_stage(key_block(n), n % ATTN_SLOTS, n % 2)
            if n + 2 < nblk:
                ja = key_block(n + 2)
                side = (ja > qi).astype(jnp.int32)
                scores_stage(ja, (qv_ref[t, side, 0], qv_ref[t, side, 1]), (n + 2) % ATTN_SLOTS)
        values_stage(key_block(nblk - 1), (nblk - 1) % 2)
        a1, a2 = acc_ref[0], acc_ref[1]
        finish(t, a1[0:V_DIM] / a1[V_DIM:V_DIM + 1] - lam * (a2[0:V_DIM] / a2[V_DIM:V_DIM + 1]))

    @pl.when(jnp.logical_not(shifted_ok))
    def _():
        for t in range(tiles):
            general_tile(t)


def _attention(q, k, vt, lamv, sub_g, q_g, k_g, slopes, lam_init, batch, seq):
    smem = pl.BlockSpec(memory_space=pltpu.SMEM)
    blk = ATTN_BLOCK
    nq = seq // blk
    tiles = ATTN_TILES
    steps = nq // tiles
    k3 = k.reshape(batch, seq, D_MODEL)
    shifted_heads = {}
    for h in range(N_HEADS):
        slope = 2.0 ** (-8.0 * (h + 1) / N_HEADS)
        dk = math.ceil((ATTN_ZERO_LOG2 / (slope * LOG2E) - 1.0) / blk)
        shifted_heads[h] = dk if 2 * dk + 1 < nq else None
    return pl.pallas_call(
        functools.partial(_attn_kernel, shifted_heads=shifted_heads),
        grid=(batch, N_HEADS, steps),
        in_specs=[
            smem, smem,
            pl.BlockSpec((tiles * blk, V_DIM), lambda b, h, i: (b * steps + i, h)),
            pl.BlockSpec((None, seq, V_DIM), lambda b, h, i: (b, 0, h)),
            pl.BlockSpec((V_DIM, seq), lambda b, h, i: (h, b)),
            pl.BlockSpec((4, HEAD_DIM), lambda b, h, i: (0, 0)),
            pl.BlockSpec((1, V_DIM), lambda b, h, i: (0, 0)),
            pl.BlockSpec((1, HEAD_DIM), lambda b, h, i: (0, 0)),
            pl.BlockSpec((1, HEAD_DIM), lambda b, h, i: (0, 0)),
        ],
        out_specs=pl.BlockSpec((tiles * blk, V_DIM), lambda b, h, i: (b * steps + i, h)),
        out_shape=jax.ShapeDtypeStruct((batch * seq, D_MODEL), BF16),
        scratch_shapes=[
            pltpu.VMEM((seq, V_DIM), BF16),
            pltpu.VMEM((seq, V_DIM), BF16),
            pltpu.VMEM((ATTN_VT_ROWS, seq), BF16),
            pltpu.VMEM((blk, blk), F32),
            pltpu.VMEM((8, 128), F32),
            pltpu.SMEM((1,), jnp.int32),
            pltpu.VMEM((tiles, 3, 2, V_DIM, blk), BF16),
            pltpu.VMEM((ATTN_SLOTS, 2, blk, blk), F32),
            pltpu.VMEM((tiles, nq, 2, blk, blk), BF16),
            pltpu.VMEM((2, 2, 1, blk), F32),
            pltpu.VMEM((2, 1, blk), F32),
            pltpu.VMEM((2, ATTN_VT_ROWS, blk), F32),
        ],
        compiler_params=pltpu.CompilerParams(
            dimension_semantics=("arbitrary", "arbitrary", "arbitrary"),
            vmem_limit_bytes=V7X_VMEM_LIMIT),
        name="diff_attn",
    )(slopes, lam_init, q, k3, vt, lamv, sub_g, q_g, k_g)


def _merge_kernel(x_ref, oa_ref, u_ref, sv_ref, ga_ref, gb_ref, sgw_ref, sgb_ref,
                  wa_ref, wb_ref, wo_ref, o_ref, ob_ref):
    for c in range(MERGE_ROWS // CHUNK):
        rows = slice(c * CHUNK, (c + 1) * CHUNK)
        for g in range(N_GROUPS):
            cols = slice(g * GROUP_DIM, (g + 1) * GROUP_DIM)
            mixed = jnp.dot(sgw_ref[g], sv_ref[rows, cols], preferred_element_type=F32)
            ob_ref[rows, cols] = (u_ref[rows, cols].astype(F32) * (mixed + sgb_ref[:, cols])).astype(BF16)
    ya = jnp.dot(oa_ref[...], wa_ref[...], preferred_element_type=F32)
    yb = jnp.dot(ob_ref[...], wb_ref[...], preferred_element_type=F32)
    merged = (ga_ref[...].astype(F32) * ya + gb_ref[...].astype(F32) * yb).astype(BF16)
    o_ref[...] = x_ref[...] + jnp.dot(merged, wo_ref[...], preferred_element_type=F32)


def _merge(x2, oa, u, sv, ga, gb, sgw, sgb_full, wa, wb, wo):
    t = x2.shape[0]
    row = lambda i: (i, 0)
    fixed2 = lambda i: (0, 0)
    tile = pl.BlockSpec((MERGE_ROWS, D_MODEL), row)
    wspec = pl.BlockSpec((D_MODEL, D_MODEL), fixed2, pipeline_mode=pl.Buffered(1))
    return pl.pallas_call(
        _merge_kernel,
        grid=(t // MERGE_ROWS,),
        in_specs=[
            tile, tile, tile, tile, tile, tile,
            pl.BlockSpec((N_GROUPS, CHUNK, CHUNK), lambda i: (0, 0, 0)),
            pl.BlockSpec((CHUNK, D_MODEL), fixed2),
            wspec, wspec, wspec,
        ],
        out_specs=tile,
        out_shape=jax.ShapeDtypeStruct((t, D_MODEL), F32),
        scratch_shapes=[pltpu.VMEM((MERGE_ROWS, D_MODEL), BF16)],
        compiler_params=pltpu.CompilerParams(
            dimension_semantics=("arbitrary",), vmem_limit_bytes=V7X_VMEM_LIMIT),
        name="sgu_merge",
    )(x2, oa, u, sv, ga, gb, sgw, sgb_full, wa, wb, wo)


def _ffn_kernel(x_ref, prev_ref, next_ref, g_ref, wup_hbm, cw_ref, cb_ref, wdn_hbm,
                o_ref, h_ref, act_ref, wup_ref, wdn_ref, up_stage_ref, dn_stage_ref, sem,
                *, blocks_per_seq):
    i = pl.program_id(0)

    @pl.when(i == 0)
    def _():
        _load_weights_bf16(wup_hbm, wup_ref, up_stage_ref, sem, axis=1)
        _load_weights_bf16(wdn_hbm, wdn_ref, dn_stage_ref, sem, axis=0)

    tm, halo = FFN_ROWS, FFN_HALO
    g = g_ref[...]
    x = x_ref[...]
    keep_prev = (i % blocks_per_seq != 0).astype(F32)
    keep_next = ((i + 1) % blocks_per_seq != 0).astype(F32)
    h_ref[0:halo, :] = (_rms_rows(prev_ref[...], g) * keep_prev).astype(BF16)
    h_ref[halo:halo + tm, :] = _rms_rows(x, g).astype(BF16)
    h_ref[halo + tm:, :] = (_rms_rows(next_ref[...], g) * keep_next).astype(BF16)
    h = h_ref[...]
    ext = tm + 2 * halo

    def conv(cols):
        up = jnp.dot(h, wup_ref[:, cols], preferred_element_type=F32)
        y = (pltpu.roll(up, 1, 0) * cw_ref[0:1, cols] + up * cw_ref[1:2, cols]
             + pltpu.roll(up, ext - 1, 0) * cw_ref[2:3, cols])
        return y[halo:halo + tm] + cb_ref[:, cols]

    for lo in range(0, D_FF, FFN_COLS):
        hi = min(lo + FFN_COLS, D_FF)
        gate = conv(slice(lo, hi))
        val = conv(slice(D_FF + lo, D_FF + hi))
        act_ref[:, lo:hi] = (gate * _sigmoid(gate) * val).astype(BF16)
    o_ref[...] = x + jnp.dot(act_ref[...], wdn_ref[...], preferred_element_type=F32)


def _ffn(x2, ln_g, wup, conv_w, conv_b, wdn, seq):
    t = x2.shape[0]
    tm, halo = FFN_ROWS, FFN_HALO
    r = tm // halo
    last = t // halo - 1
    fixed = lambda i: (0, 0)
    return pl.pallas_call(
        functools.partial(_ffn_kernel, blocks_per_seq=seq // tm),
        grid=(t // tm,),
        in_specs=[
            pl.BlockSpec((tm, D_MODEL), lambda i: (i, 0)),
            pl.BlockSpec((halo, D_MODEL), lambda i: (jnp.maximum(i * r - 1, 0), 0)),
            pl.BlockSpec((halo, D_MODEL), lambda i: (jnp.minimum((i + 1) * r, last), 0)),
            pl.BlockSpec((1, D_MODEL), fixed),
            pl.BlockSpec(memory_space=pl.ANY),
            pl.BlockSpec((3, 2 * D_FF), fixed),
            pl.BlockSpec((1, 2 * D_FF), fixed),
            pl.BlockSpec(memory_space=pl.ANY),
        ],
        out_specs=pl.BlockSpec((tm, D_MODEL), lambda i: (i, 0)),
        out_shape=jax.ShapeDtypeStruct((t, D_MODEL), F32),
        scratch_shapes=[pltpu.VMEM((tm + 2 * halo, D_MODEL), BF16),
                        pltpu.VMEM((tm, D_FF), BF16),
                        pltpu.VMEM((D_MODEL, 2 * D_FF), BF16),
                        pltpu.VMEM((D_FF, D_MODEL), BF16),
                        pltpu.VMEM((2, D_MODEL, WEIGHT_STAGE_COLS), F32),
                        pltpu.VMEM((2, WEIGHT_STAGE_COLS // 2, D_MODEL), F32),
                        pltpu.SemaphoreType.DMA((2,))],
        compiler_params=pltpu.CompilerParams(
            dimension_semantics=("arbitrary",), vmem_limit_bytes=V7X_VMEM_LIMIT),
        name="conv_ffn",
    )(x2, x2, x2, ln_g, wup, conv_w, conv_b, wdn)


def kernel(x, ln_mix_g, w_in, q_norm_g, k_norm_g, lambda_q1, lambda_k1, lambda_q2, lambda_k2,
           subln_g, sg_norm_g, sg_w, sg_b, w_branch_a, w_branch_b, w_out, ln_ffn_g, w_up,
           conv_w, conv_b, w_down):
    batch, seq, d = x.shape
    depth = w_in.shape[0]
    slopes = 2.0 ** (-8.0 * jnp.arange(1, N_HEADS + 1, dtype=F32) / N_HEADS)
    x2 = x.reshape(batch * seq, d)
    for l in range(depth):
        lam_init = jnp.full((1,), 0.8 - 0.6 * math.exp(-0.3 * l), F32)
        q, k, vt, u, sv, ga, gb = _projection(
            x2, ln_mix_g[l][None], w_in[l],
            jnp.tile(q_norm_g[l], 2 * N_HEADS)[None], jnp.tile(k_norm_g[l], 2 * N_HEADS)[None],
            sg_norm_g[l][None])
        lamv = jnp.stack([lambda_q1[l], lambda_k1[l], lambda_q2[l], lambda_k2[l]])
        oa = _attention(q, k, vt, lamv, subln_g[l][None], q_norm_g[l][None], k_norm_g[l][None],
                        slopes, lam_init, batch, seq)
        sgb_full = jnp.repeat(sg_b[l].T, GROUP_DIM, axis=1)
        x2 = _merge(x2, oa, u, sv, ga, gb, sg_w[l].astype(BF16), sgb_full,
                    w_branch_a[l].astype(BF16), w_branch_b[l].astype(BF16), w_out[l].astype(BF16))
        x2 = _ffn(x2, ln_ffn_g[l][None], w_up[l], conv_w[l], conv_b[l][None], w_down[l], seq)
    return x2.reshape(batch, seq, d)
```

```python
import functools
import math

import jax
import jax.numpy as jnp
from jax import lax
from jax.experimental import pallas as pl
from jax.experimental.pallas import tpu as pltpu

F32 = jnp.float32
BF16 = jnp.bfloat16

D_MODEL = 1024
N_HEADS = 8
HEAD_DIM = 64
V_DIM = 2 * HEAD_DIM
CHUNK = 128
N_GROUPS = 8
GROUP_DIM = D_MODEL // N_GROUPS
D_FF = 2816
EPS = 1e-6
N_SECTIONS = 7

V7X_MXU_DIM = 256
BF16_SUBLANES = 16
V7X_VMEM_LIMIT = 56 * 1024 * 1024

WEIGHT_STAGE_COLS = 512
PROJ_ROWS = 512
PROJ_COLS = 512
ATTN_BLOCK = 256
ATTN_VT_ROWS = V_DIM + 16
ATTN_SLOTS = 4
ATTN_TILES = 4
LOG2E = math.log2(math.e)
ATTN_MAX_SHIFT = 30.0
ATTN_ZERO_LOG2 = 160.0
ATTN_NULL_SCORE = 30000.0
MERGE_ROWS = 512
FFN_ROWS = 1024
FFN_HALO = 16
FFN_COLS = 512


def _rms_rows(x, g):
    return x * lax.rsqrt(jnp.mean(x * x, axis=-1, keepdims=True) + EPS) * g


def _gelu_tanh(x):
    c = math.sqrt(2.0 / math.pi)
    return 0.5 * x * (1.0 + jnp.tanh(c * (x + 0.044715 * (x * x * x))))


def _sigmoid(x):
    return 1.0 / (1.0 + jnp.exp(-x))


def _group_mean_matrix(group):
    r = lax.broadcasted_iota(jnp.int32, (V7X_MXU_DIM, V7X_MXU_DIM), 0) // group
    c = lax.broadcasted_iota(jnp.int32, (V7X_MXU_DIM, V7X_MXU_DIM), 1) // group
    return jnp.where(r == c, 1.0 / group, 0.0).astype(BF16)


def _group_rms_scale(a, mean_mat):
    sq = (a * a).astype(BF16)
    w = mean_mat.shape[0]
    ms = jnp.concatenate([jnp.dot(sq[:, lo:lo + w], mean_mat, preferred_element_type=F32)
                          for lo in range(0, a.shape[1], w)], axis=1)
    return lax.rsqrt(ms + EPS)


def _load_weights_bf16(src_hbm, dst_ref, stage_ref, sem, axis):
    chunk = stage_ref.shape[1 + axis]
    n = src_hbm.shape[axis] // chunk

    def window(c):
        return (slice(None), pl.ds(c * chunk, chunk)) if axis == 1 else (pl.ds(c * chunk, chunk), slice(None))

    def copy(c):
        return pltpu.make_async_copy(src_hbm.at[window(c)], stage_ref.at[c % 2], sem.at[c % 2])

    copy(0).start()
    for c in range(n):
        if c + 1 < n:
            copy(c + 1).start()
        copy(c).wait()
        dst_ref[window(c)] = stage_ref[c % 2].astype(BF16)


def _proj_kernel(x_ref, g_ref, w_hbm, qg_ref, kg_ref, sgg_ref,
                 q_ref, k_ref, v_ref, u_ref, sv_ref, ga_ref, gb_ref, w_ref, stage_ref, sem):
    @pl.when(pl.program_id(0) == 0)
    def _():
        _load_weights_bf16(w_hbm, w_ref, stage_ref, sem, axis=1)

    h = _rms_rows(x_ref[...], g_ref[...]).astype(BF16)
    mean64 = _group_mean_matrix(HEAD_DIM)
    mean128 = _group_mean_matrix(GROUP_DIM)
    w = PROJ_COLS

    def section(s, c):
        lo = s * D_MODEL + c * w
        return jnp.dot(h, w_ref[:, lo:lo + w], preferred_element_type=F32)

    for c in range(D_MODEL // w):
        cols = slice(c * w, (c + 1) * w)
        a = section(0, c)
        q_ref[:, cols] = (a * _group_rms_scale(a, mean64) * (qg_ref[:, cols] * (HEAD_DIM ** -0.5 * LOG2E))).astype(BF16)
        a = section(1, c)
        k_ref[:, cols] = (a * _group_rms_scale(a, mean64) * kg_ref[:, cols]).astype(BF16)
        v_ref[cols, :] = section(2, c).T.astype(BF16)
        u_ref[:, cols] = _gelu_tanh(section(3, c)).astype(BF16)
        a = _gelu_tanh(section(4, c))
        sv_ref[:, cols] = (a * _group_rms_scale(a, mean128) * sgg_ref[:, cols]).astype(BF16)
        ga_ref[:, cols] = _sigmoid(section(5, c)).astype(BF16)
        gb_ref[:, cols] = _sigmoid(section(6, c)).astype(BF16)


def _projection(x2, ln_g, w_in, q_g, k_g, sg_g):
    t = x2.shape[0]
    row = lambda i: (i, 0)
    fixed = lambda i: (0, 0)
    vec = pl.BlockSpec((1, D_MODEL), fixed)
    out = jax.ShapeDtypeStruct((t, D_MODEL), BF16)
    tile = pl.BlockSpec((PROJ_ROWS, D_MODEL), row)
    return pl.pallas_call(
        _proj_kernel,
        grid=(t // PROJ_ROWS,),
        in_specs=[
            pl.BlockSpec((PROJ_ROWS, D_MODEL), row),
            vec,
            pl.BlockSpec(memory_space=pl.ANY),
            vec, vec, vec,
        ],
        out_specs=[tile, tile, pl.BlockSpec((D_MODEL, PROJ_ROWS), lambda i: (0, i)), tile, tile, tile, tile],
        out_shape=[out, out, jax.ShapeDtypeStruct((D_MODEL, t), BF16), out, out, out, out],
        scratch_shapes=[pltpu.VMEM((D_MODEL, N_SECTIONS * D_MODEL), BF16),
                        pltpu.VMEM((2, D_MODEL, WEIGHT_STAGE_COLS), F32),
                        pltpu.SemaphoreType.DMA((2,))],
        compiler_params=pltpu.CompilerParams(
            dimension_semantics=("arbitrary",), vmem_limit_bytes=V7X_VMEM_LIMIT),
        name="proj",
    )(x2, ln_g, w_in, q_g, k_g, sg_g)


def _split3(c):
    hi = c.astype(BF16)
    mid = (c - hi.astype(F32)).astype(BF16)
    lo = (c - hi.astype(F32) - mid.astype(F32)).astype(BF16)
    return hi.astype(F32), mid.astype(F32), lo.astype(F32)


def _pick_rows(row, entries, default=0.0):
    out = default
    for lo, hi, val in reversed(entries):
        out = jnp.where((row >= lo) & (row < hi), val, out)
    return out


def _attn_kernel(slopes_ref, lam_init_ref, q_ref, k_ref, v_ref, lamv_ref, subg_ref, qg_ref, kg_ref,
                 wup_ref, wdn_ref, o_ref, wup_bf_ref, wdn_bf_ref,
                 k1_ref, k2_ref, vt_ref, dbias_ref, shift_ref, flag_ref, qv_ref, s_ref, p_ref,
                 alpha_ref, m_ref, acc_ref, *, shifted_heads):
    wup_bf_ref[...] = wup_ref[...].astype(BF16)
    wdn_bf_ref[...] = wdn_ref[...].astype(BF16)
    head = pl.program_id(1)
    step = pl.program_id(2)
    blk = ATTN_BLOCK
    seq = k_ref.shape[0]
    nblk = seq // blk
    c = jnp.full((1, 1), slopes_ref[head] * LOG2E, F32)
    c3 = _split3(c)
    c_blk = c * float(blk)

    @pl.when(step == 0)
    def _():
        lane_b = lax.broadcasted_iota(jnp.int32, (blk, V_DIM), 1)
        off = lax.broadcasted_iota(jnp.int32, (blk, V_DIM), 0).astype(F32)
        for mp, ref in enumerate((k1_ref, k2_ref)):
            a = lane_b - (HEAD_DIM if mp == 0 else 0)
            own_b = (lane_b < HEAD_DIM) if mp == 0 else (lane_b >= HEAD_DIM)
            fixed = _pick_rows(a, [(0, 1, -c3[0]), (1, 2, -c3[1]), (2, 3, -c3[2]), (3, 6, off), (9, 12, 1.0)])
            for jb in range(nblk):
                cb3 = _split3(c_blk * float(jb))
                block_lanes = _pick_rows(a[0:1], [(6, 7, cb3[0]), (7, 8, cb3[1]), (8, 9, cb3[2])])
                rows = slice(jb * blk, (jb + 1) * blk)
                ref[rows, :] = jnp.where(own_b, k_ref[rows, :], (fixed + block_lanes).astype(BF16))
        gmax = (jnp.max(jnp.abs(qg_ref[...]), axis=1, keepdims=True)
                * jnp.max(jnp.abs(kg_ref[...]), axis=1, keepdims=True))
        bound = 1.01 * (HEAD_DIM * HEAD_DIM ** -0.5 * LOG2E) * gmax + 0.01
        shift_ref[...] = jnp.broadcast_to(bound, shift_ref.shape)
        flag_ref[0] = (bound[0, 0] <= ATTN_MAX_SHIFT).astype(jnp.int32)
        vt_ref[0:V_DIM, :] = v_ref[...]
        r = lax.broadcasted_iota(jnp.int32, (ATTN_VT_ROWS - V_DIM, seq), 0)
        vt_ref[V_DIM:, :] = jnp.where(r == 0, 1.0, 0.0).astype(BF16)
        d = lax.broadcasted_iota(jnp.int32, (blk, blk), 0) - lax.broadcasted_iota(jnp.int32, (blk, blk), 1)
        dbias_ref[...] = -(c * jnp.abs(d).astype(F32))

    shifted_ok = flag_ref[0] == 1
    tiles = ATTN_TILES
    row = lax.broadcasted_iota(jnp.int32, (HEAD_DIM, blk), 0)
    il = lax.broadcasted_iota(jnp.int32, (HEAD_DIM, blk), 1).astype(F32)
    lamv = lamv_ref[...]
    lam_init = lam_init_ref[0]
    lam = (jnp.exp(jnp.sum(lamv[0:1] * lamv[1:2], axis=-1, keepdims=True))
           - jnp.exp(jnp.sum(lamv[2:3] * lamv[3:4], axis=-1, keepdims=True)) + lam_init)

    def query_rows(t):
        qt = q_ref[t * blk:(t + 1) * blk, :].astype(F32).T
        return qt[0:HEAD_DIM], qt[HEAD_DIM:]

    def with_lanes(q_rows, mp, aug):
        parts = [q_rows[0], aug] if mp == 0 else [aug, q_rows[1]]
        return jnp.concatenate(parts, axis=0).astype(BF16)

    def offset_lanes(sigma):
        return [(0, 3, sigma * il), (3, 4, sigma * c3[0]), (4, 5, sigma * c3[1]), (5, 6, sigma * c3[2])]

    def finish(t, ot):
        ot = ot * lax.rsqrt(jnp.mean(ot * ot, axis=0, keepdims=True) + EPS)
        o_ref[t * blk:(t + 1) * blk, :] = (ot.T * (subg_ref[...] * (1.0 - lam_init))).astype(BF16)

    def shifted_path(dk):
        shift = shift_ref[0:1, 0:1]
        s3 = _split3(-shift)
        qis, diag_qs = [], []
        for t in range(tiles):
            qi = tiles * step + t
            cbi = c_blk * qi.astype(F32)
            q_rows = query_rows(t)
            lo3 = _split3(-cbi - shift)
            hi3 = _split3(cbi - shift)
            diag_q = []
            for mp in range(2):
                qv_ref[t, 0, mp] = with_lanes(q_rows, mp, _pick_rows(row, offset_lanes(1.0) + [
                    (6, 9, 1.0), (9, 10, lo3[0]), (10, 11, lo3[1]), (11, 12, lo3[2])]))
                qv_ref[t, 1, mp] = with_lanes(q_rows, mp, _pick_rows(row, offset_lanes(-1.0) + [
                    (6, 9, -1.0), (9, 10, hi3[0]), (10, 11, hi3[1]), (11, 12, hi3[2])]))
                qv_ref[t, 2, mp] = with_lanes(q_rows, mp, _pick_rows(row, [(9, 10, -ATTN_NULL_SCORE)]))
                diag_q.append(with_lanes(q_rows, mp, _pick_rows(
                    row, [(9, 10, s3[0]), (10, 11, s3[1]), (11, 12, s3[2])])))
            qis.append(qi)
            diag_qs.append(diag_q)

        def visit_block(t, u):
            qi = qis[t]
            if dk is None:
                j = (u - 1) + (u - 1 >= qi).astype(jnp.int32)
                return j, (j > qi).astype(jnp.int32)
            j = qi - dk + (u - 1) if u <= dk else qi + (u - dk)
            valid = (j >= 0) & (j < nblk)
            return jnp.clip(j, 0, nblk - 1), jnp.where(valid, 0 if u <= dk else 1, 2)

        def scores_stage(t, u):
            if u == 0:
                j, qts, bias = qis[t], diag_qs[t], dbias_ref[...]
            else:
                j, var = visit_block(t, u)
                qts, bias = (qv_ref[t, var, 0], qv_ref[t, var, 1]), None
            start = pl.multiple_of(j * blk, blk)
            sums = []
            for mp, kref in enumerate((k1_ref, k2_ref)):
                s = jnp.dot(kref[pl.ds(start, blk), :], qts[mp], preferred_element_type=F32)
                p = jnp.exp2(s if bias is None else s + bias)
                p_ref[t, u, mp] = p.astype(BF16)
                sums.append(jnp.sum(p.reshape(blk // 8, 8, blk), axis=0))
            lsums[t] = sums if lsums[t] is None else [lsums[t][mp] + sums[mp] for mp in range(2)]

        def weighted_values(t, u):
            j = qis[t] if u == 0 else visit_block(t, u)[0]
            vt = vt_ref[0:V_DIM, pl.ds(pl.multiple_of(j * blk, blk), blk)]
            mixed = p_ref[t, u, 0] - ratios[t] * p_ref[t, u, 1]
            pv = jnp.dot(vt, mixed, preferred_element_type=F32)
            accs[t] = pv if accs[t] is None else accs[t] + pv

        visits = nblk if dk is None else 2 * dk + 1
        accs = [None] * tiles
        lsums = [None] * tiles
        for u in range(visits):
            for t in range(tiles):
                scores_stage(t, u)
        inv_l1, ratios = [], []
        for t in range(tiles):
            l1, l2 = [jnp.sum(l, axis=0, keepdims=True) for l in lsums[t]]
            inv_l1.append(1.0 / l1)
            ratios.append(jnp.broadcast_to(lam * l1 / l2, (blk, blk)).astype(BF16))
        for u in range(visits):
            for t in range(tiles):
                weighted_values(t, u)
        for t in range(tiles):
            finish(t, accs[t] * inv_l1[t])

    heads_by_radius = {}
    for h, dk in shifted_heads.items():
        heads_by_radius.setdefault(dk, []).append(h)
    for dk, heads in heads_by_radius.items():
        in_group = functools.reduce(jnp.logical_or, [head == h for h in heads])
        pl.when(shifted_ok & in_group)(functools.partial(shifted_path, dk))

    def general_tile(t):
        qi = tiles * step + t

        def key_block(n):
            return jnp.where(n == 0, qi, n - (n <= qi).astype(jnp.int32))

        def scores_stage(j, qts, slot, bias=None):
            start = pl.multiple_of(j * blk, blk)
            for mp, kref in enumerate((k1_ref, k2_ref)):
                s = jnp.dot(kref[pl.ds(start, blk), :], qts[mp], preferred_element_type=F32)
                s_ref[slot, mp] = s if bias is None else s + bias

        def softmax_stage(j, slot, pslot):
            const = -(c_blk * jnp.abs(qi - j).astype(F32))
            for mp in range(2):
                s = s_ref[slot, mp]
                m_old = m_ref[mp]
                m_new = jnp.maximum(m_old, jnp.max(s, axis=0, keepdims=True) + const)
                p_ref[t, pslot, mp] = jnp.exp2(s - (m_new - const)).astype(BF16)
                alpha_ref[pslot, mp] = jnp.exp2(m_old - m_new)
                m_ref[mp] = m_new

        def values_stage(j, pslot):
            start = pl.multiple_of(j * blk, blk)
            vt = vt_ref[:, pl.ds(start, blk)]
            for mp in range(2):
                pv = jnp.dot(vt, p_ref[t, pslot, mp], preferred_element_type=F32)
                acc_ref[mp] = acc_ref[mp] * alpha_ref[pslot, mp] + pv

        q_rows = query_rows(t)
        for mp in range(2):
            qv_ref[t, 0, mp] = with_lanes(q_rows, mp, _pick_rows(row, offset_lanes(1.0)))
            qv_ref[t, 1, mp] = with_lanes(q_rows, mp, _pick_rows(row, offset_lanes(-1.0)))
        m_ref[...] = jnp.full(m_ref.shape, -jnp.inf, F32)
        acc_ref[...] = jnp.zeros(acc_ref.shape, F32)
        zeros = jnp.zeros((HEAD_DIM, blk), F32)
        scores_stage(qi, (with_lanes(q_rows, 0, zeros), with_lanes(q_rows, 1, zeros)), 0, dbias_ref[...])
        j1 = key_block(1)
        side = (j1 > qi).astype(jnp.int32)
        scores_stage(j1, (qv_ref[t, side, 0], qv_ref[t, side, 1]), 1)
        for n in range(nblk):
            if n > 0:
                values_stage(key_block(n - 1), (n - 1) % 2)
            softmax_stage(key_block(n), n % ATTN_SLOTS, n % 2)
            if n + 2 < nblk:
                ja = key_block(n + 2)
                side = (ja > qi).astype(jnp.int32)
                scores_stage(ja, (qv_ref[t, side, 0], qv_ref[t, side, 1]), (n + 2) % ATTN_SLOTS)
        values_stage(key_block(nblk - 1), (nblk - 1) % 2)
        a1, a2 = acc_ref[0], acc_ref[1]
        finish(t, a1[0:V_DIM] / a1[V_DIM:V_DIM + 1] - lam * (a2[0:V_DIM] / a2[V_DIM:V_DIM + 1]))

    @pl.when(jnp.logical_not(shifted_ok))
    def _():
        for t in range(tiles):
            general_tile(t)


def _attention(q, k, vt, lamv, sub_g, q_g, k_g, w_up, w_down, slopes, lam_init, batch, seq):
    smem = pl.BlockSpec(memory_space=pltpu.SMEM)
    blk = ATTN_BLOCK
    nq = seq // blk
    tiles = ATTN_TILES
    steps = nq // tiles
    k3 = k.reshape(batch, seq, D_MODEL)
    shifted_heads = {}
    for h in range(N_HEADS):
        slope = 2.0 ** (-8.0 * (h + 1) / N_HEADS)
        dk = math.ceil((ATTN_ZERO_LOG2 / (slope * LOG2E) - 1.0) / blk)
        shifted_heads[h] = dk if 2 * dk + 1 < nq else None

    n_steps = batch * N_HEADS * steps

    def row_chunks(w):
        rows = BF16_SUBLANES
        while w.shape[0] // rows > n_steps:
            rows *= 2
        last = w.shape[0] // rows - 1
        return pl.BlockSpec((rows, w.shape[1]),
                            lambda b, h, i: (jnp.minimum((b * N_HEADS + h) * steps + i, last), 0))

    return pl.pallas_call(
        functools.partial(_attn_kernel, shifted_heads=shifted_heads),
        grid=(batch, N_HEADS, steps),
        in_specs=[
            smem, smem,
            pl.BlockSpec((tiles * blk, V_DIM), lambda b, h, i: (b * steps + i, h)),
            pl.BlockSpec((None, seq, V_DIM), lambda b, h, i: (b, 0, h)),
            pl.BlockSpec((V_DIM, seq), lambda b, h, i: (h, b)),
            pl.BlockSpec((4, HEAD_DIM), lambda b, h, i: (0, 0)),
            pl.BlockSpec((1, V_DIM), lambda b, h, i: (0, 0)),
            pl.BlockSpec((1, HEAD_DIM), lambda b, h, i: (0, 0)),
            pl.BlockSpec((1, HEAD_DIM), lambda b, h, i: (0, 0)),
            row_chunks(w_up), row_chunks(w_down),
        ],
        out_specs=[pl.BlockSpec((tiles * blk, V_DIM), lambda b, h, i: (b * steps + i, h)),
                   row_chunks(w_up), row_chunks(w_down)],
        out_shape=[jax.ShapeDtypeStruct((batch * seq, D_MODEL), BF16),
                   jax.ShapeDtypeStruct(w_up.shape, BF16), jax.ShapeDtypeStruct(w_down.shape, BF16)],
        scratch_shapes=[
            pltpu.VMEM((seq, V_DIM), BF16),
            pltpu.VMEM((seq, V_DIM), BF16),
            pltpu.VMEM((ATTN_VT_ROWS, seq), BF16),
            pltpu.VMEM((blk, blk), F32),
            pltpu.VMEM((8, 128), F32),
            pltpu.SMEM((1,), jnp.int32),
            pltpu.VMEM((tiles, 3, 2, V_DIM, blk), BF16),
            pltpu.VMEM((ATTN_SLOTS, 2, blk, blk), F32),
            pltpu.VMEM((tiles, nq, 2, blk, blk), BF16),
            pltpu.VMEM((2, 2, 1, blk), F32),
            pltpu.VMEM((2, 1, blk), F32),
            pltpu.VMEM((2, ATTN_VT_ROWS, blk), F32),
        ],
        compiler_params=pltpu.CompilerParams(
            dimension_semantics=("arbitrary", "arbitrary", "arbitrary"),
            vmem_limit_bytes=V7X_VMEM_LIMIT),
        name="diff_attn",
    )(slopes, lam_init, q, k3, vt, lamv, sub_g, q_g, k_g, w_up, w_down)


def _merge_kernel(x_ref, oa_ref, u_ref, sv_ref, ga_ref, gb_ref, sgw_ref, sgb_ref,
                  wa_ref, wb_ref, wo_ref, o_ref, ob_ref):
    for c in range(MERGE_ROWS // CHUNK):
        rows = slice(c * CHUNK, (c + 1) * CHUNK)
        for g in range(N_GROUPS):
            cols = slice(g * GROUP_DIM, (g + 1) * GROUP_DIM)
            mixed = jnp.dot(sgw_ref[g], sv_ref[rows, cols], preferred_element_type=F32)
            ob_ref[rows, cols] = (u_ref[rows, cols].astype(F32) * (mixed + sgb_ref[:, cols])).astype(BF16)
    ya = jnp.dot(oa_ref[...], wa_ref[...], preferred_element_type=F32)
    yb = jnp.dot(ob_ref[...], wb_ref[...], preferred_element_type=F32)
    merged = (ga_ref[...].astype(F32) * ya + gb_ref[...].astype(F32) * yb).astype(BF16)
    o_ref[...] = x_ref[...] + jnp.dot(merged, wo_ref[...], preferred_element_type=F32)


def _merge(x2, oa, u, sv, ga, gb, sgw, sgb_full, wa, wb, wo):
    t = x2.shape[0]
    row = lambda i: (i, 0)
    fixed2 = lambda i: (0, 0)
    tile = pl.BlockSpec((MERGE_ROWS, D_MODEL), row)
    wspec = pl.BlockSpec((D_MODEL, D_MODEL), fixed2, pipeline_mode=pl.Buffered(1))
    return pl.pallas_call(
        _merge_kernel,
        grid=(t // MERGE_ROWS,),
        in_specs=[
            tile, tile, tile, tile, tile, tile,
            pl.BlockSpec((N_GROUPS, CHUNK, CHUNK), lambda i: (0, 0, 0)),
            pl.BlockSpec((CHUNK, D_MODEL), fixed2),
            wspec, wspec, wspec,
        ],
        out_specs=tile,
        out_shape=jax.ShapeDtypeStruct((t, D_MODEL), F32),
        scratch_shapes=[pltpu.VMEM((MERGE_ROWS, D_MODEL), BF16)],
        compiler_params=pltpu.CompilerParams(
            dimension_semantics=("arbitrary",), vmem_limit_bytes=V7X_VMEM_LIMIT),
        name="sgu_merge",
    )(x2, oa, u, sv, ga, gb, sgw, sgb_full, wa, wb, wo)


def _ffn_kernel(x_ref, prev_ref, next_ref, g_ref, wup_ref, cw_ref, cb_ref, wdn_ref,
                o_ref, h_ref, act_ref, *, blocks_per_seq):
    i = pl.program_id(0)
    tm, halo = FFN_ROWS, FFN_HALO
    g = g_ref[...]
    x = x_ref[...]
    keep_prev = (i % blocks_per_seq != 0).astype(F32)
    keep_next = ((i + 1) % blocks_per_seq != 0).astype(F32)
    h_ref[0:halo, :] = (_rms_rows(prev_ref[...], g) * keep_prev).astype(BF16)
    h_ref[halo:halo + tm, :] = _rms_rows(x, g).astype(BF16)
    h_ref[halo + tm:, :] = (_rms_rows(next_ref[...], g) * keep_next).astype(BF16)
    h = h_ref[...]
    ext = tm + 2 * halo

    def conv(cols):
        up = jnp.dot(h, wup_ref[:, cols], preferred_element_type=F32)
        y = (pltpu.roll(up, 1, 0) * cw_ref[0:1, cols] + up * cw_ref[1:2, cols]
             + pltpu.roll(up, ext - 1, 0) * cw_ref[2:3, cols])
        return y[halo:halo + tm] + cb_ref[:, cols]

    for lo in range(0, D_FF, FFN_COLS):
        hi = min(lo + FFN_COLS, D_FF)
        gate = conv(slice(lo, hi))
        val = conv(slice(D_FF + lo, D_FF + hi))
        act_ref[:, lo:hi] = (gate * _sigmoid(gate) * val).astype(BF16)
    o_ref[...] = x + jnp.dot(act_ref[...], wdn_ref[...], preferred_element_type=F32)


def _ffn(x2, ln_g, wup, conv_w, conv_b, wdn, seq):
    t = x2.shape[0]
    tm, halo = FFN_ROWS, FFN_HALO
    r = tm // halo
    last = t // halo - 1
    fixed = lambda i: (0, 0)
    return pl.pallas_call(
        functools.partial(_ffn_kernel, blocks_per_seq=seq // tm),
        grid=(t // tm,),
        in_specs=[
            pl.BlockSpec((tm, D_MODEL), lambda i: (i, 0)),
            pl.BlockSpec((halo, D_MODEL), lambda i: (jnp.maximum(i * r - 1, 0), 0)),
            pl.BlockSpec((halo, D_MODEL), lambda i: (jnp.minimum((i + 1) * r, last), 0)),
            pl.BlockSpec((1, D_MODEL), fixed),
            pl.BlockSpec((D_MODEL, 2 * D_FF), fixed, pipeline_mode=pl.Buffered(1)),
            pl.BlockSpec((3, 2 * D_FF), fixed),
            pl.BlockSpec((1, 2 * D_FF), fixed),
            pl.BlockSpec((D_FF, D_MODEL), fixed, pipeline_mode=pl.Buffered(1)),
        ],
        out_specs=pl.BlockSpec((tm, D_MODEL), lambda i: (i, 0)),
        out_shape=jax.ShapeDtypeStruct((t, D_MODEL), F32),
        scratch_shapes=[pltpu.VMEM((tm + 2 * halo, D_MODEL), BF16),
                        pltpu.VMEM((tm, D_FF), BF16)],
        compiler_params=pltpu.CompilerParams(
            dimension_semantics=("arbitrary",), vmem_limit_bytes=V7X_VMEM_LIMIT),
        name="conv_ffn",
    )(x2, x2, x2, ln_g, wup, conv_w, conv_b, wdn)


def kernel(x, ln_mix_g, w_in, q_norm_g, k_norm_g, lambda_q1, lambda_k1, lambda_q2, lambda_k2,
           subln_g, sg_norm_g, sg_w, sg_b, w_branch_a, w_branch_b, w_out, ln_ffn_g, w_up,
           conv_w, conv_b, w_down):
    batch, seq, d = x.shape
    depth = w_in.shape[0]
    slopes = 2.0 ** (-8.0 * jnp.arange(1, N_HEADS + 1, dtype=F32) / N_HEADS)
    x2 = x.reshape(batch * seq, d)
    for l in range(depth):
        lam_init = jnp.full((1,), 0.8 - 0.6 * math.exp(-0.3 * l), F32)
        q, k, vt, u, sv, ga, gb = _projection(
            x2, ln_mix_g[l][None], w_in[l],
            jnp.tile(q_norm_g[l], 2 * N_HEADS)[None], jnp.tile(k_norm_g[l], 2 * N_HEADS)[None],
            sg_norm_g[l][None])
        lamv = jnp.stack([lambda_q1[l], lambda_k1[l], lambda_q2[l], lambda_k2[l]])
        oa, w_up_bf, w_down_bf = _attention(
            q, k, vt, lamv, subln_g[l][None], q_norm_g[l][None], k_norm_g[l][None],
            w_up[l], w_down[l], slopes, lam_init, batch, seq)
        sgb_full = jnp.repeat(sg_b[l].T, GROUP_DIM, axis=1)
        x2 = _merge(x2, oa, u, sv, ga, gb, sg_w[l].astype(BF16), sgb_full,
                    w_branch_a[l].astype(BF16), w_branch_b[l].astype(BF16), w_out[l].astype(BF16))
        x2 = _ffn(x2, ln_ffn_g[l][None], w_up_bf, conv_w[l], conv_b[l][None], w_down_bf, seq)
    return x2.reshape(batch, seq, d)
```
